```python
import math
import jax
import jax.numpy as jnp
from jax import lax
import numpy as np

D_MODEL = 1024
BATCH = 1
SEQ = 16384
DEPTH = 4

CTX_LEN = 256
GRID_W = 64

N_HEADS = 16
N_KV_HEADS = 4
HEAD_DIM = 64
GROUP = N_HEADS // N_KV_HEADS
WINDOW = 128
BLOCK = 128
ROPE_BASE = 10000.0
ROPE_AXIS_DIM = HEAD_DIM // 2

GLA_HEADS = 4
GLA_DK = (D_MODEL // 2) // GLA_HEADS
GLA_DV = D_MODEL // GLA_HEADS
GLA_RANK = 16
GLA_NORMALIZER = 16.0
GLA_CHUNK = 64

D_FF = 4 * D_MODEL
N_MOD = 6
EPS = 1e-6

ATT_Q = N_HEADS * HEAD_DIM
ATT_KV = N_KV_HEADS * HEAD_DIM
GLA_K = GLA_HEADS * GLA_DK
GLA_V = GLA_HEADS * GLA_DV
IN_SIZES = (ATT_Q, ATT_KV, ATT_KV, GLA_K, GLA_K, GLA_V, GLA_V, 2 * GLA_RANK, D_MODEL, D_MODEL)
D_IN = ATT_Q + 2 * ATT_KV + 2 * GLA_K + 2 * GLA_V + 2 * GLA_RANK + 2 * D_MODEL

kernel_name = "hybrid_gqa_gla_prefix_dit"

NEG = -1e30


def rms_norm(x, g):
    xf = x.astype(jnp.float32)
    y = xf * lax.rsqrt(jnp.mean(xf * xf, axis=-1, keepdims=True) + EPS)
    return (y * g.astype(jnp.float32)).astype(x.dtype)


def modulate(h, shift, scale):
    return h * (1.0 + scale) + shift


def split_in(z):
    out, idx = [], 0
    for s in IN_SIZES:
        out.append(z[..., idx:idx + s])
        idx += s
    return out


def rope_1d(x, pos):
    d = x.shape[-1]
    half = d // 2
    inv = ROPE_BASE ** (-jnp.arange(half, dtype=jnp.float32) * 2.0 / d)
    ang = pos.astype(jnp.float32)[:, None] * inv[None, :]
    cos = jnp.cos(ang)[None, :, None, :]
    sin = jnp.sin(ang)[None, :, None, :]
    xf = x.astype(jnp.float32)
    x1, x2 = xf[..., :half], xf[..., half:]
    return jnp.concatenate([x1 * cos - x2 * sin, x2 * cos + x1 * sin], axis=-1).astype(x.dtype)


def rope_2d(x, row, col):
    return jnp.concatenate([rope_1d(x[..., :ROPE_AXIS_DIM], row),
                            rope_1d(x[..., ROPE_AXIS_DIM:], col)], axis=-1)


def window_attention(q, k, v, kc, vc, sink):
    B, N = q.shape[0], q.shape[1]
    L = kc.shape[1]
    nb = N // BLOCK
    scale = HEAD_DIM ** -0.5
    qb = q.reshape(B, nb, BLOCK, N_KV_HEADS, GROUP, HEAD_DIM)
    pad = ((0, 0), (BLOCK, BLOCK), (0, 0), (0, 0))
    kp = jnp.pad(k, pad).reshape(B, nb + 2, BLOCK, N_KV_HEADS, HEAD_DIM)
    vp = jnp.pad(v, pad).reshape(B, nb + 2, BLOCK, N_KV_HEADS, HEAD_DIM)
    kw = jnp.concatenate([kp[:, :-2], kp[:, 1:-1], kp[:, 2:]], axis=2)
    vw = jnp.concatenate([vp[:, :-2], vp[:, 1:-1], vp[:, 2:]], axis=2)
    s_loc = jnp.einsum('bnqkgd,bnskd->bnkgqs', qb, kw).astype(jnp.float32) * scale
    qpos = jnp.arange(nb)[:, None] * BLOCK + jnp.arange(BLOCK)[None, :]
    kpos = (jnp.arange(nb)[:, None] - 1) * BLOCK + jnp.arange(3 * BLOCK)[None, :]
    valid = (jnp.abs(qpos[:, :, None] - kpos[:, None, :]) <= WINDOW) \
        & (kpos[:, None, :] >= 0) & (kpos[:, None, :] < N)
    s_loc = jnp.where(valid[None, :, None, None], s_loc, NEG)
    s_ctx = jnp.einsum('bnqkgd,bckd->bnkgqc', qb, kc).astype(jnp.float32) * scale
    s_sink = jnp.broadcast_to(sink.astype(jnp.float32).reshape(1, 1, N_KV_HEADS, GROUP, 1, 1),
                              (B, nb, N_KV_HEADS, GROUP, BLOCK, 1))
    p = jax.nn.softmax(jnp.concatenate([s_loc, s_ctx, s_sink], axis=-1), axis=-1)
    p_loc = p[..., :3 * BLOCK].astype(v.dtype)
    p_ctx = p[..., 3 * BLOCK:3 * BLOCK + L].astype(v.dtype)
    o = jnp.einsum('bnkgqs,bnskd->bnqkgd', p_loc, vw) + jnp.einsum('bnkgqc,bckd->bnqkgd', p_ctx, vc)
    return o.reshape(B, N, ATT_Q)


def context_attention(qc, kc, vc, sink):
    B, L = qc.shape[0], qc.shape[1]
    scale = HEAD_DIM ** -0.5
    qg = qc.reshape(B, L, N_KV_HEADS, GROUP, HEAD_DIM)
    s = jnp.einsum('bqkgd,bckd->bkgqc', qg, kc).astype(jnp.float32) * scale
    s_sink = jnp.broadcast_to(sink.astype(jnp.float32).reshape(1, N_KV_HEADS, GROUP, 1, 1),
                              (B, N_KV_HEADS, GROUP, L, 1))
    p = jax.nn.softmax(jnp.concatenate([s, s_sink], axis=-1), axis=-1)[..., :L].astype(vc.dtype)
    o = jnp.einsum('bkgqc,bckd->bqkgd', p, vc)
    return o.reshape(B, L, ATT_Q)


def gla_chunked(q, k, v, g, s0):
    B, H, N, dk = q.shape
    dv = v.shape[-1]
    C = GLA_CHUNK
    nc = N // C
    q = q.reshape(B, H, nc, C, dk)
    k = k.reshape(B, H, nc, C, dk)
    v = v.reshape(B, H, nc, C, dv)
    b = jnp.cumsum(g.reshape(B, H, nc, C, dk), axis=3)
    b_last = b[..., -1:, :]
    q_in = q * jnp.exp(b)
    k_in = k * jnp.exp(-b)
    k_out = k * jnp.exp(b_last - b)
    a = jnp.einsum('bhnid,bhnjd->bhnij', q_in, k_in)
    a = jnp.where(jnp.tril(jnp.ones((C, C), dtype=bool)), a, 0.0)
    o_intra = jnp.einsum('bhnij,bhnje->bhnie', a, v)

    def step(S, inp):
        qi, ko, vi, dl = inp
        o = jnp.einsum('bhid,bhde->bhie', qi, S)
        S = S * dl[..., None] + jnp.einsum('bhjd,bhje->bhde', ko, vi)
        return S, o

    xs = (jnp.moveaxis(q_in, 2, 0), jnp.moveaxis(k_out, 2, 0), jnp.moveaxis(v, 2, 0),
          jnp.moveaxis(jnp.exp(b_last[..., 0, :]), 2, 0))
    s_fin, o_inter = lax.scan(step, s0, xs)
    o = o_intra + jnp.moveaxis(o_inter, 0, 2)
    return o.reshape(B, H, N, dv), s_fin


def gla_inputs(gq, gk, gv, ga, w_decay, b_decay):
    B, N = gq.shape[0], gq.shape[1]

    def heads(t, d):
        return t.astype(jnp.float32).reshape(B, N, GLA_HEADS, d).transpose(0, 2, 1, 3)

    q = heads(gq, GLA_DK) * (GLA_DK ** -0.5)
    k = heads(gk, GLA_DK)
    v = heads(gv, GLA_DV)
    g_f = jax.nn.log_sigmoid((ga[..., :GLA_RANK] @ w_decay[0] + b_decay[0]).astype(jnp.float32)) / GLA_NORMALIZER
    g_b = jax.nn.log_sigmoid((ga[..., GLA_RANK:] @ w_decay[1] + b_decay[1]).astype(jnp.float32)) / GLA_NORMALIZER
    return q, k, v, heads(g_f, GLA_DK), heads(g_b, GLA_DK)


def bidir_gla(q, k, v, g_f, g_b, s0_f, s0_b):
    o_f, s_f = gla_chunked(q, k, v, g_f, s0_f)
    flip = lambda t: jnp.flip(t, axis=2)
    o_b, s_b = gla_chunked(flip(q), flip(k), flip(v), flip(g_b), s0_b)
    return o_f + flip(o_b), s_f, s_b


def gla_output(o, gr, gain):
    B, H, N, dv = o.shape
    o = rms_norm(o.transpose(0, 2, 1, 3), gain).reshape(B, N, H * dv)
    return o.astype(gr.dtype) * jax.nn.silu(gr)


def squared_relu_mlp(h, w1, w2):
    return jnp.square(jax.nn.relu(h @ w1)) @ w2


def setup_inputs(seed: int = 0) -> dict:
    key = jax.random.key(seed)
    ks = jax.random.split(key, 20)
    nrm = lambda k, shape, s: jax.random.normal(k, shape, dtype=jnp.float32) * s
    D = D_MODEL
    return {
        "x": nrm(ks[0], (BATCH, SEQ, D), 1.0),
        "c": nrm(ks[1], (BATCH, D), 1.0),
        "ctx": nrm(ks[2], (BATCH, CTX_LEN, D), 1.0),
        "c_ctx": nrm(ks[3], (D,), 1.0),
        "w_mod": nrm(ks[4], (DEPTH, D, N_MOD * D), 0.5 * D ** -0.5),
        "b_mod": nrm(ks[5], (DEPTH, N_MOD * D), 0.02),
        "g_norm1": 1.0 + nrm(ks[6], (DEPTH, D), 0.1),
        "w_in": nrm(ks[7], (DEPTH, D, D_IN), D ** -0.5),
        "q_gain": 1.0 + nrm(ks[8], (DEPTH, HEAD_DIM), 0.1),
        "k_gain": 1.0 + nrm(ks[9], (DEPTH, HEAD_DIM), 0.1),
        "sink": nrm(ks[10], (DEPTH, N_HEADS), 0.5),
        "w_decay": nrm(ks[11], (DEPTH, 2, GLA_RANK, GLA_K), GLA_RANK ** -0.5),
        "b_decay": nrm(ks[12], (DEPTH, 2, GLA_K), 0.5),
        "gla_gain": 1.0 + nrm(ks[13], (DEPTH, GLA_DV), 0.1),
        "w_branch_attn": nrm(ks[14], (DEPTH, ATT_Q, D), ATT_Q ** -0.5),
        "w_branch_gla": nrm(ks[15], (DEPTH, GLA_V, D), GLA_V ** -0.5),
        "w_out": nrm(ks[16], (DEPTH, D, D), D ** -0.5),
        "g_norm2": 1.0 + nrm(ks[17], (DEPTH, D), 0.1),
        "w_ff1": nrm(ks[18], (DEPTH, D, D_FF), D ** -0.5),
        "w_ff2": nrm(ks[19], (DEPTH, D_FF, D), D_FF ** -0.5),
    }


def reference(x, c, ctx, c_ctx, w_mod, b_mod, g_norm1, w_in, q_gain, k_gain, sink, w_decay, b_decay,
              gla_gain, w_branch_attn, w_branch_gla, w_out, g_norm2, w_ff1, w_ff2):
    B, N = x.shape[0], x.shape[1]
    L = ctx.shape[1]
    ROWS = N // GRID_W
    row = jnp.repeat(jnp.arange(ROWS, dtype=jnp.int32), GRID_W)
    col = jnp.tile(jnp.arange(GRID_W, dtype=jnp.int32), ROWS)
    silu_c = jax.nn.silu(c)
    silu_cc = jax.nn.silu(c_ctx)
    xc = ctx
    s_zero = jnp.zeros((B, GLA_HEADS, GLA_DK, GLA_DV), jnp.float32)

    for l in range(DEPTH):
        last = l == DEPTH - 1
        mod = (silu_c @ w_mod[l] + b_mod[l])[:, None, :]
        mod_c = (silu_cc @ w_mod[l] + b_mod[l])[None, None, :]
        sh1, sc1, gt1, sh2, sc2, gt2 = jnp.split(mod, N_MOD, axis=-1)
        sh1c, sc1c, gt1c, sh2c, sc2c, gt2c = jnp.split(mod_c, N_MOD, axis=-1)

        h = modulate(rms_norm(x, g_norm1[l]), sh1, sc1)
        hc = modulate(rms_norm(xc, g_norm1[l]), sh1c, sc1c)
        aq, ak, av, gq, gk, gv, gr, ga, gate_a, gate_b = split_in(h @ w_in[l])
        aqc, akc, avc, gqc, gkc, gvc, grc, gac, gate_ac, gate_bc = split_in(hc @ w_in[l])

        q = rope_2d(rms_norm(aq.reshape(B, N, N_HEADS, HEAD_DIM), q_gain[l]), row, col)
        k = rope_2d(rms_norm(ak.reshape(B, N, N_KV_HEADS, HEAD_DIM), k_gain[l]), row, col)
        v = av.reshape(B, N, N_KV_HEADS, HEAD_DIM)
        kc = rms_norm(akc.reshape(B, L, N_KV_HEADS, HEAD_DIM), k_gain[l])
        vc = avc.reshape(B, L, N_KV_HEADS, HEAD_DIM)
        attn = window_attention(q, k, v, kc, vc, sink[l])

        qg_c, kg_c, vg_c, gf_c, gb_c = gla_inputs(gqc, gkc, gvc, gac, w_decay[l], b_decay[l])
        o_c, s_f, s_b = bidir_gla(qg_c, kg_c, vg_c, gf_c, gb_c, s_zero, s_zero)
        qg, kg, vg, gf, gb = gla_inputs(gq, gk, gv, ga, w_decay[l], b_decay[l])
        o_l, _, _ = bidir_gla(qg, kg, vg, gf, gb, s_f, s_b)
        gla = gla_output(o_l, gr, gla_gain[l])

        y = jax.nn.sigmoid(gate_a) * (attn @ w_branch_attn[l]) + jax.nn.sigmoid(gate_b) * (gla @ w_branch_gla[l])
        x = x + gt1 * (y @ w_out[l])

        x = x + gt2 * squared_relu_mlp(modulate(rms_norm(x, g_norm2[l]), sh2, sc2), w_ff1[l], w_ff2[l])

        if not last:
            qc = rms_norm(aqc.reshape(B, L, N_HEADS, HEAD_DIM), q_gain[l])
            attn_c = context_attention(qc, kc, vc, sink[l])
            gla_c = gla_output(o_c, grc, gla_gain[l])
            yc = jax.nn.sigmoid(gate_ac) * (attn_c @ w_branch_attn[l]) \
                + jax.nn.sigmoid(gate_bc) * (gla_c @ w_branch_gla[l])
            xc = xc + gt1c * (yc @ w_out[l])
            xc = xc + gt2c * squared_relu_mlp(modulate(rms_norm(xc, g_norm2[l]), sh2c, sc2c), w_ff1[l], w_ff2[l])

    return x
```

```python
import functools

import numpy as np
import jax
import jax.numpy as jnp
from jax import lax
from jax.experimental import pallas as pl
from jax.experimental.pallas import tpu as pltpu

F32 = jnp.float32
BF16 = jnp.bfloat16

D_MODEL = 1024
N_HEADS = 16
N_KV_HEADS = 4
HEAD_DIM = 64
GROUP = N_HEADS // N_KV_HEADS
ATT_BLOCK = 128
GRID_W = 64
ROPE_BASE = 10000.0
ROPE_AXIS_DIM = HEAD_DIM // 2
GLA_HEADS = 4
GLA_DK = 128
GLA_DV = 256
GLA_RANK = 16
GLA_NORMALIZER = 16.0
GLA_CHUNK = 64
D_FF = 4 * D_MODEL
N_MOD = 6
EPS = 1e-6
NEG = -1e30

ATT_Q = N_HEADS * HEAD_DIM
ATT_KV = N_KV_HEADS * HEAD_DIM
GLA_K = GLA_HEADS * GLA_DK
GLA_V = GLA_HEADS * GLA_DV

LANES = 128
TM = 256
CHUNKS_PER_TILE = TM // GLA_CHUNK
VMEM_LIMIT = 56 * 1024 * 1024


def _cparams(n_axes):
    return pltpu.CompilerParams(dimension_semantics=("arbitrary",) * n_axes,
                                vmem_limit_bytes=VMEM_LIMIT)


def _dot(a, b):
    return jnp.dot(a, b, preferred_element_type=F32)


def _dot_nt(a, b):
    return lax.dot_general(a, b, (((1,), (1,)), ((), ())), preferred_element_type=F32)


def _split3(x):
    x1 = x.astype(BF16)
    r1 = x - x1.astype(F32)
    x2 = r1.astype(BF16)
    x3 = (r1 - x2.astype(F32)).astype(BF16)
    return x1, x2, x3


MOD_TN = 1536


def _mod_kernel(c_ref, w_ref, b_ref, o_ref):
    c = c_ref[...]
    s = c * jax.nn.sigmoid(c)
    s1, s2, _ = _split3(s)
    w = w_ref[0]
    w1 = w.astype(BF16)
    w2 = (w - w1.astype(F32)).astype(BF16)
    o_ref[0] = _dot(s1, w1) + _dot(s2, w1) + _dot(s1, w2) + b_ref[0]


def _mod_vectors(cvec, w_mod, b_mod):
    depth = w_mod.shape[0]
    n_out = w_mod.shape[2]
    return pl.pallas_call(
        _mod_kernel,
        grid=(depth, n_out // MOD_TN),
        in_specs=[pl.BlockSpec((8, D_MODEL), lambda l, j: (0, 0)),
                  pl.BlockSpec((1, D_MODEL, MOD_TN), lambda l, j: (l, 0, j)),
                  pl.BlockSpec((1, 1, MOD_TN), lambda l, j: (l, 0, j))],
        out_specs=pl.BlockSpec((1, 8, MOD_TN), lambda l, j: (l, 0, j)),
        out_shape=jax.ShapeDtypeStruct((depth, 8, n_out), F32),
        compiler_params=_cparams(2),
        name="mod_vectors",
    )(cvec, w_mod, b_mod.reshape(depth, 1, n_out))


def _mod_row_spec():
    return pl.BlockSpec((1, 1, N_MOD * D_MODEL), lambda i: (jnp.where(i == 0, 1, 0), 0, 0))


def _norm_modulate(x, g, shift, scale):
    ms = jnp.mean(x * x, axis=-1, keepdims=True)
    y = x * lax.rsqrt(ms + EPS) * g
    return y * (1.0 + scale) + shift


C_Q = 0
C_K = C_Q + ATT_Q
C_V = C_K + ATT_KV
C_GQ = C_V + ATT_KV
C_GK = C_GQ + GLA_K
C_GV = C_GK + GLA_K
C_GA = C_GV + GLA_V
C_END = C_GA + LANES


def _headnorm_rope(z, gain, cos, sin, lane):
    lo = lane < HEAD_DIM
    sq = z * z
    s_lo = jnp.sum(jnp.where(lo, sq, 0.0), axis=-1, keepdims=True)
    s_hi = jnp.sum(jnp.where(lo, 0.0, sq), axis=-1, keepdims=True)
    ms = jnp.where(lo, s_lo, s_hi) * (1.0 / HEAD_DIM)
    y = z * lax.rsqrt(ms + EPS) * gain
    half = ROPE_AXIS_DIM // 2
    nxt = pltpu.roll(y, LANES - half, 1)
    prv = pltpu.roll(y, half, 1)
    partner = jnp.where((lane % ROPE_AXIS_DIM) < half, nxt, prv)
    return y * cos + partner * sin


def _in_proj_kernel(x_ref, mod_ref, g1_ref, w_ref, qg_ref, kg_ref, cos_ref, sin_ref,
                    q_ref, k_ref, vt_ref, gq_ref, gk_ref, gv_ref, ga_ref):
    mod = mod_ref[0]
    h = _norm_modulate(x_ref[...], g1_ref[...], mod[:, 0:D_MODEL], mod[:, D_MODEL:2 * D_MODEL])
    h = h.astype(BF16)
    lane = lax.broadcasted_iota(jnp.int32, (TM, LANES), 1)
    cos = cos_ref[...]
    sin = sin_ref[...]
    qg = qg_ref[...]
    kg = kg_ref[...]
    for j in range(ATT_Q // 256):
        z = _dot(h, w_ref[:, C_Q + j * 256:C_Q + (j + 1) * 256])
        for e in range(2):
            r = _headnorm_rope(z[:, e * LANES:(e + 1) * LANES], qg, cos, sin, lane)
            c0 = j * 256 + e * LANES
            q_ref[:, c0:c0 + LANES] = (r * (HEAD_DIM ** -0.5)).astype(BF16)
    z = _dot(h, w_ref[:, C_K:C_K + ATT_KV])
    for e in range(2):
        r = _headnorm_rope(z[:, e * LANES:(e + 1) * LANES], kg, cos, sin, lane)
        k_ref[:, e * LANES:(e + 1) * LANES] = r.astype(BF16)
    z = _dot(h, w_ref[:, C_V:C_V + ATT_KV])
    vt_ref[...] = z.T.astype(BF16)
    gq_ref[...] = _dot(h, w_ref[:, C_GQ:C_GQ + GLA_K])
    gk_ref[...] = _dot(h, w_ref[:, C_GK:C_GK + GLA_K])
    for j in range(GLA_V // 512):
        gv_ref[:, j * 512:(j + 1) * 512] = _dot(h, w_ref[:, C_GV + j * 512:C_GV + (j + 1) * 512]).astype(BF16)
    ga_ref[...] = _dot(h, w_ref[:, C_GA:C_GA + LANES]).astype(BF16)


def _in_proj(xc, mod2, g1, w_a, qg, kg, cos, sin):
    t_tot = xc.shape[0]
    nt = t_tot // TM
    row = lambda w: pl.BlockSpec((TM, w), lambda i: (i, 0))
    full = lambda a: pl.BlockSpec(a.shape, lambda i: (0,) * a.ndim)
    return pl.pallas_call(
        _in_proj_kernel,
        grid=(nt,),
        in_specs=[row(D_MODEL), _mod_row_spec(), full(g1), full(w_a), full(qg), full(kg),
                  row(LANES), row(LANES)],
        out_specs=[row(ATT_Q), row(ATT_KV), pl.BlockSpec((ATT_KV, TM), lambda i: (0, i)),
                   row(GLA_K), row(GLA_K), row(GLA_V), row(LANES)],
        out_shape=[jax.ShapeDtypeStruct((t_tot, ATT_Q), BF16),
                   jax.ShapeDtypeStruct((t_tot, ATT_KV), BF16),
                   jax.ShapeDtypeStruct((ATT_KV, t_tot), BF16),
                   jax.ShapeDtypeStruct((t_tot, GLA_K), F32),
                   jax.ShapeDtypeStruct((t_tot, GLA_K), F32),
                   jax.ShapeDtypeStruct((t_tot, GLA_V), BF16),
                   jax.ShapeDtypeStruct((t_tot, LANES), BF16)],
        compiler_params=_cparams(1),
        name="in_proj",
    )(xc, mod2, g1, w_a, qg, kg, cos, sin)


def _attn_kernel(q_ref, kp_ref, kc_ref, kn_ref, kx_ref, vp_ref, vc_ref, vn_ref, vx_ref, sink_ref,
                 o_ref, *, n_blocks, ctx_blocks):
    b = pl.program_id(0)
    is_lat = b >= ctx_blocks
    prev_ok = b >= ctx_blocks + 1
    next_ok = jnp.logical_and(is_lat, b <= n_blocks - 2)
    gw = GROUP * ATT_BLOCK
    key_i = lax.broadcasted_iota(jnp.int32, (ATT_BLOCK, gw), 0)
    qry_i = lax.broadcasted_iota(jnp.int32, (ATT_BLOCK, gw), 1) % ATT_BLOCK
    m_prev = jnp.logical_and(key_i >= qry_i, prev_ok)
    m_next = jnp.logical_and(key_i <= qry_i, next_ok)
    m_cur = jnp.logical_and(key_i >= 0, is_lat)
    lane_q = lax.broadcasted_iota(jnp.int32, (ATT_BLOCK, LANES), 1)
    n_ctx = kx_ref.shape[0]
    for j in range(N_KV_HEADS // 2):
        cs = slice(j * LANES, (j + 1) * LANES)
        k_all = jnp.concatenate([kp_ref[:, cs], kc_ref[:, cs], kn_ref[:, cs], kx_ref[:, cs]], axis=0)
        for e in range(2):
            g = 2 * j + e
            half = (lane_q < HEAD_DIM) if e == 0 else (lane_q >= HEAD_DIM)
            qs = []
            for hh in range(GROUP):
                qc = q_ref[:, (j * GROUP + hh) * LANES:(j * GROUP + hh + 1) * LANES]
                qs.append(jnp.where(half, qc, jnp.zeros_like(qc)))
            qg = jnp.concatenate(qs, axis=0)
            s = _dot_nt(k_all, qg)
            s_p = jnp.where(m_prev, s[0:ATT_BLOCK], NEG)
            s_c = jnp.where(m_cur, s[ATT_BLOCK:2 * ATT_BLOCK], NEG)
            s_n = jnp.where(m_next, s[2 * ATT_BLOCK:3 * ATT_BLOCK], NEG)
            s_x = s[3 * ATT_BLOCK:3 * ATT_BLOCK + n_ctx]
            sk = sink_ref[g]
            cmax = lambda a: jnp.max(a, axis=0, keepdims=True)
            m = jnp.maximum(jnp.maximum(cmax(s_p), cmax(s_c)), jnp.maximum(cmax(s_n), cmax(s_x)))
            m = jnp.maximum(m, sk)
            p_p = jnp.exp(s_p - m)
            p_c = jnp.exp(s_c - m)
            p_n = jnp.exp(s_n - m)
            p_x = jnp.exp(s_x - m)
            csum = lambda a: jnp.sum(a, axis=0, keepdims=True)
            denom = csum(p_p) + csum(p_c) + csum(p_n) + csum(p_x) + jnp.exp(sk - m)
            p = jnp.concatenate([p_p, p_c, p_n, p_x], axis=0).astype(BF16)
            rs = slice(g * HEAD_DIM, (g + 1) * HEAD_DIM)
            vt_all = jnp.concatenate([vp_ref[rs, :], vc_ref[rs, :], vn_ref[rs, :], vx_ref[rs, :]], axis=1)
            ot = _dot(vt_all, p) * (1.0 / denom)
            for pair in range(GROUP // 2):
                x2 = jnp.concatenate([ot[:, (2 * pair) * ATT_BLOCK:(2 * pair + 1) * ATT_BLOCK],
                                      ot[:, (2 * pair + 1) * ATT_BLOCK:(2 * pair + 2) * ATT_BLOCK]], axis=0)
                c0 = g * GROUP * HEAD_DIM + pair * LANES
                o_ref[:, c0:c0 + LANES] = x2.T.astype(BF16)


def _window_attn(q, k, vt, sink_lanes, n_ctx):
    t_tot = q.shape[0]
    nb = t_tot // ATT_BLOCK
    cb = n_ctx // ATT_BLOCK
    prev_i = lambda b: jnp.clip(b - 1, cb, nb - 1)
    next_i = lambda b: jnp.clip(b + 1, cb, nb - 1)
    kspec = lambda f: pl.BlockSpec((ATT_BLOCK, ATT_KV), lambda b: (f(b), 0))
    vspec = lambda f: pl.BlockSpec((ATT_KV, ATT_BLOCK), lambda b: (0, f(b)))
    ident = lambda b: b
    return pl.pallas_call(
        functools.partial(_attn_kernel, n_blocks=nb, ctx_blocks=cb),
        grid=(nb,),
        in_specs=[pl.BlockSpec((ATT_BLOCK, ATT_Q), lambda b: (b, 0)),
                  kspec(prev_i), kspec(ident), kspec(next_i),
                  pl.BlockSpec((n_ctx, ATT_KV), lambda b: (0, 0)),
                  vspec(prev_i), vspec(ident), vspec(next_i),
                  pl.BlockSpec((ATT_KV, n_ctx), lambda b: (0, 0)),
                  pl.BlockSpec(sink_lanes.shape, lambda b: (0, 0, 0))],
        out_specs=pl.BlockSpec((ATT_BLOCK, ATT_Q), lambda b: (b, 0)),
        out_shape=jax.ShapeDtypeStruct((t_tot, ATT_Q), BF16),
        compiler_params=_cparams(1),
        name="window_attn",
    )(q, k, k, k, k, vt, vt, vt, vt, sink_lanes)


def _gla_direction(d, gq_ref, gk_ref, gv_ref, ga_ref, wd_ref, bd_ref, tri_ref, o_ref, s_ref):
    x = _dot(ga_ref[...], wd_ref[d]) + bd_ref[d]
    g = (jnp.minimum(x, 0.0) - jnp.log1p(jnp.exp(-jnp.abs(x)))) * (1.0 / GLA_NORMALIZER)
    g1, g2, g3 = _split3(g)
    tri = tri_ref[d]
    b = _dot(tri, g1) + _dot(tri, g2) + _dot(tri, g3)
    q_in = gq_ref[...] * (GLA_DK ** -0.5) * jnp.exp(b)
    gk = gk_ref[...]
    k_in = gk * jnp.exp(-b)
    ci = lax.broadcasted_iota(jnp.int32, (GLA_CHUNK, GLA_CHUNK), 0)
    cj = lax.broadcasted_iota(jnp.int32, (GLA_CHUNK, GLA_CHUNK), 1)
    causal = (cj <= ci) if d == 0 else (cj >= ci)
    order = range(CHUNKS_PER_TILE) if d == 0 else range(CHUNKS_PER_TILE - 1, -1, -1)
    for c in order:
        r0 = c * GLA_CHUNK
        rows = slice(r0, r0 + GLA_CHUNK)
        last = r0 + GLA_CHUNK - 1 if d == 0 else r0
        b_last = b[last:last + 1, :]
        k_out = gk[rows] * jnp.exp(b_last - b[rows])
        decay = jnp.exp(b_last)
        for hh in range(GLA_HEADS):
            kc = slice(hh * GLA_DK, (hh + 1) * GLA_DK)
            vc = slice(hh * GLA_DV, (hh + 1) * GLA_DV)
            qi = q_in[rows, kc].astype(BF16)
            ki = k_in[rows, kc].astype(BF16)
            v = gv_ref[rows, vc]
            a = jnp.where(causal, _dot_nt(qi, ki), 0.0).astype(BF16)
            s_old = s_ref[hh]
            o_ref[rows, vc] = _dot(a, v) + _dot(qi, s_old.astype(BF16))
            ko_t = k_out[:, kc].T.astype(BF16)
            decay_col = jnp.broadcast_to(decay[:, kc], (GLA_DK, GLA_DK)).T
            dcol = jnp.concatenate([decay_col] * (GLA_DV // GLA_DK), axis=1)
            s_ref[hh] = s_old * dcol + _dot(ko_t, v)


def _gla_kernel(gqf, gkf, gvf, gaf, gqb, gkb, gvb, gab, wd_ref, bd_ref, tri_ref,
                of_ref, ob_ref, sf_ref, sb_ref):
    @pl.when(pl.program_id(0) == 0)
    def _():
        sf_ref[...] = jnp.zeros_like(sf_ref)
        sb_ref[...] = jnp.zeros_like(sb_ref)

    _gla_direction(0, gqf, gkf, gvf, gaf, wd_ref, bd_ref, tri_ref, of_ref, sf_ref)
    _gla_direction(1, gqb, gkb, gvb, gab, wd_ref, bd_ref, tri_ref, ob_ref, sb_ref)


def _gla_scan(gq, gk, gv, ga, wd, bd, tri):
    t_tot = gq.shape[0]
    nt = t_tot // TM
    fwd = lambda t: t
    bwd = lambda t: jnp.where(t == 0, 0, nt - t)
    spec = lambda w, f: pl.BlockSpec((TM, w), lambda t: (f(t), 0))
    full = lambda a: pl.BlockSpec(a.shape, lambda t: (0,) * a.ndim)
    ins = []
    for f in (fwd, bwd):
        ins += [spec(GLA_K, f), spec(GLA_K, f), spec(GLA_V, f), spec(LANES, f)]
    return pl.pallas_call(
        _gla_kernel,
        grid=(nt,),
        in_specs=ins + [full(wd), full(bd), full(tri)],
        out_specs=[spec(GLA_V, fwd), spec(GLA_V, bwd)],
        out_shape=[jax.ShapeDtypeStruct((t_tot, GLA_V), F32)] * 2,
        scratch_shapes=[pltpu.VMEM((GLA_HEADS, GLA_DK, GLA_DV), F32)] * 2,
        compiler_params=_cparams(1),
        name="gla_scan",
    )(gq, gk, gv, ga, gq, gk, gv, ga, wd, bd, tri)


def _merge_kernel(x_ref, mod_ref, g1_ref, wg_ref, attn_ref, of_ref, ob_ref, gain_ref,
                  wa_ref, wl_ref, wo_ref, o_ref):
    mod = mod_ref[0]
    x = x_ref[...]
    h = _norm_modulate(x, g1_ref[...], mod[:, 0:D_MODEL], mod[:, D_MODEL:2 * D_MODEL]).astype(BF16)
    gain = gain_ref[...]
    parts = []
    for hh in range(GLA_HEADS):
        vc = slice(hh * GLA_DV, (hh + 1) * GLA_DV)
        o = of_ref[:, vc] + ob_ref[:, vc]
        ms = jnp.mean(o * o, axis=-1, keepdims=True)
        on = o * lax.rsqrt(ms + EPS) * gain
        gr = _dot(h, wg_ref[:, vc])
        parts.append((on * (gr * jax.nn.sigmoid(gr))).astype(BF16))
    gla = jnp.concatenate(parts, axis=1)
    gate_a = jax.nn.sigmoid(_dot(h, wg_ref[:, GLA_V:GLA_V + D_MODEL]))
    gate_b = jax.nn.sigmoid(_dot(h, wg_ref[:, GLA_V + D_MODEL:GLA_V + 2 * D_MODEL]))
    y = gate_a * _dot(attn_ref[...], wa_ref[...]) + gate_b * _dot(gla, wl_ref[...])
    gt1 = mod[:, 2 * D_MODEL:3 * D_MODEL]
    o_ref[...] = x + gt1 * _dot(y.astype(BF16), wo_ref[...])


def _merge_out(xc, mod2, g1, w_g, attn, o_f, o_b, gain, wa, wl, wo):
    t_tot = xc.shape[0]
    row = lambda w: pl.BlockSpec((TM, w), lambda i: (i, 0))
    full = lambda a: pl.BlockSpec(a.shape, lambda i: (0,) * a.ndim)
    return pl.pallas_call(
        _merge_kernel,
        grid=(t_tot // TM,),
        in_specs=[row(D_MODEL), _mod_row_spec(), full(g1), full(w_g), row(ATT_Q), row(GLA_V), row(GLA_V),
                  full(gain), full(wa), full(wl), full(wo)],
        out_specs=row(D_MODEL),
        out_shape=jax.ShapeDtypeStruct((t_tot, D_MODEL), F32),
        compiler_params=_cparams(1),
        name="merge_out",
    )(xc, mod2, g1, w_g, attn, o_f, o_b, gain, wa, wl, wo)


FF_CHUNK = 1024


def _mlp_kernel(x_ref, mod_ref, g2_ref, w1_ref, w2_ref, o_ref):
    mod = mod_ref[0]
    x = x_ref[...]
    h = _norm_modulate(x, g2_ref[...], mod[:, 3 * D_MODEL:4 * D_MODEL], mod[:, 4 * D_MODEL:5 * D_MODEL])
    h = h.astype(BF16)
    acc = jnp.zeros((TM, D_MODEL), F32)
    for j in range(D_FF // FF_CHUNK):
        cs = slice(j * FF_CHUNK, (j + 1) * FF_CHUNK)
        u = jnp.maximum(_dot(h, w1_ref[:, cs]), 0.0)
        acc = acc + _dot((u * u).astype(BF16), w2_ref[cs, :])
    o_ref[...] = x + mod[:, 5 * D_MODEL:6 * D_MODEL] * acc


def _mlp(xc, mod2, g2, w1, w2):
    t_tot = xc.shape[0]
    row = lambda w: pl.BlockSpec((TM, w), lambda i: (i, 0))
    full = lambda a: pl.BlockSpec(a.shape, lambda i: (0,) * a.ndim)
    return pl.pallas_call(
        _mlp_kernel,
        grid=(t_tot // TM,),
        in_specs=[row(D_MODEL), _mod_row_spec(), full(g2), full(w1), full(w2)],
        out_specs=row(D_MODEL),
        out_shape=jax.ShapeDtypeStruct((t_tot, D_MODEL), F32),
        compiler_params=_cparams(1),
        name="mlp",
    )(xc, mod2, g2, w1, w2)


def _q_column_perm():
    perm = np.zeros(ATT_Q, np.int32)
    for g in range(N_KV_HEADS):
        for hh in range(GROUP):
            dst = ((g // 2) * GROUP + hh) * LANES + (g % 2) * HEAD_DIM
            src = (g * GROUP + hh) * HEAD_DIM
            perm[dst:dst + HEAD_DIM] = np.arange(src, src + HEAD_DIM)
    return perm


def _rope_tables(n_lat, n_ctx):
    half = ROPE_AXIS_DIM // 2
    lane = np.arange(LANES)
    dd = lane % HEAD_DIM
    inv = ROPE_BASE ** (-(dd % half).astype(np.float64) * 2.0 / ROPE_AXIS_DIM)
    is_row = dd < ROPE_AXIS_DIM
    sign = np.where((dd % ROPE_AXIS_DIM) < half, -1.0, 1.0)
    rows = n_lat // GRID_W
    ang_r = np.arange(rows, dtype=np.float32).astype(np.float64)[:, None] * inv.astype(np.float32)[None, :]
    ang_c = np.arange(GRID_W, dtype=np.float32).astype(np.float64)[:, None] * inv.astype(np.float32)[None, :]
    m = jnp.asarray(is_row)[None, None, :]
    pick = lambda fr, fc: jnp.where(m, jnp.asarray(fr, F32)[:, None, :], jnp.asarray(fc, F32)[None, :, :])
    cos = pick(np.cos(ang_r), np.cos(ang_c)).reshape(n_lat, LANES)
    sin = pick(np.sin(ang_r) * sign, np.sin(ang_c) * sign).reshape(n_lat, LANES)
    cos = jnp.concatenate([jnp.ones((n_ctx, LANES), F32), cos], axis=0)
    sin = jnp.concatenate([jnp.zeros((n_ctx, LANES), F32), sin], axis=0)
    return cos, sin


def _chunk_cumsum_matrices():
    i = np.arange(TM)[:, None]
    j = np.arange(TM)[None, :]
    same = (i // GLA_CHUNK) == (j // GLA_CHUNK)
    return jnp.asarray(np.stack([same & (j <= i), same & (j >= i)]).astype(np.float32), BF16)


def kernel(x, c, ctx, c_ctx, w_mod, b_mod, g_norm1, w_in, q_gain, k_gain, sink, w_decay, b_decay,
           gla_gain, w_branch_attn, w_branch_gla, w_out, g_norm2, w_ff1, w_ff2):
    assert x.shape[0] == 1 and ctx.shape[0] == 1
    n_lat, n_ctx = x.shape[1], ctx.shape[1]
    assert n_ctx == TM and n_lat % TM == 0
    depth = w_mod.shape[0]

    cvec = jnp.zeros((8, D_MODEL), F32).at[0].set(c[0]).at[1].set(c_ctx)
    mod_all = _mod_vectors(cvec, w_mod, b_mod)
    cos, sin = _rope_tables(n_lat, n_ctx)
    tri = _chunk_cumsum_matrices()
    qperm = _q_column_perm()
    o_q, o_k, o_v, o_gq, o_gk, o_gv, o_gr, o_ga, o_a, o_b = np.cumsum(
        [0, ATT_Q, ATT_KV, ATT_KV, GLA_K, GLA_K, GLA_V, GLA_V, 2 * GLA_RANK, D_MODEL])

    xc = jnp.concatenate([ctx[0], x[0]], axis=0)
    for l in range(depth):
        w = w_in[l]
        w_a = jnp.concatenate(
            [w[:, o_q:o_k][:, qperm], w[:, o_k:o_gr],
             jnp.pad(w[:, o_ga:o_a], ((0, 0), (0, LANES - 2 * GLA_RANK)))], axis=1).astype(BF16)
        w_g = jnp.concatenate([w[:, o_gr:o_ga], w[:, o_a:]], axis=1).astype(BF16)
        wd = jnp.zeros((2, LANES, GLA_K), F32)
        wd = wd.at[0, 0:GLA_RANK].set(w_decay[l, 0]).at[1, GLA_RANK:2 * GLA_RANK].set(w_decay[l, 1]).astype(BF16)
        bd = b_decay[l].reshape(2, 1, GLA_K)
        mod2 = mod_all[l, 0:2].reshape(2, 1, N_MOD * D_MODEL)
        g1 = g_norm1[l].reshape(1, D_MODEL)
        qg = jnp.tile(q_gain[l], LANES // HEAD_DIM).reshape(1, LANES)
        kg = jnp.tile(k_gain[l], LANES // HEAD_DIM).reshape(1, LANES)
        sink_lanes = jnp.repeat(sink[l].reshape(N_KV_HEADS, 1, GROUP), ATT_BLOCK, axis=2)

        q, k, vt, gq, gk, gv, ga = _in_proj(xc, mod2, g1, w_a, qg, kg, cos, sin)
        attn = _window_attn(q, k, vt, sink_lanes, n_ctx)
        o_f, o_b_ = _gla_scan(gq, gk, gv, ga, wd, bd, tri)
        xc = _merge_out(xc, mod2, g1, w_g, attn, o_f, o_b_, gla_gain[l].reshape(1, GLA_DV),
                        w_branch_attn[l].astype(BF16), w_branch_gla[l].astype(BF16), w_out[l].astype(BF16))
        xc = _mlp(xc, mod2, g_norm2[l].reshape(1, D_MODEL), w_ff1[l].astype(BF16), w_ff2[l].astype(BF16))
    return xc[n_ctx:][None]
```

```python
import functools

import numpy as np
import jax
import jax.numpy as jnp
from jax import lax
from jax.experimental import pallas as pl
from jax.experimental.pallas import tpu as pltpu

F32 = jnp.float32
BF16 = jnp.bfloat16

D_MODEL = 1024
N_HEADS = 16
N_KV_HEADS = 4
HEAD_DIM = 64
GROUP = N_HEADS // N_KV_HEADS
ATT_BLOCK = 128
GRID_W = 64
ROPE_BASE = 10000.0
ROPE_AXIS_DIM = HEAD_DIM // 2
GLA_HEADS = 4
GLA_DK = 128
GLA_DV = 256
GLA_RANK = 16
GLA_NORMALIZER = 16.0
GLA_CHUNK = 64
D_FF = 4 * D_MODEL
N_MOD = 6
EPS = 1e-6
NEG = -1e30

ATT_Q = N_HEADS * HEAD_DIM
ATT_KV = N_KV_HEADS * HEAD_DIM
GLA_K = GLA_HEADS * GLA_DK
GLA_V = GLA_HEADS * GLA_DV

LANES = 128
TM = 256
CHUNKS_PER_TILE = TM // GLA_CHUNK
VMEM_LIMIT = 56 * 1024 * 1024


def _cparams(n_axes):
    return pltpu.CompilerParams(dimension_semantics=("arbitrary",) * n_axes,
                                vmem_limit_bytes=VMEM_LIMIT)


def _dot(a, b):
    return jnp.dot(a, b, preferred_element_type=F32)


def _dot_nt(a, b):
    return lax.dot_general(a, b, (((1,), (1,)), ((), ())), preferred_element_type=F32)


def _split3(x):
    x1 = x.astype(BF16)
    r1 = x - x1.astype(F32)
    x2 = r1.astype(BF16)
    x3 = (r1 - x2.astype(F32)).astype(BF16)
    return x1, x2, x3


MOD_TN = 1536


def _mod_kernel(c_ref, w_ref, b_ref, o_ref):
    c = c_ref[...]
    s = c * jax.nn.sigmoid(c)
    s1, s2, _ = _split3(s)
    w = w_ref[0]
    w1 = w.astype(BF16)
    w2 = (w - w1.astype(F32)).astype(BF16)
    o_ref[0] = _dot(s1, w1) + _dot(s2, w1) + _dot(s1, w2) + b_ref[0]


def _mod_vectors(cvec, w_mod, b_mod):
    depth = w_mod.shape[0]
    n_out = w_mod.shape[2]
    return pl.pallas_call(
        _mod_kernel,
        grid=(depth, n_out // MOD_TN),
        in_specs=[pl.BlockSpec((8, D_MODEL), lambda l, j: (0, 0)),
                  pl.BlockSpec((1, D_MODEL, MOD_TN), lambda l, j: (l, 0, j)),
                  pl.BlockSpec((1, 1, MOD_TN), lambda l, j: (l, 0, j))],
        out_specs=pl.BlockSpec((1, 8, MOD_TN), lambda l, j: (l, 0, j)),
        out_shape=jax.ShapeDtypeStruct((depth, 8, n_out), F32),
        compiler_params=_cparams(2),
        name="mod_vectors",
    )(cvec, w_mod, b_mod.reshape(depth, 1, n_out))


def _mod_row_spec():
    return pl.BlockSpec((1, 1, N_MOD * D_MODEL), lambda i: (jnp.where(i == 0, 1, 0), 0, 0))


def _norm_modulate(x, g, shift, scale):
    ms = jnp.mean(x * x, axis=-1, keepdims=True)
    y = x * lax.rsqrt(ms + EPS) * g
    return y * (1.0 + scale) + shift


C_Q = 0
C_K = C_Q + ATT_Q
C_V = C_K + ATT_KV
C_GQ = C_V + ATT_KV
C_GK = C_GQ + GLA_K
C_GV = C_GK + GLA_K
C_GA = C_GV + GLA_V
C_END = C_GA + LANES
LOG2E = 1.4426950408889634
Q_SCALE = HEAD_DIM ** -0.5 * LOG2E


def _headnorm_rope(z, gain, cos, sin, lane):
    lo = lane < HEAD_DIM
    sq = z * z
    s_lo = jnp.sum(jnp.where(lo, sq, 0.0), axis=-1, keepdims=True)
    s_hi = jnp.sum(jnp.where(lo, 0.0, sq), axis=-1, keepdims=True)
    ms = jnp.where(lo, s_lo, s_hi) * (1.0 / HEAD_DIM)
    y = z * lax.rsqrt(ms + EPS) * gain
    half = ROPE_AXIS_DIM // 2
    nxt = pltpu.roll(y, LANES - half, 1)
    prv = pltpu.roll(y, half, 1)
    partner = jnp.where((lane % ROPE_AXIS_DIM) < half, nxt, prv)
    return y * cos + partner * sin


def _in_proj_kernel(x_ref, mod_ref, g1_ref, w_ref, qg_ref, kg_ref, cos_ref, sin_ref,
                    q_ref, k_ref, vt_ref, gq_ref, gk_ref, gv_ref, gvt_ref, ga_ref):
    mod = mod_ref[0]
    h = _norm_modulate(x_ref[...], g1_ref[...], mod[:, 0:D_MODEL], mod[:, D_MODEL:2 * D_MODEL])
    h = h.astype(BF16)
    lane = lax.broadcasted_iota(jnp.int32, (TM, LANES), 1)
    cos = cos_ref[...]
    sin = sin_ref[...]
    qg = qg_ref[...]
    kg = kg_ref[...]
    for j in range(ATT_Q // 256):
        z = _dot(h, w_ref[:, C_Q + j * 256:C_Q + (j + 1) * 256])
        for e in range(2):
            r = _headnorm_rope(z[:, e * LANES:(e + 1) * LANES], qg, cos, sin, lane)
            c0 = j * 256 + e * LANES
            q_ref[:, c0:c0 + LANES] = (r * Q_SCALE).astype(BF16)
    z = _dot(h, w_ref[:, C_K:C_K + ATT_KV])
    for e in range(2):
        r = _headnorm_rope(z[:, e * LANES:(e + 1) * LANES], kg, cos, sin, lane)
        k_ref[:, e * LANES:(e + 1) * LANES] = r.astype(BF16)
    z = _dot(h, w_ref[:, C_V:C_V + ATT_KV])
    vt_ref[...] = z.T.astype(BF16)
    gq_ref[...] = _dot(h, w_ref[:, C_GQ:C_GQ + GLA_K])
    gk_ref[...] = _dot(h, w_ref[:, C_GK:C_GK + GLA_K])
    for j in range(GLA_V // 512):
        z = _dot(h, w_ref[:, C_GV + j * 512:C_GV + (j + 1) * 512])
        gv_ref[:, j * 512:(j + 1) * 512] = z.astype(BF16)
        gvt_ref[j * 512:(j + 1) * 512, :] = z.T.astype(BF16)
    ga_ref[...] = _dot(h, w_ref[:, C_GA:C_GA + LANES]).astype(BF16)


def _in_proj(xc, mod2, g1, w_a, qg, kg, cos, sin):
    t_tot = xc.shape[0]
    nt = t_tot // TM
    row = lambda w: pl.BlockSpec((TM, w), lambda i: (i, 0))
    full = lambda a: pl.BlockSpec(a.shape, lambda i: (0,) * a.ndim)
    return pl.pallas_call(
        _in_proj_kernel,
        grid=(nt,),
        in_specs=[row(D_MODEL), _mod_row_spec(), full(g1), full(w_a), full(qg), full(kg),
                  row(LANES), row(LANES)],
        out_specs=[row(ATT_Q), row(ATT_KV), pl.BlockSpec((ATT_KV, TM), lambda i: (0, i)),
                   row(GLA_K), row(GLA_K), row(GLA_V), pl.BlockSpec((GLA_V, TM), lambda i: (0, i)), row(LANES)],
        out_shape=[jax.ShapeDtypeStruct((t_tot, ATT_Q), BF16),
                   jax.ShapeDtypeStruct((t_tot, ATT_KV), BF16),
                   jax.ShapeDtypeStruct((ATT_KV, t_tot), BF16),
                   jax.ShapeDtypeStruct((t_tot, GLA_K), F32),
                   jax.ShapeDtypeStruct((t_tot, GLA_K), F32),
                   jax.ShapeDtypeStruct((t_tot, GLA_V), BF16),
                   jax.ShapeDtypeStruct((GLA_V, t_tot), BF16),
                   jax.ShapeDtypeStruct((t_tot, LANES), BF16)],
        compiler_params=_cparams(1),
        name="in_proj",
    )(xc, mod2, g1, w_a, qg, kg, cos, sin)


ONES_ROWS = 16


def _attn_groups(q_ref, k_refs, v_refs, sink_ref, o_ref, masks):
    lane_q = lax.broadcasted_iota(jnp.int32, (ATT_BLOCK, LANES), 1)
    n_keys = sum(r.shape[0] for r in k_refs)
    ones = jnp.ones((ONES_ROWS, n_keys), BF16)
    for j in range(N_KV_HEADS // 2):
        cs = slice(j * LANES, (j + 1) * LANES)
        k_all = jnp.concatenate([r[:, cs] for r in k_refs], axis=0)
        for e in range(2):
            g = 2 * j + e
            half = (lane_q < HEAD_DIM) if e == 0 else (lane_q >= HEAD_DIM)
            qs = []
            for hh in range(GROUP):
                qc = q_ref[:, (j * GROUP + hh) * LANES:(j * GROUP + hh + 1) * LANES]
                qs.append(jnp.where(half, qc, jnp.zeros_like(qc)))
            qg = jnp.concatenate(qs, axis=0)
            s = _dot_nt(k_all, qg)
            parts, r0 = [], 0
            for r, mk in zip(k_refs, masks):
                sp = s[r0:r0 + r.shape[0]]
                parts.append(sp if mk is None else jnp.where(mk, sp, NEG))
                r0 += r.shape[0]
            sk = sink_ref[g]
            m = sk
            for sp in parts:
                m = jnp.maximum(m, jnp.max(sp, axis=0, keepdims=True))
            p = jnp.concatenate([jnp.exp2(sp - m) for sp in parts], axis=0).astype(BF16)
            rs = slice(g * HEAD_DIM, (g + 1) * HEAD_DIM)
            vt_all = jnp.concatenate([jnp.concatenate([r[rs, :] for r in v_refs], axis=1), ones], axis=0)
            ot = _dot(vt_all, p)
            denom = ot[HEAD_DIM:HEAD_DIM + 1] + jnp.exp2(sk - m)
            ot = ot[0:HEAD_DIM] * (1.0 / denom)
            for pair in range(GROUP // 2):
                x2 = jnp.concatenate([ot[:, (2 * pair) * ATT_BLOCK:(2 * pair + 1) * ATT_BLOCK],
                                      ot[:, (2 * pair + 1) * ATT_BLOCK:(2 * pair + 2) * ATT_BLOCK]], axis=0)
                c0 = g * GROUP * HEAD_DIM + pair * LANES
                o_ref[:, c0:c0 + LANES] = x2.T.astype(BF16)


def _attn_kernel(q_ref, kp_ref, kc_ref, kn_ref, kx_ref, vp_ref, vc_ref, vn_ref, vx_ref, sink_ref,
                 o_ref, *, n_blocks, ctx_blocks):
    b = pl.program_id(0)

    @pl.when(b >= ctx_blocks)
    def _():
        gw = GROUP * ATT_BLOCK
        key_i = lax.broadcasted_iota(jnp.int32, (ATT_BLOCK, gw), 0)
        qry_i = lax.broadcasted_iota(jnp.int32, (ATT_BLOCK, gw), 1) % ATT_BLOCK
        m_prev = jnp.logical_and(key_i >= qry_i, b >= ctx_blocks + 1)
        m_next = jnp.logical_and(key_i <= qry_i, b <= n_blocks - 2)
        _attn_groups(q_ref, (kp_ref, kc_ref, kn_ref, kx_ref), (vp_ref, vc_ref, vn_ref, vx_ref),
                     sink_ref, o_ref, (m_prev, None, m_next, None))

    @pl.when(b < ctx_blocks)
    def _():
        _attn_groups(q_ref, (kx_ref,), (vx_ref,), sink_ref, o_ref, (None,))


def _window_attn(q, k, vt, sink_lanes, n_ctx):
    t_tot = q.shape[0]
    nb = t_tot // ATT_BLOCK
    cb = n_ctx // ATT_BLOCK
    prev_i = lambda b: jnp.clip(b - 1, cb, nb - 1)
    next_i = lambda b: jnp.clip(b + 1, cb, nb - 1)
    kspec = lambda f: pl.BlockSpec((ATT_BLOCK, ATT_KV), lambda b: (f(b), 0))
    vspec = lambda f: pl.BlockSpec((ATT_KV, ATT_BLOCK), lambda b: (0, f(b)))
    ident = lambda b: b
    return pl.pallas_call(
        functools.partial(_attn_kernel, n_blocks=nb, ctx_blocks=cb),
        grid=(nb,),
        in_specs=[pl.BlockSpec((ATT_BLOCK, ATT_Q), lambda b: (b, 0)),
                  kspec(prev_i), kspec(ident), kspec(next_i),
                  pl.BlockSpec((n_ctx, ATT_KV), lambda b: (0, 0)),
                  vspec(prev_i), vspec(ident), vspec(next_i),
                  pl.BlockSpec((ATT_KV, n_ctx), lambda b: (0, 0)),
                  pl.BlockSpec(sink_lanes.shape, lambda b: (0, 0, 0))],
        out_specs=pl.BlockSpec((ATT_BLOCK, ATT_Q), lambda b: (b, 0)),
        out_shape=jax.ShapeDtypeStruct((t_tot, ATT_Q), BF16),
        compiler_params=_cparams(1),
        name="window_attn",
    )(q, k, k, k, k, vt, vt, vt, vt, sink_lanes)


def _gla_direction(d, gq_ref, gk_ref, gv_ref, gvt_ref, ga_ref, wd_ref, bd_ref, tri_ref, o_ref, st_ref):
    x = _dot(ga_ref[...], wd_ref[d]) + bd_ref[d]
    g = (jnp.minimum(x, 0.0) - jnp.log1p(jnp.exp(-jnp.abs(x)))) * (1.0 / GLA_NORMALIZER)
    g1, g2, g3 = _split3(g)
    tri = tri_ref[d]
    bcum = _dot(tri, g1) + _dot(tri, g2) + _dot(tri, g3)
    zero_row = jnp.zeros((1, GLA_K), F32)

    def before_chunk(c):
        if d == 0:
            return zero_row if c == 0 else bcum[c * GLA_CHUNK - 1:c * GLA_CHUNK]
        last = CHUNKS_PER_TILE - 1
        return zero_row if c == last else bcum[(c + 1) * GLA_CHUNK:(c + 1) * GLA_CHUNK + 1]

    total = bcum[TM - 1:TM] if d == 0 else bcum[0:1]
    gq = gq_ref[...] * (GLA_DK ** -0.5)
    gk = gk_ref[...]
    q_loc = jnp.concatenate(
        [gq[c * GLA_CHUNK:(c + 1) * GLA_CHUNK] * jnp.exp(bcum[c * GLA_CHUNK:(c + 1) * GLA_CHUNK] - before_chunk(c))
         for c in range(CHUNKS_PER_TILE)], axis=0).astype(BF16)
    q_tile = (gq * jnp.exp(bcum)).astype(BF16)
    k_state = (gk * jnp.exp(total - bcum)).astype(BF16)
    k_dst = []
    for c in range(CHUNKS_PER_TILE):
        lo, hi = (0, (c + 1) * GLA_CHUNK) if d == 0 else (c * GLA_CHUNK, TM)
        scaled = gk[lo:hi] * jnp.exp(before_chunk(c) - bcum[lo:hi])
        pieces = [gk[0:lo]] * (lo > 0) + [scaled] + [gk[hi:TM]] * (hi < TM)
        k_dst.append(jnp.concatenate(pieces, axis=0).astype(BF16))
    ti = lax.broadcasted_iota(jnp.int32, (TM, TM), 0)
    tj = lax.broadcasted_iota(jnp.int32, (TM, TM), 1)
    causal = (tj <= ti) if d == 0 else (tj >= ti)
    decay = jnp.exp(total)
    for hh in range(GLA_HEADS):
        kc = slice(hh * GLA_DK, (hh + 1) * GLA_DK)
        vc = slice(hh * GLA_DV, (hh + 1) * GLA_DV)
        a = jnp.concatenate(
            [_dot_nt(q_loc[c * GLA_CHUNK:(c + 1) * GLA_CHUNK, kc], k_dst[c][:, kc])
             for c in range(CHUNKS_PER_TILE)], axis=0)
        a = jnp.where(causal, a, 0.0).astype(BF16)
        st_old = st_ref[hh]
        o_ref[:, vc] = _dot(a, gv_ref[:, vc]) + _dot_nt(q_tile[:, kc], st_old.astype(BF16))
        st_ref[hh] = st_old * decay[:, kc] + _dot(gvt_ref[vc, :], k_state[:, kc])


def _gla_kernel(gqf, gkf, gvf, gvtf, gaf, gqb, gkb, gvb, gvtb, gab, wd_ref, bd_ref, tri_ref,
                of_ref, ob_ref, sf_ref, sb_ref):
    @pl.when(pl.program_id(0) == 0)
    def _():
        sf_ref[...] = jnp.zeros_like(sf_ref)
        sb_ref[...] = jnp.zeros_like(sb_ref)

    _gla_direction(0, gqf, gkf, gvf, gvtf, gaf, wd_ref, bd_ref, tri_ref, of_ref, sf_ref)
    _gla_direction(1, gqb, gkb, gvb, gvtb, gab, wd_ref, bd_ref, tri_ref, ob_ref, sb_ref)


def _gla_scan(gq, gk, gv, gvt, ga, wd, bd, tri):
    t_tot = gq.shape[0]
    nt = t_tot // TM
    fwd = lambda t: t
    bwd = lambda t: jnp.where(t == 0, 0, nt - t)
    spec = lambda w, f: pl.BlockSpec((TM, w), lambda t: (f(t), 0))
    full = lambda a: pl.BlockSpec(a.shape, lambda t: (0,) * a.ndim)
    ins = []
    for f in (fwd, bwd):
        ins += [spec(GLA_K, f), spec(GLA_K, f), spec(GLA_V, f),
                pl.BlockSpec((GLA_V, TM), lambda t, f=f: (0, f(t))), spec(LANES, f)]
    return pl.pallas_call(
        _gla_kernel,
        grid=(nt,),
        in_specs=ins + [full(wd), full(bd), full(tri)],
        out_specs=[spec(GLA_V, fwd), spec(GLA_V, bwd)],
        out_shape=[jax.ShapeDtypeStruct((t_tot, GLA_V), F32)] * 2,
        scratch_shapes=[pltpu.VMEM((GLA_HEADS, GLA_DV, GLA_DK), F32)] * 2,
        compiler_params=_cparams(1),
        name="gla_scan",
    )(gq, gk, gv, gvt, ga, gq, gk, gv, gvt, ga, wd, bd, tri)


def _merge_kernel(x_ref, mod_ref, g1_ref, wg_ref, attn_ref, of_ref, ob_ref, gain_ref,
                  wa_ref, wl_ref, wo_ref, o_ref):
    mod = mod_ref[0]
    x = x_ref[...]
    h = _norm_modulate(x, g1_ref[...], mod[:, 0:D_MODEL], mod[:, D_MODEL:2 * D_MODEL]).astype(BF16)
    gain = gain_ref[...]
    parts = []
    for hh in range(GLA_HEADS):
        vc = slice(hh * GLA_DV, (hh + 1) * GLA_DV)
        o = of_ref[:, vc] + ob_ref[:, vc]
        ms = jnp.mean(o * o, axis=-1, keepdims=True)
        on = o * lax.rsqrt(ms + EPS) * gain
        gr = _dot(h, wg_ref[:, vc])
        parts.append((on * (gr * jax.nn.sigmoid(gr))).astype(BF16))
    gla = jnp.concatenate(parts, axis=1)
    gate_a = jax.nn.sigmoid(_dot(h, wg_ref[:, GLA_V:GLA_V + D_MODEL]))
    gate_b = jax.nn.sigmoid(_dot(h, wg_ref[:, GLA_V + D_MODEL:GLA_V + 2 * D_MODEL]))
    y = gate_a * _dot(attn_ref[...], wa_ref[...]) + gate_b * _dot(gla, wl_ref[...])
    gt1 = mod[:, 2 * D_MODEL:3 * D_MODEL]
    o_ref[...] = x + gt1 * _dot(y.astype(BF16), wo_ref[...])


def _merge_out(xc, mod2, g1, w_g, attn, o_f, o_b, gain, wa, wl, wo):
    t_tot = xc.shape[0]
    row = lambda w: pl.BlockSpec((TM, w), lambda i: (i, 0))
    full = lambda a: pl.BlockSpec(a.shape, lambda i: (0,) * a.ndim)
    return pl.pallas_call(
        _merge_kernel,
        grid=(t_tot // TM,),
        in_specs=[row(D_MODEL), _mod_row_spec(), full(g1), full(w_g), row(ATT_Q), row(GLA_V), row(GLA_V),
                  full(gain), full(wa), full(wl), full(wo)],
        out_specs=row(D_MODEL),
        out_shape=jax.ShapeDtypeStruct((t_tot, D_MODEL), F32),
        compiler_params=_cparams(1),
        name="merge_out",
    )(xc, mod2, g1, w_g, attn, o_f, o_b, gain, wa, wl, wo)


FF_CHUNK = 1024


def _mlp_kernel(x_ref, mod_ref, g2_ref, w1_ref, w2_ref, o_ref):
    mod = mod_ref[0]
    x = x_ref[...]
    h = _norm_modulate(x, g2_ref[...], mod[:, 3 * D_MODEL:4 * D_MODEL], mod[:, 4 * D_MODEL:5 * D_MODEL])
    h = h.astype(BF16)
    acc = jnp.zeros((TM, D_MODEL), F32)
    for j in range(D_FF // FF_CHUNK):
        cs = slice(j * FF_CHUNK, (j + 1) * FF_CHUNK)
        u = jnp.maximum(_dot(h, w1_ref[:, cs]), 0.0)
        acc = acc + _dot((u * u).astype(BF16), w2_ref[cs, :])
    o_ref[...] = x + mod[:, 5 * D_MODEL:6 * D_MODEL] * acc


def _mlp(xc, mod2, g2, w1, w2):
    t_tot = xc.shape[0]
    row = lambda w: pl.BlockSpec((TM, w), lambda i: (i, 0))
    full = lambda a: pl.BlockSpec(a.shape, lambda i: (0,) * a.ndim)
    return pl.pallas_call(
        _mlp_kernel,
        grid=(t_tot // TM,),
        in_specs=[row(D_MODEL), _mod_row_spec(), full(g2), full(w1), full(w2)],
        out_specs=row(D_MODEL),
        out_shape=jax.ShapeDtypeStruct((t_tot, D_MODEL), F32),
        compiler_params=_cparams(1),
        name="mlp",
    )(xc, mod2, g2, w1, w2)


def _q_column_perm():
    perm = np.zeros(ATT_Q, np.int32)
    for g in range(N_KV_HEADS):
        for hh in range(GROUP):
            dst = ((g // 2) * GROUP + hh) * LANES + (g % 2) * HEAD_DIM
            src = (g * GROUP + hh) * HEAD_DIM
            perm[dst:dst + HEAD_DIM] = np.arange(src, src + HEAD_DIM)
    return perm


def _rope_tables(n_lat, n_ctx):
    half = ROPE_AXIS_DIM // 2
    lane = np.arange(LANES)
    dd = lane % HEAD_DIM
    inv = ROPE_BASE ** (-(dd % half).astype(np.float64) * 2.0 / ROPE_AXIS_DIM)
    is_row = dd < ROPE_AXIS_DIM
    sign = np.where((dd % ROPE_AXIS_DIM) < half, -1.0, 1.0)
    rows = n_lat // GRID_W
    ang_r = np.arange(rows, dtype=np.float32).astype(np.float64)[:, None] * inv.astype(np.float32)[None, :]
    ang_c = np.arange(GRID_W, dtype=np.float32).astype(np.float64)[:, None] * inv.astype(np.float32)[None, :]
    m = jnp.asarray(is_row)[None, None, :]
    pick = lambda fr, fc: jnp.where(m, jnp.asarray(fr, F32)[:, None, :], jnp.asarray(fc, F32)[None, :, :])
    cos = pick(np.cos(ang_r), np.cos(ang_c)).reshape(n_lat, LANES)
    sin = pick(np.sin(ang_r) * sign, np.sin(ang_c) * sign).reshape(n_lat, LANES)
    cos = jnp.concatenate([jnp.ones((n_ctx, LANES), F32), cos], axis=0)
    sin = jnp.concatenate([jnp.zeros((n_ctx, LANES), F32), sin], axis=0)
    return cos, sin


def _tile_cumsum_matrices():
    i = np.arange(TM)[:, None]
    j = np.arange(TM)[None, :]
    return jnp.asarray(np.stack([j <= i, j >= i]).astype(np.float32), BF16)


def kernel(x, c, ctx, c_ctx, w_mod, b_mod, g_norm1, w_in, q_gain, k_gain, sink, w_decay, b_decay,
           gla_gain, w_branch_attn, w_branch_gla, w_out, g_norm2, w_ff1, w_ff2):
    assert x.shape[0] == 1 and ctx.shape[0] == 1
    n_lat, n_ctx = x.shape[1], ctx.shape[1]
    assert n_ctx == TM and n_lat % TM == 0
    depth = w_mod.shape[0]

    cvec = jnp.zeros((8, D_MODEL), F32).at[0].set(c[0]).at[1].set(c_ctx)
    mod_all = _mod_vectors(cvec, w_mod, b_mod)
    cos, sin = _rope_tables(n_lat, n_ctx)
    tri = _tile_cumsum_matrices()
    qperm = _q_column_perm()
    o_q, o_k, o_v, o_gq, o_gk, o_gv, o_gr, o_ga, o_a, o_b = np.cumsum(
        [0, ATT_Q, ATT_KV, ATT_KV, GLA_K, GLA_K, GLA_V, GLA_V, 2 * GLA_RANK, D_MODEL])

    xc = jnp.concatenate([ctx[0], x[0]], axis=0)
    for l in range(depth):
        w = w_in[l]
        w_a = jnp.concatenate(
            [w[:, o_q:o_k][:, qperm], w[:, o_k:o_gr],
             jnp.pad(w[:, o_ga:o_a], ((0, 0), (0, LANES - 2 * GLA_RANK)))], axis=1).astype(BF16)
        w_g = jnp.concatenate([w[:, o_gr:o_ga], w[:, o_a:]], axis=1).astype(BF16)
        wd = jnp.zeros((2, LANES, GLA_K), F32)
        wd = wd.at[0, 0:GLA_RANK].set(w_decay[l, 0]).at[1, GLA_RANK:2 * GLA_RANK].set(w_decay[l, 1]).astype(BF16)
        bd = b_decay[l].reshape(2, 1, GLA_K)
        mod2 = mod_all[l, 0:2].reshape(2, 1, N_MOD * D_MODEL)
        g1 = g_norm1[l].reshape(1, D_MODEL)
        qg = jnp.tile(q_gain[l], LANES // HEAD_DIM).reshape(1, LANES)
        kg = jnp.tile(k_gain[l], LANES // HEAD_DIM).reshape(1, LANES)
        sink_lanes = jnp.repeat((sink[l] * LOG2E).reshape(N_KV_HEADS, 1, GROUP), ATT_BLOCK, axis=2)

        q, k, vt, gq, gk, gv, gvt, ga = _in_proj(xc, mod2, g1, w_a, qg, kg, cos, sin)
        attn = _window_attn(q, k, vt, sink_lanes, n_ctx)
        o_f, o_b_ = _gla_scan(gq, gk, gv, gvt, ga, wd, bd, tri)
        xc = _merge_out(xc, mod2, g1, w_g, attn, o_f, o_b_, gla_gain[l].reshape(1, GLA_DV),
                        w_branch_attn[l].astype(BF16), w_branch_gla[l].astype(BF16), w_out[l].astype(BF16))
        xc = _mlp(xc, mod2, g_norm2[l].reshape(1, D_MODEL), w_ff1[l].astype(BF16), w_ff2[l].astype(BF16))
    return xc[n_ctx:][None]
```

```python
import functools

import numpy as np
import jax
import jax.numpy as jnp
from jax import lax
from jax.experimental import pallas as pl
from jax.experimental.pallas import tpu as pltpu

F32 = jnp.float32
BF16 = jnp.bfloat16

D_MODEL = 1024
N_HEADS = 16
N_KV_HEADS = 4
HEAD_DIM = 64
GROUP = N_HEADS // N_KV_HEADS
ATT_BLOCK = 128
GRID_W = 64
ROPE_BASE = 10000.0
ROPE_AXIS_DIM = HEAD_DIM // 2
GLA_HEADS = 4
GLA_DK = 128
GLA_DV = 256
GLA_RANK = 16
GLA_NORMALIZER = 16.0
GLA_CHUNK = 64
D_FF = 4 * D_MODEL
N_MOD = 6
EPS = 1e-6
NEG = -1e30
LOG2E = 1.4426950408889634

ATT_Q = N_HEADS * HEAD_DIM
ATT_KV = N_KV_HEADS * HEAD_DIM
GLA_K = GLA_HEADS * GLA_DK
GLA_V = GLA_HEADS * GLA_DV

LANES = 128
TM = 256
CHUNKS_PER_TILE = TM // GLA_CHUNK
MOD_ROWS = 8
VMEM_LIMIT = 56 * 1024 * 1024


def _cparams(n_axes):
    return pltpu.CompilerParams(dimension_semantics=("arbitrary",) * n_axes,
                                vmem_limit_bytes=VMEM_LIMIT)


def _dot(a, b):
    return jnp.dot(a, b, preferred_element_type=F32)


def _dot_nt(a, b):
    return lax.dot_general(a, b, (((1,), (1,)), ((), ())), preferred_element_type=F32)


def _dot_tn(a, b):
    return lax.dot_general(a, b, (((0,), (0,)), ((), ())), preferred_element_type=F32)


def _split2(x):
    x1 = x.astype(BF16)
    return x1, (x - x1.astype(F32)).astype(BF16)


def _row_spec(width):
    return pl.BlockSpec((TM, width), lambda i: (i, 0))


def _layer_spec(arr, layer):
    rest = arr.shape[1:]
    return pl.BlockSpec((None,) + rest, lambda *_: (layer,) + (0,) * len(rest))


def _mod_spec(layer):
    return pl.BlockSpec((1, 1, N_MOD * D_MODEL),
                        lambda i: (layer * MOD_ROWS + jnp.where(i == 0, 1, 0), 0, 0))


MOD_TN = 1536


def _mod_kernel(c_ref, w_ref, b_ref, o_ref):
    c = c_ref[...]
    s1, s2 = _split2(c * jax.nn.sigmoid(c))
    w1, w2 = _split2(w_ref[0])
    o_ref[0] = _dot(s1, w1) + _dot(s2, w1) + _dot(s1, w2) + b_ref[0]


def _mod_vectors(cvec, w_mod, b_mod):
    depth = w_mod.shape[0]
    n_out = w_mod.shape[2]
    return pl.pallas_call(
        _mod_kernel,
        grid=(depth, n_out // MOD_TN),
        in_specs=[pl.BlockSpec((MOD_ROWS, D_MODEL), lambda l, j: (0, 0)),
                  pl.BlockSpec((1, D_MODEL, MOD_TN), lambda l, j: (l, 0, j)),
                  pl.BlockSpec((1, 1, MOD_TN), lambda l, j: (l, 0, j))],
        out_specs=pl.BlockSpec((1, MOD_ROWS, MOD_TN), lambda l, j: (l, 0, j)),
        out_shape=jax.ShapeDtypeStruct((depth, MOD_ROWS, n_out), F32),
        compiler_params=_cparams(2),
        name="mod_vectors",
    )(cvec, w_mod, b_mod.reshape(depth, 1, n_out))


def _norm_modulate(x, g, shift, scale):
    ms = jnp.mean(x * x, axis=-1, keepdims=True)
    y = x * lax.rsqrt(ms + EPS) * g
    return y * (1.0 + scale) + shift


C_Q = 0
C_K = C_Q + ATT_Q
C_V = C_K + ATT_KV
C_GQ = C_V + ATT_KV
C_GK = C_GQ + GLA_K
C_GV = C_GK + GLA_K
C_GA = C_GV + GLA_V
C_END = C_GA + LANES
Q_SCALE = HEAD_DIM ** -0.5 * LOG2E


def _headnorm_rope(z, gain, cos, sin, lane):
    lo = lane < HEAD_DIM
    sq = z * z
    s_lo = jnp.sum(jnp.where(lo, sq, 0.0), axis=-1, keepdims=True)
    s_hi = jnp.sum(jnp.where(lo, 0.0, sq), axis=-1, keepdims=True)
    ms = jnp.where(lo, s_lo, s_hi) * (1.0 / HEAD_DIM)
    y = z * lax.rsqrt(ms + EPS) * gain
    half = ROPE_AXIS_DIM // 2
    nxt = pltpu.roll(y, LANES - half, 1)
    prv = pltpu.roll(y, half, 1)
    partner = jnp.where((lane % ROPE_AXIS_DIM) < half, nxt, prv)
    return y * cos + partner * sin


def _in_proj_kernel(x_ref, mod_ref, g1_ref, w_ref, qg_ref, kg_ref, cos_ref, sin_ref,
                    q_ref, k_ref, vt_ref, gq_ref, gk_ref, gv_ref, ga_ref):
    mod = mod_ref[0]
    h = _norm_modulate(x_ref[...], g1_ref[...], mod[:, 0:D_MODEL], mod[:, D_MODEL:2 * D_MODEL])
    h = h.astype(BF16)
    lane = lax.broadcasted_iota(jnp.int32, (TM, LANES), 1)
    cos = cos_ref[...]
    sin = sin_ref[...]
    qg = qg_ref[...]
    kg = kg_ref[...]
    for j in range(ATT_Q // 256):
        z = _dot(h, w_ref[:, C_Q + j * 256:C_Q + (j + 1) * 256])
        for e in range(2):
            r = _headnorm_rope(z[:, e * LANES:(e + 1) * LANES], qg, cos, sin, lane)
            c0 = j * 256 + e * LANES
            q_ref[:, c0:c0 + LANES] = (r * Q_SCALE).astype(BF16)
    z = _dot(h, w_ref[:, C_K:C_K + ATT_KV])
    for e in range(2):
        r = _headnorm_rope(z[:, e * LANES:(e + 1) * LANES], kg, cos, sin, lane)
        k_ref[:, e * LANES:(e + 1) * LANES] = r.astype(BF16)
    z = _dot(h, w_ref[:, C_V:C_V + ATT_KV])
    vt_ref[...] = z.T.astype(BF16)
    gq_ref[...] = _dot(h, w_ref[:, C_GQ:C_GQ + GLA_K])
    gk_ref[...] = _dot(h, w_ref[:, C_GK:C_GK + GLA_K])
    for j in range(GLA_V // 512):
        gv_ref[:, j * 512:(j + 1) * 512] = _dot(h, w_ref[:, C_GV + j * 512:C_GV + (j + 1) * 512]).astype(BF16)
    ga_ref[...] = _dot(h, w_ref[:, C_GA:C_GA + LANES]).astype(BF16)


def _in_proj(layer, xc, mod, g1, w_a, qg, kg, cos, sin):
    t_tot = xc.shape[0]
    return pl.pallas_call(
        _in_proj_kernel,
        grid=(t_tot // TM,),
        in_specs=[_row_spec(D_MODEL), _mod_spec(layer), _layer_spec(g1, layer), _layer_spec(w_a, layer),
                  _layer_spec(qg, layer), _layer_spec(kg, layer), _row_spec(LANES), _row_spec(LANES)],
        out_specs=[_row_spec(ATT_Q), _row_spec(ATT_KV), pl.BlockSpec((ATT_KV, TM), lambda i: (0, i)),
                   _row_spec(GLA_K), _row_spec(GLA_K), _row_spec(GLA_V), _row_spec(LANES)],
        out_shape=[jax.ShapeDtypeStruct((t_tot, ATT_Q), BF16),
                   jax.ShapeDtypeStruct((t_tot, ATT_KV), BF16),
                   jax.ShapeDtypeStruct((ATT_KV, t_tot), BF16),
                   jax.ShapeDtypeStruct((t_tot, GLA_K), F32),
                   jax.ShapeDtypeStruct((t_tot, GLA_K), F32),
                   jax.ShapeDtypeStruct((t_tot, GLA_V), BF16),
                   jax.ShapeDtypeStruct((t_tot, LANES), BF16)],
        compiler_params=_cparams(1),
        name="in_proj",
    )(xc, mod, g1, w_a, qg, kg, cos, sin)


ONES_ROWS = 16


def _attn_groups(q_ref, k_refs, v_refs, sink_ref, o_ref, masks):
    lane_q = lax.broadcasted_iota(jnp.int32, (ATT_BLOCK, LANES), 1)
    n_keys = sum(r.shape[0] for r in k_refs)
    ones = jnp.ones((ONES_ROWS, n_keys), BF16)

    def scores(g):
        j, e = divmod(g, 2)
        cs = slice(j * LANES, (j + 1) * LANES)
        half = (lane_q < HEAD_DIM) if e == 0 else (lane_q >= HEAD_DIM)
        qs = []
        for hh in range(GROUP):
            qc = q_ref[:, (j * GROUP + hh) * LANES:(j * GROUP + hh + 1) * LANES]
            qs.append(jnp.where(half, qc, jnp.zeros_like(qc)))
        qg = jnp.concatenate(qs, axis=0)
        s = _dot_nt(jnp.concatenate([r[:, cs] for r in k_refs], axis=0), qg)
        parts, r0 = [], 0
        for r, mk in zip(k_refs, masks):
            for _ in range(r.shape[0] // ATT_BLOCK):
                sp = s[r0:r0 + ATT_BLOCK]
                parts.append(sp if mk is None else jnp.where(mk, sp, NEG))
                r0 += ATT_BLOCK
        return parts

    def finish(g, parts):
        sk = sink_ref[g]
        tile_max = parts[0]
        for sp in parts[1:]:
            tile_max = jnp.maximum(tile_max, sp)
        m = jnp.maximum(sk, jnp.max(tile_max, axis=0, keepdims=True))
        p = jnp.concatenate([jnp.exp2(sp - m) for sp in parts], axis=0).astype(BF16)
        rs = slice(g * HEAD_DIM, (g + 1) * HEAD_DIM)
        vt_all = jnp.concatenate([jnp.concatenate([r[rs, :] for r in v_refs], axis=1), ones], axis=0)
        ot = _dot(vt_all, p)
        denom = ot[HEAD_DIM:HEAD_DIM + 1] + jnp.exp2(sk - m)
        ot = ot[0:HEAD_DIM] * (1.0 / denom)
        for pair in range(GROUP // 2):
            x2 = jnp.concatenate([ot[:, (2 * pair) * ATT_BLOCK:(2 * pair + 1) * ATT_BLOCK],
                                  ot[:, (2 * pair + 1) * ATT_BLOCK:(2 * pair + 2) * ATT_BLOCK]], axis=0)
            c0 = g * GROUP * HEAD_DIM + pair * LANES
            o_ref[:, c0:c0 + LANES] = x2.T.astype(BF16)

    pending = scores(0)
    for g in range(N_KV_HEADS):
        upcoming = scores(g + 1) if g + 1 < N_KV_HEADS else None
        finish(g, pending)
        pending = upcoming


def _attn_kernel(q_ref, kp_ref, kc_ref, kn_ref, kx_ref, vp_ref, vc_ref, vn_ref, vx_ref, sink_ref,
                 o_ref, *, n_blocks, ctx_blocks):
    b = pl.program_id(0)

    @pl.when(b >= ctx_blocks)
    def _():
        gw = GROUP * ATT_BLOCK
        key_i = lax.broadcasted_iota(jnp.int32, (ATT_BLOCK, gw), 0)
        qry_i = lax.broadcasted_iota(jnp.int32, (ATT_BLOCK, gw), 1) % ATT_BLOCK
        m_prev = jnp.logical_and(key_i >= qry_i, b >= ctx_blocks + 1)
        m_next = jnp.logical_and(key_i <= qry_i, b <= n_blocks - 2)
        _attn_groups(q_ref, (kp_ref, kc_ref, kn_ref, kx_ref), (vp_ref, vc_ref, vn_ref, vx_ref),
                     sink_ref, o_ref, (m_prev, None, m_next, None))

    @pl.when(b < ctx_blocks)
    def _():
        _attn_groups(q_ref, (kx_ref,), (vx_ref,), sink_ref, o_ref, (None,))


def _window_attn(layer, q, k, vt, sink_lanes, n_ctx):
    t_tot = q.shape[0]
    nb = t_tot // ATT_BLOCK
    cb = n_ctx // ATT_BLOCK
    prev_i = lambda b: jnp.clip(b - 1, cb, nb - 1)
    next_i = lambda b: jnp.clip(b + 1, cb, nb - 1)
    kspec = lambda f: pl.BlockSpec((ATT_BLOCK, ATT_KV), lambda b: (f(b), 0))
    vspec = lambda f: pl.BlockSpec((ATT_KV, ATT_BLOCK), lambda b: (0, f(b)))
    ident = lambda b: b
    return pl.pallas_call(
        functools.partial(_attn_kernel, n_blocks=nb, ctx_blocks=cb),
        grid=(nb,),
        in_specs=[pl.BlockSpec((ATT_BLOCK, ATT_Q), lambda b: (b, 0)),
                  kspec(prev_i), kspec(ident), kspec(next_i),
                  pl.BlockSpec((n_ctx, ATT_KV), lambda b: (0, 0)),
                  vspec(prev_i), vspec(ident), vspec(next_i),
                  pl.BlockSpec((ATT_KV, n_ctx), lambda b: (0, 0)),
                  _layer_spec(sink_lanes, layer)],
        out_specs=pl.BlockSpec((ATT_BLOCK, ATT_Q), lambda b: (b, 0)),
        out_shape=jax.ShapeDtypeStruct((t_tot, ATT_Q), BF16),
        compiler_params=_cparams(1),
        name="window_attn",
    )(q, k, k, k, k, vt, vt, vt, vt, sink_lanes)


def _gla_direction(d, gq_ref, gk_ref, gv_ref, ga_ref, wd_ref, bd_ref, tri_ref, o_ref, s_ref):
    nc = CHUNKS_PER_TILE
    rows = lambda c: slice(c * GLA_CHUNK, (c + 1) * GLA_CHUNK)
    x = _dot(ga_ref[...], wd_ref[d]) + bd_ref[d]
    u = jnp.exp2(jnp.abs(x) * (-LOG2E))
    g1, g2 = _split2(jnp.minimum(x, 0.0) * LOG2E - jnp.log2(1.0 + u))
    tri = tri_ref[d]
    bcum = _dot(tri, g1) + _dot(tri, g2)
    zero_row = jnp.zeros((1, GLA_K), F32)
    if d == 0:
        earlier = lambda c: range(0, c)
        b_start = lambda c: zero_row if c == 0 else bcum[c * GLA_CHUNK - 1:c * GLA_CHUNK]
        b_end = lambda c: bcum[(c + 1) * GLA_CHUNK - 1:(c + 1) * GLA_CHUNK]
        last_chunk = nc - 1
    else:
        earlier = lambda c: range(c + 1, nc)
        b_start = lambda c: zero_row if c == nc - 1 else bcum[(c + 1) * GLA_CHUNK:(c + 1) * GLA_CHUNK + 1]
        b_end = lambda c: bcum[c * GLA_CHUNK:c * GLA_CHUNK + 1]
        last_chunk = 0
    total = b_end(last_chunk)
    gq = gq_ref[...] * (GLA_DK ** -0.5)
    gk = gk_ref[...]
    q_loc, q_tile, k_in, k_out, k_raw = [], [], [], [], []
    for c in range(nc):
        b_loc = bcum[rows(c)] - b_start(c)
        ql = gq[rows(c)] * jnp.exp2(b_loc)
        q_loc.append(ql.astype(BF16))
        q_tile.append((ql * jnp.exp2(b_start(c))).astype(BF16))
        k_in.append((gk[rows(c)] * jnp.exp2(-b_loc)).astype(BF16))
        k_out.append(gk[rows(c)] * jnp.exp2(b_end(c) - bcum[rows(c)]))
        k_raw.append(gk[rows(c)].astype(BF16))
    k_out_bf = [k.astype(BF16) for k in k_out]

    def key_seen_from(c, src):
        if src == c:
            return k_in[c]
        if src not in earlier(c):
            return k_raw[src]
        if abs(src - c) == 1:
            return k_out_bf[src]
        return (k_out[src] * jnp.exp2(b_start(c) - b_end(src))).astype(BF16)

    k_dst = [jnp.concatenate([key_seen_from(c, src) for src in range(nc)], axis=0) for c in range(nc)]
    k_state = jnp.concatenate(
        [k_out_bf[c] if c == last_chunk else (k_out[c] * jnp.exp2(total - b_end(c))).astype(BF16)
         for c in range(nc)], axis=0)
    q_tile = jnp.concatenate(q_tile, axis=0)
    ti = lax.broadcasted_iota(jnp.int32, (TM, TM), 0)
    tj = lax.broadcasted_iota(jnp.int32, (TM, TM), 1)
    causal = (tj <= ti) if d == 0 else (tj >= ti)
    decay = jnp.exp2(total)

    for hh in range(GLA_HEADS):
        kc = slice(hh * GLA_DK, (hh + 1) * GLA_DK)
        vc = slice(hh * GLA_DV, (hh + 1) * GLA_DV)
        a = jnp.concatenate([_dot_nt(q_loc[c][:, kc], k_dst[c][:, kc]) for c in range(nc)], axis=0)
        a = jnp.where(causal, a, 0.0).astype(BF16)
        v = gv_ref[:, vc]
        s_old = s_ref[hh]
        o_ref[:, vc] = _dot(a, v) + _dot(q_tile[:, kc], s_old.astype(BF16))
        decay_col = jnp.broadcast_to(decay[:, kc], (GLA_DK, GLA_DK)).T
        decay_col = jnp.concatenate([decay_col] * (GLA_DV // GLA_DK), axis=1)
        s_ref[hh] = s_old * decay_col + _dot_tn(k_state[:, kc], v)


def _gla_kernel(gqf, gkf, gvf, gaf, gqb, gkb, gvb, gab, wd_ref, bd_ref, tri_ref,
                of_ref, ob_ref, sf_ref, sb_ref):
    @pl.when(pl.program_id(0) == 0)
    def _():
        sf_ref[...] = jnp.zeros_like(sf_ref)
        sb_ref[...] = jnp.zeros_like(sb_ref)

    _gla_direction(0, gqf, gkf, gvf, gaf, wd_ref, bd_ref, tri_ref, of_ref, sf_ref)
    _gla_direction(1, gqb, gkb, gvb, gab, wd_ref, bd_ref, tri_ref, ob_ref, sb_ref)


def _gla_scan(layer, gq, gk, gv, ga, wd, bd, tri):
    t_tot = gq.shape[0]
    nt = t_tot // TM
    fwd = lambda t: t
    bwd = lambda t: jnp.where(t == 0, 0, nt - t)
    spec = lambda w, f: pl.BlockSpec((TM, w), lambda t: (f(t), 0))
    ins = []
    for f in (fwd, bwd):
        ins += [spec(GLA_K, f), spec(GLA_K, f), spec(GLA_V, f), spec(LANES, f)]
    return pl.pallas_call(
        _gla_kernel,
        grid=(nt,),
        in_specs=ins + [_layer_spec(wd, layer), _layer_spec(bd, layer),
                        pl.BlockSpec(tri.shape, lambda t: (0, 0, 0))],
        out_specs=[spec(GLA_V, fwd), spec(GLA_V, bwd)],
        out_shape=[jax.ShapeDtypeStruct((t_tot, GLA_V), F32)] * 2,
        scratch_shapes=[pltpu.VMEM((GLA_HEADS, GLA_DK, GLA_DV), F32)] * 2,
        compiler_params=_cparams(1),
        name="gla_scan",
    )(gq, gk, gv, ga, gq, gk, gv, ga, wd, bd, tri)


def _merge_kernel(x_ref, mod_ref, g1_ref, wg_ref, attn_ref, of_ref, ob_ref, gain_ref,
                  wa_ref, wl_ref, wo_ref, o_ref):
    mod = mod_ref[0]
    x = x_ref[...]
    h = _norm_modulate(x, g1_ref[...], mod[:, 0:D_MODEL], mod[:, D_MODEL:2 * D_MODEL]).astype(BF16)
    gain = gain_ref[...]
    parts = []
    for hh in range(GLA_HEADS):
        vc = slice(hh * GLA_DV, (hh + 1) * GLA_DV)
        o = of_ref[:, vc] + ob_ref[:, vc]
        ms = jnp.mean(o * o, axis=-1, keepdims=True)
        on = o * lax.rsqrt(ms + EPS) * gain
        gr = _dot(h, wg_ref[:, vc])
        parts.append((on * (gr * jax.nn.sigmoid(gr))).astype(BF16))
    gla = jnp.concatenate(parts, axis=1)
    gate_a = jax.nn.sigmoid(_dot(h, wg_ref[:, GLA_V:GLA_V + D_MODEL]))
    gate_b = jax.nn.sigmoid(_dot(h, wg_ref[:, GLA_V + D_MODEL:GLA_V + 2 * D_MODEL]))
    y = gate_a * _dot(attn_ref[...], wa_ref[...]) + gate_b * _dot(gla, wl_ref[...])
    gt1 = mod[:, 2 * D_MODEL:3 * D_MODEL]
    o_ref[...] = x + gt1 * _dot(y.astype(BF16), wo_ref[...])


def _merge_out(layer, xc, mod, g1, w_g, attn, o_f, o_b, gain, wa, wl, wo):
    t_tot = xc.shape[0]
    return pl.pallas_call(
        _merge_kernel,
        grid=(t_tot // TM,),
        in_specs=[_row_spec(D_MODEL), _mod_spec(layer), _layer_spec(g1, layer), _layer_spec(w_g, layer),
                  _row_spec(ATT_Q), _row_spec(GLA_V), _row_spec(GLA_V), _layer_spec(gain, layer),
                  _layer_spec(wa, layer), _layer_spec(wl, layer), _layer_spec(wo, layer)],
        out_specs=_row_spec(D_MODEL),
        out_shape=jax.ShapeDtypeStruct((t_tot, D_MODEL), F32),
        compiler_params=_cparams(1),
        name="merge_out",
    )(xc, mod, g1, w_g, attn, o_f, o_b, gain, wa, wl, wo)


FF_CHUNK = 1024


def _mlp_kernel(x_ref, mod_ref, g2_ref, w1_ref, w2_ref, o_ref):
    mod = mod_ref[0]
    x = x_ref[...]
    h = _norm_modulate(x, g2_ref[...], mod[:, 3 * D_MODEL:4 * D_MODEL], mod[:, 4 * D_MODEL:5 * D_MODEL])
    h = h.astype(BF16)
    acc = jnp.zeros((TM, D_MODEL), F32)
    for j in range(D_FF // FF_CHUNK):
        cs = slice(j * FF_CHUNK, (j + 1) * FF_CHUNK)
        u = jnp.maximum(_dot(h, w1_ref[:, cs]), 0.0)
        acc = acc + _dot((u * u).astype(BF16), w2_ref[cs, :])
    o_ref[...] = x + mod[:, 5 * D_MODEL:6 * D_MODEL] * acc


def _mlp(layer, xc, mod, g2, w1, w2, latent_only):
    t_tot = xc.shape[0]
    if latent_only:
        out_rows = t_tot - TM
        out_spec = pl.BlockSpec((TM, D_MODEL), lambda i: (jnp.maximum(i - 1, 0), 0))
    else:
        out_rows = t_tot
        out_spec = _row_spec(D_MODEL)
    return pl.pallas_call(
        _mlp_kernel,
        grid=(t_tot // TM,),
        in_specs=[_row_spec(D_MODEL), _mod_spec(layer), _layer_spec(g2, layer),
                  _layer_spec(w1, layer), _layer_spec(w2, layer)],
        out_specs=out_spec,
        out_shape=jax.ShapeDtypeStruct((out_rows, D_MODEL), F32),
        compiler_params=_cparams(1),
        name="mlp",
    )(xc, mod, g2, w1, w2)


def _rope_tables(n_lat, n_ctx):
    half = ROPE_AXIS_DIM // 2
    lane = np.arange(LANES)
    dd = lane % HEAD_DIM
    inv = (ROPE_BASE ** (-(dd % half).astype(np.float64) * 2.0 / ROPE_AXIS_DIM)).astype(np.float32)
    is_row = dd < ROPE_AXIS_DIM
    sign = np.where((dd % ROPE_AXIS_DIM) < half, -1.0, 1.0)
    rows = n_lat // GRID_W
    ang_r = np.arange(rows, dtype=np.float32).astype(np.float64)[:, None] * inv[None, :]
    ang_c = np.arange(GRID_W, dtype=np.float32).astype(np.float64)[:, None] * inv[None, :]
    m = jnp.asarray(is_row)[None, None, :]
    pick = lambda fr, fc: jnp.where(m, jnp.asarray(fr, F32)[:, None, :], jnp.asarray(fc, F32)[None, :, :])
    cos = pick(np.cos(ang_r), np.cos(ang_c)).reshape(n_lat, LANES)
    sin = pick(np.sin(ang_r) * sign, np.sin(ang_c) * sign).reshape(n_lat, LANES)
    cos = jnp.concatenate([jnp.ones((n_ctx, LANES), F32), cos], axis=0)
    sin = jnp.concatenate([jnp.zeros((n_ctx, LANES), F32), sin], axis=0)
    return cos, sin


def _tile_cumsum_matrices():
    i = np.arange(TM)[:, None]
    j = np.arange(TM)[None, :]
    return jnp.asarray(np.stack([j <= i, j >= i]).astype(np.float32) / GLA_NORMALIZER, BF16)


def kernel(x, c, ctx, c_ctx, w_mod, b_mod, g_norm1, w_in, q_gain, k_gain, sink, w_decay, b_decay,
           gla_gain, w_branch_attn, w_branch_gla, w_out, g_norm2, w_ff1, w_ff2):
    assert x.shape[0] == 1 and ctx.shape[0] == 1
    n_lat, n_ctx = x.shape[1], ctx.shape[1]
    assert n_ctx == TM and n_lat % TM == 0
    depth = w_mod.shape[0]

    cvec = jnp.zeros((MOD_ROWS, D_MODEL), F32).at[0].set(c[0]).at[1].set(c_ctx)
    mod = _mod_vectors(cvec, w_mod, b_mod).reshape(depth * MOD_ROWS, 1, N_MOD * D_MODEL)
    cos, sin = _rope_tables(n_lat, n_ctx)
    tri = _tile_cumsum_matrices()

    o_q, o_k, o_v, o_gq, o_gk, o_gv, o_gr, o_ga, o_ma, o_mb = np.cumsum(
        [0, ATT_Q, ATT_KV, ATT_KV, GLA_K, GLA_K, GLA_V, GLA_V, 2 * GLA_RANK, D_MODEL])
    w_q = w_in[:, :, o_q:o_k].reshape(depth, D_MODEL, N_KV_HEADS // 2, 2, GROUP, HEAD_DIM)
    w_q = w_q.transpose(0, 1, 2, 4, 3, 5).reshape(depth, D_MODEL, ATT_Q)
    w_ga = jnp.pad(w_in[:, :, o_ga:o_ma], ((0, 0), (0, 0), (0, LANES - 2 * GLA_RANK)))
    w_a = jnp.concatenate([w_q, w_in[:, :, o_k:o_gr], w_ga], axis=2).astype(BF16)
    w_g = jnp.concatenate([w_in[:, :, o_gr:o_ga], w_in[:, :, o_ma:]], axis=2).astype(BF16)
    wa = w_branch_attn.astype(BF16)
    wl = w_branch_gla.astype(BF16)
    wo = w_out.astype(BF16)
    w1 = w_ff1.astype(BF16)
    w2 = w_ff2.astype(BF16)
    wd = jnp.zeros((depth, 2, LANES, GLA_K), F32)
    wd = wd.at[:, 0, 0:GLA_RANK].set(w_decay[:, 0]).at[:, 1, GLA_RANK:2 * GLA_RANK].set(w_decay[:, 1]).astype(BF16)
    bd = b_decay.reshape(depth, 2, 1, GLA_K)
    g1 = g_norm1.reshape(depth, 1, D_MODEL)
    g2 = g_norm2.reshape(depth, 1, D_MODEL)
    qg = jnp.tile(q_gain, (1, LANES // HEAD_DIM)).reshape(depth, 1, LANES)
    kg = jnp.tile(k_gain, (1, LANES // HEAD_DIM)).reshape(depth, 1, LANES)
    gain = gla_gain.reshape(depth, 1, GLA_DV)
    sink_lanes = jnp.repeat((sink * LOG2E).reshape(depth, N_KV_HEADS, 1, GROUP), ATT_BLOCK, axis=3)

    xc = jnp.concatenate([ctx[0], x[0]], axis=0)
    for l in range(depth):
        q, k, vt, gq, gk, gv, ga = _in_proj(l, xc, mod, g1, w_a, qg, kg, cos, sin)
        attn = _window_attn(l, q, k, vt, sink_lanes, n_ctx)
        o_f, o_b = _gla_scan(l, gq, gk, gv, ga, wd, bd, tri)
        xc = _merge_out(l, xc, mod, g1, w_g, attn, o_f, o_b, gain, wa, wl, wo)
        xc = _mlp(l, xc, mod, g2, w1, w2, latent_only=(l == depth - 1))
    return xc[None]
```

```python
import functools

import numpy as np
import jax
import jax.numpy as jnp
from jax import lax
from jax.experimental import pallas as pl
from jax.experimental.pallas import tpu as pltpu

F32 = jnp.float32
BF16 = jnp.bfloat16

D_MODEL = 1024
N_HEADS = 16
N_KV_HEADS = 4
HEAD_DIM = 64
GROUP = N_HEADS // N_KV_HEADS
ATT_BLOCK = 128
GRID_W = 64
ROPE_BASE = 10000.0
ROPE_AXIS_DIM = HEAD_DIM // 2
GLA_HEADS = 4
GLA_DK = 128
GLA_DV = 256
GLA_RANK = 16
GLA_NORMALIZER = 16.0
GLA_CHUNK = 64
D_FF = 4 * D_MODEL
N_MOD = 6
EPS = 1e-6
NEG = -1e30
LOG2E = 1.4426950408889634

ATT_Q = N_HEADS * HEAD_DIM
ATT_KV = N_KV_HEADS * HEAD_DIM
GLA_K = GLA_HEADS * GLA_DK
GLA_V = GLA_HEADS * GLA_DV

LANES = 128
TM = 256
CHUNKS_PER_TILE = TM // GLA_CHUNK
MOD_ROWS = 8
VMEM_LIMIT = 56 * 1024 * 1024


def _cparams(n_axes):
    return pltpu.CompilerParams(dimension_semantics=("arbitrary",) * n_axes,
                                vmem_limit_bytes=VMEM_LIMIT)


def _dot(a, b):
    return jnp.dot(a, b, preferred_element_type=F32)


def _dot_nt(a, b):
    return lax.dot_general(a, b, (((1,), (1,)), ((), ())), preferred_element_type=F32)


def _dot_tn(a, b):
    return lax.dot_general(a, b, (((0,), (0,)), ((), ())), preferred_element_type=F32)


def _split2(x):
    x1 = x.astype(BF16)
    return x1, (x - x1.astype(F32)).astype(BF16)


def _row_spec(width):
    return pl.BlockSpec((TM, width), lambda i: (i, 0))


def _layer_spec(arr, layer):
    rest = arr.shape[1:]
    return pl.BlockSpec((None,) + rest, lambda *_: (layer,) + (0,) * len(rest))


def _mod_spec(layer):
    return pl.BlockSpec((1, 1, N_MOD * D_MODEL),
                        lambda i: (layer * MOD_ROWS + jnp.where(i == 0, 1, 0), 0, 0))


MOD_TN = 1536


def _mod_kernel(c_ref, w_ref, b_ref, o_ref):
    c = c_ref[...]
    s1, s2 = _split2(c * jax.nn.sigmoid(c))
    w1, w2 = _split2(w_ref[0])
    o_ref[0] = _dot(s1, w1) + _dot(s2, w1) + _dot(s1, w2) + b_ref[0]


def _mod_vectors(cvec, w_mod, b_mod):
    depth = w_mod.shape[0]
    n_out = w_mod.shape[2]
    return pl.pallas_call(
        _mod_kernel,
        grid=(depth, n_out // MOD_TN),
        in_specs=[pl.BlockSpec((MOD_ROWS, D_MODEL), lambda l, j: (0, 0)),
                  pl.BlockSpec((1, D_MODEL, MOD_TN), lambda l, j: (l, 0, j)),
                  pl.BlockSpec((1, 1, MOD_TN), lambda l, j: (l, 0, j))],
        out_specs=pl.BlockSpec((1, MOD_ROWS, MOD_TN), lambda l, j: (l, 0, j)),
        out_shape=jax.ShapeDtypeStruct((depth, MOD_ROWS, n_out), F32),
        compiler_params=_cparams(2),
        name="mod_vectors",
    )(cvec, w_mod, b_mod.reshape(depth, 1, n_out))


def _norm_modulate(x, g, shift, scale):
    ms = jnp.mean(x * x, axis=-1, keepdims=True)
    y = x * lax.rsqrt(ms + EPS) * g
    return y * (1.0 + scale) + shift


C_Q = 0
C_K = C_Q + ATT_Q
C_V = C_K + ATT_KV
C_GQ = C_V + ATT_KV
C_GK = C_GQ + GLA_K
C_GV = C_GK + GLA_K
C_GR = C_GV + GLA_V
C_GA = C_GR + GLA_V
C_END = C_GA + LANES
C_MERGE = C_GA + 2 * GLA_RANK
Q_SCALE = HEAD_DIM ** -0.5 * LOG2E


def _headnorm_rope(z, gain, cos, sin, lane):
    lo = lane < HEAD_DIM
    sq = z * z
    s_lo = jnp.sum(jnp.where(lo, sq, 0.0), axis=-1, keepdims=True)
    s_hi = jnp.sum(jnp.where(lo, 0.0, sq), axis=-1, keepdims=True)
    ms = jnp.where(lo, s_lo, s_hi) * (1.0 / HEAD_DIM)
    y = z * lax.rsqrt(ms + EPS) * gain
    half = ROPE_AXIS_DIM // 2
    nxt = pltpu.roll(y, LANES - half, 1)
    prv = pltpu.roll(y, half, 1)
    partner = jnp.where((lane % ROPE_AXIS_DIM) < half, nxt, prv)
    return y * cos + partner * sin


def _tile_input(x_refs):
    if len(x_refs) == 1:
        return x_refs[0][...]
    ctx_ref, lat_ref = x_refs
    return jnp.where(pl.program_id(0) == 0, ctx_ref[...], lat_ref[...])


def _tile_input_specs(xs):
    if len(xs) == 1:
        return [_row_spec(D_MODEL)]
    return [pl.BlockSpec((TM, D_MODEL), lambda i: (0, 0)),
            pl.BlockSpec((TM, D_MODEL), lambda i: (jnp.maximum(i - 1, 0), 0))]


def _in_proj_kernel(*refs, n_x):
    x_refs = refs[:n_x]
    (mod_ref, g1_ref, w_ref, qg_ref, kg_ref, cos_ref, sin_ref, wd_ref, bd_ref, tri_ref,
     q_ref, k_ref, vt_ref, gq_ref, gk_ref, gv_ref, bf_ref, bb_ref) = refs[n_x:]
    mod = mod_ref[0]
    h = _norm_modulate(_tile_input(x_refs), g1_ref[...], mod[:, 0:D_MODEL], mod[:, D_MODEL:2 * D_MODEL])
    h = h.astype(BF16)
    lane = lax.broadcasted_iota(jnp.int32, (TM, LANES), 1)
    lo = lane < HEAD_DIM
    cos = cos_ref[...]
    sin = sin_ref[...]
    qg = qg_ref[...]
    kg = kg_ref[...]
    ga = _dot(h, w_ref[:, C_GA:C_GA + LANES]).astype(BF16)
    decay_pre = [_dot(ga, wd_ref[d]) + bd_ref[d] for d in range(2)]
    for j in range(N_KV_HEADS // 2):
        w = 2 * GROUP * HEAD_DIM
        z = _dot(h, w_ref[:, C_Q + j * w:C_Q + (j + 1) * w])
        r = [_headnorm_rope(z[:, i * LANES:(i + 1) * LANES], qg, cos, sin, lane) * Q_SCALE
             for i in range(w // LANES)]
        for hh in range(GROUP):
            a = r[hh // 2]
            b = r[GROUP // 2 + hh // 2]
            if hh % 2 == 0:
                out = jnp.where(lo, a, pltpu.roll(b, HEAD_DIM, 1))
            else:
                out = jnp.where(lo, pltpu.roll(a, HEAD_DIM, 1), b)
            c0 = (j * GROUP + hh) * LANES
            q_ref[:, c0:c0 + LANES] = out.astype(BF16)
    for d, b_ref in enumerate((bf_ref, bb_ref)):
        x = decay_pre[d]
        u = jnp.exp2(jnp.abs(x) * (-LOG2E))
        g1, g2 = _split2(jnp.minimum(x, 0.0) * LOG2E - jnp.log2(1.0 + u))
        tri = tri_ref[d]
        b_ref[...] = _dot(tri, g1) + _dot(tri, g2)
    z = _dot(h, w_ref[:, C_K:C_K + ATT_KV])
    for e in range(2):
        r = _headnorm_rope(z[:, e * LANES:(e + 1) * LANES], kg, cos, sin, lane)
        k_ref[:, e * LANES:(e + 1) * LANES] = r.astype(BF16)
    z = _dot(h, w_ref[:, C_V:C_V + ATT_KV])
    vt_ref[...] = z.T.astype(BF16)
    gq_ref[...] = _dot(h, w_ref[:, C_GQ:C_GQ + GLA_K])
    gk_ref[...] = _dot(h, w_ref[:, C_GK:C_GK + GLA_K])
    for j in range(GLA_V // 512):
        gv_ref[:, j * 512:(j + 1) * 512] = _dot(h, w_ref[:, C_GV + j * 512:C_GV + (j + 1) * 512]).astype(BF16)


def _in_proj(layer, xs, mod, g1, w_a, qg, kg, cos, sin, wd, bd, tri):
    t_tot = sum(a.shape[0] for a in xs)
    return pl.pallas_call(
        functools.partial(_in_proj_kernel, n_x=len(xs)),
        grid=(t_tot // TM,),
        in_specs=_tile_input_specs(xs) + [
            _mod_spec(layer), _layer_spec(g1, layer), _layer_spec(w_a, layer),
            _layer_spec(qg, layer), _layer_spec(kg, layer), _row_spec(LANES), _row_spec(LANES),
            _layer_spec(wd, layer), _layer_spec(bd, layer), pl.BlockSpec(tri.shape, lambda i: (0, 0, 0))],
        out_specs=[_row_spec(ATT_Q), _row_spec(ATT_KV), pl.BlockSpec((ATT_KV, TM), lambda i: (0, i)),
                   _row_spec(GLA_K), _row_spec(GLA_K), _row_spec(GLA_V), _row_spec(GLA_K), _row_spec(GLA_K)],
        out_shape=[jax.ShapeDtypeStruct((t_tot, ATT_Q), BF16),
                   jax.ShapeDtypeStruct((t_tot, ATT_KV), BF16),
                   jax.ShapeDtypeStruct((ATT_KV, t_tot), BF16),
                   jax.ShapeDtypeStruct((t_tot, GLA_K), F32),
                   jax.ShapeDtypeStruct((t_tot, GLA_K), F32),
                   jax.ShapeDtypeStruct((t_tot, GLA_V), BF16),
                   jax.ShapeDtypeStruct((t_tot, GLA_K), F32),
                   jax.ShapeDtypeStruct((t_tot, GLA_K), F32)],
        compiler_params=_cparams(1),
        name="in_proj",
    )(*xs, mod, g1, w_a, qg, kg, cos, sin, wd, bd, tri)


ONES_ROWS = 16


ATT_STEP = 2 * ATT_BLOCK


def _attn_blocks(q_ref, sink_ref, o_ref, key_lists):
    lane_q = lax.broadcasted_iota(jnp.int32, (ATT_BLOCK, LANES), 1)

    def scores(blk, g):
        j, e = divmod(g, 2)
        cs = slice(j * LANES, (j + 1) * LANES)
        qrows = slice(blk * ATT_BLOCK, (blk + 1) * ATT_BLOCK)
        half = (lane_q < HEAD_DIM) if e == 0 else (lane_q >= HEAD_DIM)
        qs = []
        for hh in range(GROUP):
            qc = q_ref[qrows, (j * GROUP + hh) * LANES:(j * GROUP + hh + 1) * LANES]
            qs.append(jnp.where(half, qc, jnp.zeros_like(qc)))
        qg = jnp.concatenate(qs, axis=0)
        keys = key_lists[blk]
        k_all = jnp.concatenate([kr[r0:r0 + ATT_BLOCK, cs] for kr, _, r0, _ in keys], axis=0)
        s = _dot_nt(k_all, qg)
        parts = []
        for i, (_, _, _, mk) in enumerate(keys):
            sp = s[i * ATT_BLOCK:(i + 1) * ATT_BLOCK]
            parts.append(sp if mk is None else jnp.where(mk, sp, NEG))
        return parts

    def finish(blk, g, parts):
        keys = key_lists[blk]
        sk = sink_ref[g]
        tile_max = parts[0]
        for sp in parts[1:]:
            tile_max = jnp.maximum(tile_max, sp)
        m = jnp.maximum(sk, jnp.max(tile_max, axis=0, keepdims=True))
        p = jnp.concatenate([jnp.exp2(sp - m) for sp in parts], axis=0).astype(BF16)
        rs = slice(g * HEAD_DIM, (g + 1) * HEAD_DIM)
        vt_all = jnp.concatenate([vr[rs, r0:r0 + ATT_BLOCK] for _, vr, r0, _ in keys], axis=1)
        ones = jnp.ones((ONES_ROWS, len(keys) * ATT_BLOCK), BF16)
        ot = _dot(jnp.concatenate([vt_all, ones], axis=0), p)
        denom = ot[HEAD_DIM:HEAD_DIM + 1] + jnp.exp2(sk - m)
        ot = ot[0:HEAD_DIM] * (1.0 / denom)
        qrows = slice(blk * ATT_BLOCK, (blk + 1) * ATT_BLOCK)
        for pair in range(GROUP // 2):
            x2 = jnp.concatenate([ot[:, (2 * pair) * ATT_BLOCK:(2 * pair + 1) * ATT_BLOCK],
                                  ot[:, (2 * pair + 1) * ATT_BLOCK:(2 * pair + 2) * ATT_BLOCK]], axis=0)
            c0 = g * GROUP * HEAD_DIM + pair * LANES
            o_ref[qrows, c0:c0 + LANES] = x2.T.astype(BF16)

    units = [(blk, g) for blk in range(len(key_lists)) for g in range(N_KV_HEADS)]
    pending = scores(*units[0])
    for i, unit in enumerate(units):
        upcoming = scores(*units[i + 1]) if i + 1 < len(units) else None
        finish(*unit, pending)
        pending = upcoming


def _attn_kernel(q_ref, kp_ref, kc_ref, kn_ref, kx_ref, vp_ref, vc_ref, vn_ref, vx_ref, sink_ref,
                 o_ref, *, n_steps, ctx_steps):
    s = pl.program_id(0)
    n_ctx = kx_ref.shape[0]
    ctx_keys = [(kx_ref, vx_ref, r0, None) for r0 in range(0, n_ctx, ATT_BLOCK)]

    @pl.when(s >= ctx_steps)
    def _():
        gw = GROUP * ATT_BLOCK
        key_i = lax.broadcasted_iota(jnp.int32, (ATT_BLOCK, gw), 0)
        qry_i = lax.broadcasted_iota(jnp.int32, (ATT_BLOCK, gw), 1) % ATT_BLOCK
        before = key_i >= qry_i
        after = key_i <= qry_i
        m_first = jnp.logical_and(before, s >= ctx_steps + 1)
        m_last = jnp.logical_and(after, s <= n_steps - 2)
        lo = (kc_ref, vc_ref, 0)
        hi = (kc_ref, vc_ref, ATT_BLOCK)
        _attn_blocks(q_ref, sink_ref, o_ref, [
            [(kp_ref, vp_ref, 0, m_first), lo + (None,), hi + (after,)] + ctx_keys,
            [lo + (before,), hi + (None,), (kn_ref, vn_ref, 0, m_last)] + ctx_keys])

    @pl.when(s < ctx_steps)
    def _():
        _attn_blocks(q_ref, sink_ref, o_ref, [ctx_keys] * (ATT_STEP // ATT_BLOCK))


def _window_attn(layer, q, k, vt, sink_lanes, n_ctx):
    t_tot = q.shape[0]
    assert t_tot % ATT_STEP == 0 and n_ctx % ATT_STEP == 0
    ns = t_tot // ATT_STEP
    cs = n_ctx // ATT_STEP
    nb = t_tot // ATT_BLOCK
    cb = n_ctx // ATT_BLOCK
    prev_i = lambda s: jnp.clip(2 * s - 1, cb, nb - 1)
    next_i = lambda s: jnp.clip(2 * s + 2, cb, nb - 1)
    ident = lambda s: s
    kspec = lambda rows, f: pl.BlockSpec((rows, ATT_KV), lambda s: (f(s), 0))
    vspec = lambda rows, f: pl.BlockSpec((ATT_KV, rows), lambda s: (0, f(s)))
    return pl.pallas_call(
        functools.partial(_attn_kernel, n_steps=ns, ctx_steps=cs),
        grid=(ns,),
        in_specs=[pl.BlockSpec((ATT_STEP, ATT_Q), lambda s: (s, 0)),
                  kspec(ATT_BLOCK, prev_i), kspec(ATT_STEP, ident), kspec(ATT_BLOCK, next_i),
                  pl.BlockSpec((n_ctx, ATT_KV), lambda s: (0, 0)),
                  vspec(ATT_BLOCK, prev_i), vspec(ATT_STEP, ident), vspec(ATT_BLOCK, next_i),
                  pl.BlockSpec((ATT_KV, n_ctx), lambda s: (0, 0)),
                  _layer_spec(sink_lanes, layer)],
        out_specs=pl.BlockSpec((ATT_STEP, ATT_Q), lambda s: (s, 0)),
        out_shape=jax.ShapeDtypeStruct((t_tot, ATT_Q), BF16),
        compiler_params=_cparams(1),
        name="window_attn",
    )(q, k, k, k, k, vt, vt, vt, vt, sink_lanes)


def _gla_direction(d, gq_ref, gk_ref, gv_ref, b_ref, o_ref, s_ref):
    nc = CHUNKS_PER_TILE
    rows = lambda c: slice(c * GLA_CHUNK, (c + 1) * GLA_CHUNK)
    bcum = b_ref[...]
    zero_row = jnp.zeros((1, GLA_K), F32)
    if d == 0:
        earlier = lambda c: range(0, c)
        b_start = lambda c: zero_row if c == 0 else bcum[c * GLA_CHUNK - 1:c * GLA_CHUNK]
        b_end = lambda c: bcum[(c + 1) * GLA_CHUNK - 1:(c + 1) * GLA_CHUNK]
        last_chunk = nc - 1
    else:
        earlier = lambda c: range(c + 1, nc)
        b_start = lambda c: zero_row if c == nc - 1 else bcum[(c + 1) * GLA_CHUNK:(c + 1) * GLA_CHUNK + 1]
        b_end = lambda c: bcum[c * GLA_CHUNK:c * GLA_CHUNK + 1]
        last_chunk = 0
    total = b_end(last_chunk)
    gq = gq_ref[...] * (GLA_DK ** -0.5)
    gk = gk_ref[...]
    q_loc, q_tile, k_in, k_out, k_raw = [], [], [], [], []
    for c in range(nc):
        b_loc = bcum[rows(c)] - b_start(c)
        ql = gq[rows(c)] * jnp.exp2(b_loc)
        q_loc.append(ql.astype(BF16))
        q_tile.append((ql * jnp.exp2(b_start(c))).astype(BF16))
        k_in.append((gk[rows(c)] * jnp.exp2(-b_loc)).astype(BF16))
        k_out.append(gk[rows(c)] * jnp.exp2(b_end(c) - bcum[rows(c)]))
        k_raw.append(gk[rows(c)].astype(BF16))
    k_out_bf = [k.astype(BF16) for k in k_out]

    def key_seen_from(c, src):
        if src == c:
            return k_in[c]
        if src not in earlier(c):
            return k_raw[src]
        if abs(src - c) == 1:
            return k_out_bf[src]
        return (k_out[src] * jnp.exp2(b_start(c) - b_end(src))).astype(BF16)

    k_dst = [jnp.concatenate([key_seen_from(c, src) for src in range(nc)], axis=0) for c in range(nc)]
    k_state = jnp.concatenate(
        [k_out_bf[c] if c == last_chunk else (k_out[c] * jnp.exp2(total - b_end(c))).astype(BF16)
         for c in range(nc)], axis=0)
    q_tile = jnp.concatenate(q_tile, axis=0)
    ti = lax.broadcasted_iota(jnp.int32, (TM, TM), 0)
    tj = lax.broadcasted_iota(jnp.int32, (TM, TM), 1)
    causal = (tj <= ti) if d == 0 else (tj >= ti)
    decay = jnp.exp2(total)

    for hh in range(GLA_HEADS):
        kc = slice(hh * GLA_DK, (hh + 1) * GLA_DK)
        vc = slice(hh * GLA_DV, (hh + 1) * GLA_DV)
        a = jnp.concatenate([_dot_nt(q_loc[c][:, kc], k_dst[c][:, kc]) for c in range(nc)], axis=0)
        a = jnp.where(causal, a, 0.0).astype(BF16)
        v = gv_ref[:, vc]
        s_old = s_ref[hh]
        o_ref[:, vc] = _dot(a, v) + _dot(q_tile[:, kc], s_old.astype(BF16))
        decay_col = jnp.broadcast_to(decay[:, kc], (GLA_DK, GLA_DK)).T
        decay_col = jnp.concatenate([decay_col] * (GLA_DV // GLA_DK), axis=1)
        s_ref[hh] = s_old * decay_col + _dot_tn(k_state[:, kc], v)


def _gla_kernel(gqf, gkf, gvf, bf, gqb, gkb, gvb, bb, of_ref, ob_ref, sf_ref, sb_ref):
    @pl.when(pl.program_id(0) == 0)
    def _():
        sf_ref[...] = jnp.zeros_like(sf_ref)
        sb_ref[...] = jnp.zeros_like(sb_ref)

    _gla_direction(0, gqf, gkf, gvf, bf, of_ref, sf_ref)
    _gla_direction(1, gqb, gkb, gvb, bb, ob_ref, sb_ref)


def _gla_scan(gq, gk, gv, b_fwd, b_bwd):
    t_tot = gq.shape[0]
    nt = t_tot // TM
    fwd = lambda t: t
    bwd = lambda t: jnp.where(t == 0, 0, nt - t)
    spec = lambda w, f: pl.BlockSpec((TM, w), lambda t: (f(t), 0))
    ins = []
    for f in (fwd, bwd):
        ins += [spec(GLA_K, f), spec(GLA_K, f), spec(GLA_V, f), spec(GLA_K, f)]
    return pl.pallas_call(
        _gla_kernel,
        grid=(nt,),
        in_specs=ins,
        out_specs=[spec(GLA_V, fwd), spec(GLA_V, bwd)],
        out_shape=[jax.ShapeDtypeStruct((t_tot, GLA_V), F32)] * 2,
        scratch_shapes=[pltpu.VMEM((GLA_HEADS, GLA_DK, GLA_DV), F32)] * 2,
        compiler_params=_cparams(1),
        name="gla_scan",
    )(gq, gk, gv, b_fwd, gq, gk, gv, b_bwd)


def _merge_kernel(*refs, n_x):
    x_refs = refs[:n_x]
    mod_ref, g1_ref, wg_ref, attn_ref, of_ref, ob_ref, gain_ref, wa_ref, wl_ref, wo_ref, o_ref = refs[n_x:]
    mod = mod_ref[0]
    x = _tile_input(x_refs)
    h = _norm_modulate(x, g1_ref[...], mod[:, 0:D_MODEL], mod[:, D_MODEL:2 * D_MODEL]).astype(BF16)
    gain = gain_ref[...]
    parts = []
    for hh in range(GLA_HEADS):
        vc = slice(hh * GLA_DV, (hh + 1) * GLA_DV)
        o = of_ref[:, vc] + ob_ref[:, vc]
        ms = jnp.mean(o * o, axis=-1, keepdims=True)
        on = o * lax.rsqrt(ms + EPS) * gain
        gr = _dot(h, wg_ref[:, vc])
        parts.append((on * (gr * jax.nn.sigmoid(gr))).astype(BF16))
    gla = jnp.concatenate(parts, axis=1)
    gate_a = jax.nn.sigmoid(_dot(h, wg_ref[:, GLA_V:GLA_V + D_MODEL]))
    gate_b = jax.nn.sigmoid(_dot(h, wg_ref[:, GLA_V + D_MODEL:GLA_V + 2 * D_MODEL]))
    y = gate_a * _dot(attn_ref[...], wa_ref[...]) + gate_b * _dot(gla, wl_ref[...])
    gt1 = mod[:, 2 * D_MODEL:3 * D_MODEL]
    o_ref[...] = x + gt1 * _dot(y.astype(BF16), wo_ref[...])


def _merge_out(layer, xs, mod, g1, w_g, attn, o_f, o_b, gain, wa, wl, wo):
    t_tot = sum(a.shape[0] for a in xs)
    return pl.pallas_call(
        functools.partial(_merge_kernel, n_x=len(xs)),
        grid=(t_tot // TM,),
        in_specs=_tile_input_specs(xs) + [
            _mod_spec(layer), _layer_spec(g1, layer), _layer_spec(w_g, layer),
            _row_spec(ATT_Q), _row_spec(GLA_V), _row_spec(GLA_V), _layer_spec(gain, layer),
            _layer_spec(wa, layer), _layer_spec(wl, layer), _layer_spec(wo, layer)],
        out_specs=_row_spec(D_MODEL),
        out_shape=jax.ShapeDtypeStruct((t_tot, D_MODEL), F32),
        compiler_params=_cparams(1),
        name="merge_out",
    )(*xs, mod, g1, w_g, attn, o_f, o_b, gain, wa, wl, wo)


FF_CHUNK = 1024


def _mlp_kernel(x_ref, mod_ref, g2_ref, w1_ref, w2_ref, o_ref):
    mod = mod_ref[0]
    x = x_ref[...]
    h = _norm_modulate(x, g2_ref[...], mod[:, 3 * D_MODEL:4 * D_MODEL], mod[:, 4 * D_MODEL:5 * D_MODEL])
    h = h.astype(BF16)
    acc = jnp.zeros((TM, D_MODEL), F32)
    for j in range(D_FF // FF_CHUNK):
        cs = slice(j * FF_CHUNK, (j + 1) * FF_CHUNK)
        u = jnp.maximum(_dot(h, w1_ref[:, cs]), 0.0)
        acc = acc + _dot((u * u).astype(BF16), w2_ref[cs, :])
    o_ref[...] = x + mod[:, 5 * D_MODEL:6 * D_MODEL] * acc


def _mlp(layer, xc, mod, g2, w1, w2, latent_only):
    t_tot = xc.shape[0]
    if latent_only:
        out_rows = t_tot - TM
        out_spec = pl.BlockSpec((TM, D_MODEL), lambda i: (jnp.maximum(i - 1, 0), 0))
    else:
        out_rows = t_tot
        out_spec = _row_spec(D_MODEL)
    return pl.pallas_call(
        _mlp_kernel,
        grid=(t_tot // TM,),
        in_specs=[_row_spec(D_MODEL), _mod_spec(layer), _layer_spec(g2, layer),
                  _layer_spec(w1, layer), _layer_spec(w2, layer)],
        out_specs=out_spec,
        out_shape=jax.ShapeDtypeStruct((out_rows, D_MODEL), F32),
        compiler_params=_cparams(1),
        name="mlp",
    )(xc, mod, g2, w1, w2)


def _rope_tables(n_lat, n_ctx):
    half = ROPE_AXIS_DIM // 2
    lane = np.arange(LANES)
    dd = lane % HEAD_DIM
    inv = (ROPE_BASE ** (-(dd % half).astype(np.float64) * 2.0 / ROPE_AXIS_DIM)).astype(np.float32)
    is_row = dd < ROPE_AXIS_DIM
    sign = np.where((dd % ROPE_AXIS_DIM) < half, -1.0, 1.0)
    rows = n_lat // GRID_W
    ang_r = np.arange(rows, dtype=np.float32).astype(np.float64)[:, None] * inv[None, :]
    ang_c = np.arange(GRID_W, dtype=np.float32).astype(np.float64)[:, None] * inv[None, :]
    m = jnp.asarray(is_row)[None, None, :]
    pick = lambda fr, fc: jnp.where(m, jnp.asarray(fr, F32)[:, None, :], jnp.asarray(fc, F32)[None, :, :])
    cos = pick(np.cos(ang_r), np.cos(ang_c)).reshape(n_lat, LANES)
    sin = pick(np.sin(ang_r) * sign, np.sin(ang_c) * sign).reshape(n_lat, LANES)
    cos = jnp.concatenate([jnp.ones((n_ctx, LANES), F32), cos], axis=0)
    sin = jnp.concatenate([jnp.zeros((n_ctx, LANES), F32), sin], axis=0)
    return cos, sin


def _tile_cumsum_matrices():
    i = np.arange(TM)[:, None]
    j = np.arange(TM)[None, :]
    return jnp.asarray(np.stack([j <= i, j >= i]).astype(np.float32) / GLA_NORMALIZER, BF16)


def kernel(x, c, ctx, c_ctx, w_mod, b_mod, g_norm1, w_in, q_gain, k_gain, sink, w_decay, b_decay,
           gla_gain, w_branch_attn, w_branch_gla, w_out, g_norm2, w_ff1, w_ff2):
    assert x.shape[0] == 1 and ctx.shape[0] == 1
    n_lat, n_ctx = x.shape[1], ctx.shape[1]
    assert n_ctx == TM and n_lat % TM == 0
    depth = w_mod.shape[0]

    cvec = jnp.zeros((MOD_ROWS, D_MODEL), F32).at[0].set(c[0]).at[1].set(c_ctx)
    mod = _mod_vectors(cvec, w_mod, b_mod).reshape(depth * MOD_ROWS, 1, N_MOD * D_MODEL)
    cos, sin = _rope_tables(n_lat, n_ctx)
    tri = _tile_cumsum_matrices()

    w_a = w_in[:, :, 0:C_END].astype(BF16)
    w_g = jnp.concatenate([w_in[:, :, C_GR:C_GA], w_in[:, :, C_MERGE:]], axis=2).astype(BF16)
    wa = w_branch_attn.astype(BF16)
    wl = w_branch_gla.astype(BF16)
    wo = w_out.astype(BF16)
    w1 = w_ff1.astype(BF16)
    w2 = w_ff2.astype(BF16)
    wd = jnp.zeros((depth, 2, LANES, GLA_K), F32)
    wd = wd.at[:, 0, 0:GLA_RANK].set(w_decay[:, 0]).at[:, 1, GLA_RANK:2 * GLA_RANK].set(w_decay[:, 1]).astype(BF16)
    bd = b_decay.reshape(depth, 2, 1, GLA_K)
    g1 = g_norm1.reshape(depth, 1, D_MODEL)
    g2 = g_norm2.reshape(depth, 1, D_MODEL)
    qg = jnp.tile(q_gain, (1, LANES // HEAD_DIM)).reshape(depth, 1, LANES)
    kg = jnp.tile(k_gain, (1, LANES // HEAD_DIM)).reshape(depth, 1, LANES)
    gain = gla_gain.reshape(depth, 1, GLA_DV)
    sink_lanes = jnp.repeat((sink * LOG2E).reshape(depth, N_KV_HEADS, 1, GROUP), ATT_BLOCK, axis=3)

    xs = (ctx[0], x[0])
    for l in range(depth):
        q, k, vt, gq, gk, gv, b_fwd, b_bwd = _in_proj(l, xs, mod, g1, w_a, qg, kg, cos, sin, wd, bd, tri)
        attn = _window_attn(l, q, k, vt, sink_lanes, n_ctx)
        o_f, o_b = _gla_scan(gq, gk, gv, b_fwd, b_bwd)
        xc = _merge_out(l, xs, mod, g1, w_g, attn, o_f, o_b, gain, wa, wl, wo)
        xc = _mlp(l, xc, mod, g2, w1, w2, latent_only=(l == depth - 1))
        xs = (xc,)
    return xc[None]
```

```python
import functools

import numpy as np
import jax
import jax.numpy as jnp
from jax import lax
from jax.experimental import pallas as pl
from jax.experimental.pallas import tpu as pltpu

F32 = jnp.float32
BF16 = jnp.bfloat16

D_MODEL = 1024
N_HEADS = 16
N_KV_HEADS = 4
HEAD_DIM = 64
GROUP = N_HEADS // N_KV_HEADS
ATT_BLOCK = 128
GRID_W = 64
ROPE_BASE = 10000.0
ROPE_AXIS_DIM = HEAD_DIM // 2
GLA_HEADS = 4
GLA_DK = 128
GLA_DV = 256
GLA_RANK = 16
GLA_NORMALIZER = 16.0
GLA_CHUNK = 64
D_FF = 4 * D_MODEL
N_MOD = 6
EPS = 1e-6
NEG = -1e30
LOG2E = 1.4426950408889634

ATT_Q = N_HEADS * HEAD_DIM
ATT_KV = N_KV_HEADS * HEAD_DIM
GLA_K = GLA_HEADS * GLA_DK
GLA_V = GLA_HEADS * GLA_DV

LANES = 128
TM = 256
SUB_TILES = 5
CHUNKS_PER_TILE = TM // GLA_CHUNK
MOD_ROWS = 8
VMEM_LIMIT = 56 * 1024 * 1024


def _cparams(n_axes):
    return pltpu.CompilerParams(dimension_semantics=("arbitrary",) * n_axes,
                                vmem_limit_bytes=VMEM_LIMIT)


def _dot(a, b):
    return jnp.dot(a, b, preferred_element_type=F32)


def _dot_nt(a, b):
    return lax.dot_general(a, b, (((1,), (1,)), ((), ())), preferred_element_type=F32)


def _dot_tn(a, b):
    return lax.dot_general(a, b, (((0,), (0,)), ((), ())), preferred_element_type=F32)


def _split2(x):
    x1 = x.astype(BF16)
    return x1, (x - x1.astype(F32)).astype(BF16)


def _layer_spec(arr, layer):
    rest = arr.shape[1:]
    return pl.BlockSpec((None,) + rest, lambda *_: (layer,) + (0,) * len(rest))


def _mod_pair_spec(layer):
    return pl.BlockSpec((2, 1, N_MOD * D_MODEL), lambda i: (layer * (MOD_ROWS // 2), 0, 0))


def _tile_mod(mod_ref, t):
    if t > 0:
        return mod_ref[0]
    return jnp.where(pl.program_id(0) == 0, mod_ref[1], mod_ref[0])


MOD_TN = 1536


def _mod_kernel(c_ref, w_ref, b_ref, o_ref):
    c = c_ref[...]
    s1, s2 = _split2(c * jax.nn.sigmoid(c))
    w1, w2 = _split2(w_ref[0])
    o_ref[0] = _dot(s1, w1) + _dot(s2, w1) + _dot(s1, w2) + b_ref[0]


def _mod_vectors(cvec, w_mod, b_mod):
    depth = w_mod.shape[0]
    n_out = w_mod.shape[2]
    return pl.pallas_call(
        _mod_kernel,
        grid=(depth, n_out // MOD_TN),
        in_specs=[pl.BlockSpec((MOD_ROWS, D_MODEL), lambda l, j: (0, 0)),
                  pl.BlockSpec((1, D_MODEL, MOD_TN), lambda l, j: (l, 0, j)),
                  pl.BlockSpec((1, 1, MOD_TN), lambda l, j: (l, 0, j))],
        out_specs=pl.BlockSpec((1, MOD_ROWS, MOD_TN), lambda l, j: (l, 0, j)),
        out_shape=jax.ShapeDtypeStruct((depth, MOD_ROWS, n_out), F32),
        compiler_params=_cparams(2),
        name="mod_vectors",
    )(cvec, w_mod, b_mod.reshape(depth, 1, n_out))


def _norm_modulate(x, g, shift, scale):
    ms = jnp.mean(x * x, axis=-1, keepdims=True)
    y = x * lax.rsqrt(ms + EPS) * g
    return y * (1.0 + scale) + shift


C_Q = 0
C_K = C_Q + ATT_Q
C_V = C_K + ATT_KV
C_GQ = C_V + ATT_KV
C_GK = C_GQ + GLA_K
C_GV = C_GK + GLA_K
C_GR = C_GV + GLA_V
C_GA = C_GR + GLA_V
C_END = C_GA + LANES
C_MERGE = C_GA + 2 * GLA_RANK
Q_SCALE = HEAD_DIM ** -0.5 * LOG2E


def _headnorm_rope(z, gain, cos, sin, lane):
    lo = lane < HEAD_DIM
    sq = z * z
    s_lo = jnp.sum(jnp.where(lo, sq, 0.0), axis=-1, keepdims=True)
    s_hi = jnp.sum(jnp.where(lo, 0.0, sq), axis=-1, keepdims=True)
    ms = jnp.where(lo, s_lo, s_hi) * (1.0 / HEAD_DIM)
    y = z * lax.rsqrt(ms + EPS) * gain
    half = ROPE_AXIS_DIM // 2
    nxt = pltpu.roll(y, LANES - half, 1)
    prv = pltpu.roll(y, half, 1)
    partner = jnp.where((lane % ROPE_AXIS_DIM) < half, nxt, prv)
    return y * cos + partner * sin


def _tile_input(x_refs, t):
    if len(x_refs) == 1:
        return x_refs[0][t * TM:(t + 1) * TM, :]
    if t > 0:
        return x_refs[1 + t][...]
    return jnp.where(pl.program_id(0) == 0, x_refs[0][...], x_refs[1][...])


def _tile_input_specs(xs):
    if len(xs) == 1:
        return [_step_spec(D_MODEL)]
    lat = lambda t: pl.BlockSpec((TM, D_MODEL), lambda i: (jnp.maximum(SUB_TILES * i + t - 1, 0), 0))
    return [pl.BlockSpec((TM, D_MODEL), lambda i: (0, 0))] + [lat(t) for t in range(SUB_TILES)]


def _tile_input_args(xs):
    return xs if len(xs) == 1 else (xs[0],) + (xs[1],) * SUB_TILES


def _step_spec(width):
    return pl.BlockSpec((SUB_TILES * TM, width), lambda i: (i, 0))


def _in_proj_kernel(*refs, n_x):
    x_refs = refs[:n_x]
    (mod_ref, g1_ref, w_ref, qg_ref, kg_ref, cos_ref, sin_ref, wd_ref, bd_ref, tri_ref,
     q_ref, k_ref, vt_ref, gq_ref, gk_ref, gv_ref, bf_ref, bb_ref) = refs[n_x:]
    for t in range(SUB_TILES):
        rows = slice(t * TM, (t + 1) * TM)
        _in_proj_tile(_tile_input(x_refs, t), _tile_mod(mod_ref, t), g1_ref, w_ref, qg_ref, kg_ref,
                      cos_ref[rows, :], sin_ref[rows, :], wd_ref, bd_ref, tri_ref,
                      q_ref.at[rows, :], k_ref.at[rows, :], vt_ref.at[:, rows], gq_ref.at[rows, :],
                      gk_ref.at[rows, :], gv_ref.at[rows, :], bf_ref.at[rows, :], bb_ref.at[rows, :])


def _in_proj_tile(x, mod, g1_ref, w_ref, qg_ref, kg_ref, cos, sin, wd_ref, bd_ref, tri_ref,
                  q_ref, k_ref, vt_ref, gq_ref, gk_ref, gv_ref, bf_ref, bb_ref):
    h = _norm_modulate(x, g1_ref[...], mod[:, 0:D_MODEL], mod[:, D_MODEL:2 * D_MODEL])
    h = h.astype(BF16)
    lane = lax.broadcasted_iota(jnp.int32, (TM, LANES), 1)
    lo = lane < HEAD_DIM
    qg = qg_ref[...]
    kg = kg_ref[...]
    ga = _dot(h, w_ref[:, C_GA:C_GA + LANES]).astype(BF16)
    decay_pre = [_dot(ga, wd_ref[d]) + bd_ref[d] for d in range(2)]
    for j in range(N_KV_HEADS // 2):
        w = 2 * GROUP * HEAD_DIM
        z = _dot(h, w_ref[:, C_Q + j * w:C_Q + (j + 1) * w])
        r = [_headnorm_rope(z[:, i * LANES:(i + 1) * LANES], qg, cos, sin, lane) * Q_SCALE
             for i in range(w // LANES)]
        for hh in range(GROUP):
            a = r[hh // 2]
            b = r[GROUP // 2 + hh // 2]
            if hh % 2 == 0:
                out = jnp.where(lo, a, pltpu.roll(b, HEAD_DIM, 1))
            else:
                out = jnp.where(lo, pltpu.roll(a, HEAD_DIM, 1), b)
            c0 = (j * GROUP + hh) * LANES
            q_ref[:, c0:c0 + LANES] = out.astype(BF16)
    for d, b_ref in enumerate((bf_ref, bb_ref)):
        x = decay_pre[d]
        u = jnp.exp2(jnp.abs(x) * (-LOG2E))
        g1, g2 = _split2(jnp.minimum(x, 0.0) * LOG2E - jnp.log2(1.0 + u))
        tri = tri_ref[d]
        b_ref[...] = _dot(tri, g1) + _dot(tri, g2)
    z = _dot(h, w_ref[:, C_K:C_K + ATT_KV])
    for e in range(2):
        r = _headnorm_rope(z[:, e * LANES:(e + 1) * LANES], kg, cos, sin, lane)
        k_ref[:, e * LANES:(e + 1) * LANES] = r.astype(BF16)
    z = _dot(h, w_ref[:, C_V:C_V + ATT_KV])
    vt_ref[...] = z.T.astype(BF16)
    gq_ref[...] = _dot(h, w_ref[:, C_GQ:C_GQ + GLA_K]).astype(BF16)
    gk_ref[...] = _dot(h, w_ref[:, C_GK:C_GK + GLA_K]).astype(BF16)
    for j in range(GLA_V // 512):
        gv_ref[:, j * 512:(j + 1) * 512] = _dot(h, w_ref[:, C_GV + j * 512:C_GV + (j + 1) * 512]).astype(BF16)


def _in_proj(layer, xs, mod, g1, w_in_bf, qg, kg, cos, sin, wd, bd, tri):
    t_tot = sum(a.shape[0] for a in xs)
    depth, d_model, _ = w_in_bf.shape
    x_args = _tile_input_args(xs)
    return pl.pallas_call(
        functools.partial(_in_proj_kernel, n_x=len(x_args)),
        grid=(t_tot // (SUB_TILES * TM),),
        in_specs=_tile_input_specs(xs) + [
            _mod_pair_spec(layer), _layer_spec(g1, layer),
            pl.BlockSpec((None, d_model, C_END), lambda i: (layer, 0, 0)),
            _layer_spec(qg, layer), _layer_spec(kg, layer), _step_spec(LANES), _step_spec(LANES),
            _layer_spec(wd, layer), _layer_spec(bd, layer), pl.BlockSpec(tri.shape, lambda i: (0, 0, 0))],
        out_specs=[_step_spec(ATT_Q), _step_spec(ATT_KV), pl.BlockSpec((ATT_KV, SUB_TILES * TM), lambda i: (0, i)),
                   _step_spec(GLA_K), _step_spec(GLA_K), _step_spec(GLA_V), _step_spec(GLA_K), _step_spec(GLA_K)],
        out_shape=[jax.ShapeDtypeStruct((t_tot, ATT_Q), BF16),
                   jax.ShapeDtypeStruct((t_tot, ATT_KV), BF16),
                   jax.ShapeDtypeStruct((ATT_KV, t_tot), BF16),
                   jax.ShapeDtypeStruct((t_tot, GLA_K), BF16),
                   jax.ShapeDtypeStruct((t_tot, GLA_K), BF16),
                   jax.ShapeDtypeStruct((t_tot, GLA_V), BF16),
                   jax.ShapeDtypeStruct((t_tot, GLA_K), F32),
                   jax.ShapeDtypeStruct((t_tot, GLA_K), F32)],
        compiler_params=_cparams(1),
        name="in_proj",
    )(*x_args, mod, g1, w_in_bf, qg, kg, cos, sin, wd, bd, tri)


ONES_ROWS = 16


ATT_STEP = 2 * ATT_BLOCK


def _attn_blocks(q_ref, sink_ref, o_ref, key_lists):
    lane_q = lax.broadcasted_iota(jnp.int32, (ATT_BLOCK, LANES), 1)

    def scores(blk, g):
        j, e = divmod(g, 2)
        cs = slice(j * LANES, (j + 1) * LANES)
        qrows = slice(blk * ATT_BLOCK, (blk + 1) * ATT_BLOCK)
        half = (lane_q < HEAD_DIM) if e == 0 else (lane_q >= HEAD_DIM)
        qs = []
        for hh in range(GROUP):
            qc = q_ref[qrows, (j * GROUP + hh) * LANES:(j * GROUP + hh + 1) * LANES]
            qs.append(jnp.where(half, qc, jnp.zeros_like(qc)))
        qg = jnp.concatenate(qs, axis=0)
        keys = key_lists[blk]
        k_all = jnp.concatenate([kr[r0:r0 + ATT_BLOCK, cs] for kr, _, r0, _ in keys], axis=0)
        s = _dot_nt(k_all, qg)
        parts = []
        for i, (_, _, _, mk) in enumerate(keys):
            sp = s[i * ATT_BLOCK:(i + 1) * ATT_BLOCK]
            parts.append(sp if mk is None else jnp.where(mk, sp, NEG))
        return parts

    def finish(blk, g, parts):
        keys = key_lists[blk]
        sk = sink_ref[g]
        tile_max = parts[0]
        for sp in parts[1:]:
            tile_max = jnp.maximum(tile_max, sp)
        m = jnp.maximum(sk, jnp.max(tile_max, axis=0, keepdims=True))
        p = jnp.concatenate([jnp.exp2(sp - m) for sp in parts], axis=0).astype(BF16)
        rs = slice(g * HEAD_DIM, (g + 1) * HEAD_DIM)
        vt_all = jnp.concatenate([vr[rs, r0:r0 + ATT_BLOCK] for _, vr, r0, _ in keys], axis=1)
        ones = jnp.ones((ONES_ROWS, len(keys) * ATT_BLOCK), BF16)
        ot = _dot(jnp.concatenate([vt_all, ones], axis=0), p)
        denom = ot[HEAD_DIM:HEAD_DIM + 1] + jnp.exp2(sk - m)
        ot = ot[0:HEAD_DIM] * (1.0 / denom)
        qrows = slice(blk * ATT_BLOCK, (blk + 1) * ATT_BLOCK)
        for pair in range(GROUP // 2):
            x2 = jnp.concatenate([ot[:, (2 * pair) * ATT_BLOCK:(2 * pair + 1) * ATT_BLOCK],
                                  ot[:, (2 * pair + 1) * ATT_BLOCK:(2 * pair + 2) * ATT_BLOCK]], axis=0)
            c0 = g * GROUP * HEAD_DIM + pair * LANES
            o_ref[qrows, c0:c0 + LANES] = x2.T.astype(BF16)

    units = [(blk, g) for blk in range(len(key_lists)) for g in range(N_KV_HEADS)]
    pending = scores(*units[0])
    for i, unit in enumerate(units):
        upcoming = scores(*units[i + 1]) if i + 1 < len(units) else None
        finish(*unit, pending)
        pending = upcoming


def _attn_kernel(q_ref, kp_ref, kc_ref, kn_ref, kx_ref, vp_ref, vc_ref, vn_ref, vx_ref, sink_ref,
                 o_ref, *, n_steps, ctx_steps):
    s = pl.program_id(0)
    n_ctx = kx_ref.shape[0]
    ctx_keys = [(kx_ref, vx_ref, r0, None) for r0 in range(0, n_ctx, ATT_BLOCK)]

    @pl.when(s >= ctx_steps)
    def _():
        gw = GROUP * ATT_BLOCK
        key_i = lax.broadcasted_iota(jnp.int32, (ATT_BLOCK, gw), 0)
        qry_i = lax.broadcasted_iota(jnp.int32, (ATT_BLOCK, gw), 1) % ATT_BLOCK
        before = key_i >= qry_i
        after = key_i <= qry_i
        m_first = jnp.logical_and(before, s >= ctx_steps + 1)
        m_last = jnp.logical_and(after, s <= n_steps - 2)
        lo = (kc_ref, vc_ref, 0)
        hi = (kc_ref, vc_ref, ATT_BLOCK)
        _attn_blocks(q_ref, sink_ref, o_ref, [
            [(kp_ref, vp_ref, 0, m_first), lo + (None,), hi + (after,)] + ctx_keys,
            [lo + (before,), hi + (None,), (kn_ref, vn_ref, 0, m_last)] + ctx_keys])

    @pl.when(s < ctx_steps)
    def _():
        _attn_blocks(q_ref, sink_ref, o_ref, [ctx_keys] * (ATT_STEP // ATT_BLOCK))


def _window_attn(layer, q, k, vt, sink_lanes, n_ctx):
    t_tot = q.shape[0]
    assert t_tot % ATT_STEP == 0 and n_ctx % ATT_STEP == 0
    ns = t_tot // ATT_STEP
    cs = n_ctx // ATT_STEP
    nb = t_tot // ATT_BLOCK
    cb = n_ctx // ATT_BLOCK
    prev_i = lambda s: jnp.clip(2 * s - 1, cb, nb - 1)
    next_i = lambda s: jnp.clip(2 * s + 2, cb, nb - 1)
    ident = lambda s: s
    kspec = lambda rows, f: pl.BlockSpec((rows, ATT_KV), lambda s: (f(s), 0))
    vspec = lambda rows, f: pl.BlockSpec((ATT_KV, rows), lambda s: (0, f(s)))
    return pl.pallas_call(
        functools.partial(_attn_kernel, n_steps=ns, ctx_steps=cs),
        grid=(ns,),
        in_specs=[pl.BlockSpec((ATT_STEP, ATT_Q), lambda s: (s, 0)),
                  kspec(ATT_BLOCK, prev_i), kspec(ATT_STEP, ident), kspec(ATT_BLOCK, next_i),
                  pl.BlockSpec((n_ctx, ATT_KV), lambda s: (0, 0)),
                  vspec(ATT_BLOCK, prev_i), vspec(ATT_STEP, ident), vspec(ATT_BLOCK, next_i),
                  pl.BlockSpec((ATT_KV, n_ctx), lambda s: (0, 0)),
                  _layer_spec(sink_lanes, layer)],
        out_specs=pl.BlockSpec((ATT_STEP, ATT_Q), lambda s: (s, 0)),
        out_shape=jax.ShapeDtypeStruct((t_tot, ATT_Q), BF16),
        compiler_params=_cparams(1),
        name="window_attn",
    )(q, k, k, k, k, vt, vt, vt, vt, sink_lanes)


def _gla_direction(d, gq_ref, gk_ref, gv_ref, b_ref, o_ref, s_ref):
    nc = CHUNKS_PER_TILE
    rows = lambda c: slice(c * GLA_CHUNK, (c + 1) * GLA_CHUNK)
    bcum = b_ref[...]
    zero_row = jnp.zeros((1, GLA_K), F32)
    if d == 0:
        earlier = lambda c: range(0, c)
        b_start = lambda c: zero_row if c == 0 else bcum[c * GLA_CHUNK - 1:c * GLA_CHUNK]
        b_end = lambda c: bcum[(c + 1) * GLA_CHUNK - 1:(c + 1) * GLA_CHUNK]
        last_chunk = nc - 1
    else:
        earlier = lambda c: range(c + 1, nc)
        b_start = lambda c: zero_row if c == nc - 1 else bcum[(c + 1) * GLA_CHUNK:(c + 1) * GLA_CHUNK + 1]
        b_end = lambda c: bcum[c * GLA_CHUNK:c * GLA_CHUNK + 1]
        last_chunk = 0
    total = b_end(last_chunk)
    gq = gq_ref[...].astype(F32) * (GLA_DK ** -0.5)
    gk = gk_ref[...].astype(F32)
    q_loc, q_tile, k_in, k_out, k_raw = [], [], [], [], []
    for c in range(nc):
        b_loc = bcum[rows(c)] - b_start(c)
        ql = gq[rows(c)] * jnp.exp2(b_loc)
        q_loc.append(ql.astype(BF16))
        q_tile.append((ql * jnp.exp2(b_start(c))).astype(BF16))
        k_in.append((gk[rows(c)] * jnp.exp2(-b_loc)).astype(BF16))
        k_out.append(gk[rows(c)] * jnp.exp2(b_end(c) - bcum[rows(c)]))
        k_raw.append(gk_ref[rows(c), :])
    k_out_bf = [k.astype(BF16) for k in k_out]

    def key_seen_from(c, src):
        if src == c:
            return k_in[c]
        if src not in earlier(c):
            return k_raw[src]
        if abs(src - c) == 1:
            return k_out_bf[src]
        return (k_out[src] * jnp.exp2(b_start(c) - b_end(src))).astype(BF16)

    k_dst = [jnp.concatenate([key_seen_from(c, src) for src in range(nc)], axis=0) for c in range(nc)]
    k_state = jnp.concatenate(
        [k_out_bf[c] if c == last_chunk else (k_out[c] * jnp.exp2(total - b_end(c))).astype(BF16)
         for c in range(nc)], axis=0)
    q_tile = jnp.concatenate(q_tile, axis=0)
    ti = lax.broadcasted_iota(jnp.int32, (TM, TM), 0)
    tj = lax.broadcasted_iota(jnp.int32, (TM, TM), 1)
    causal = (tj <= ti) if d == 0 else (tj >= ti)
    decay = jnp.exp2(total)

    for hh in range(GLA_HEADS):
        kc = slice(hh * GLA_DK, (hh + 1) * GLA_DK)
        vc = slice(hh * GLA_DV, (hh + 1) * GLA_DV)
        a = jnp.concatenate([_dot_nt(q_loc[c][:, kc], k_dst[c][:, kc]) for c in range(nc)], axis=0)
        a = jnp.where(causal, a, 0.0).astype(BF16)
        v = gv_ref[:, vc]
        s_old = s_ref[hh]
        o_ref[:, vc] = (_dot(a, v) + _dot(q_tile[:, kc], s_old.astype(BF16))).astype(BF16)
        decay_col = jnp.broadcast_to(decay[:, kc], (GLA_DK, GLA_DK)).T
        decay_col = jnp.concatenate([decay_col] * (GLA_DV // GLA_DK), axis=1)
        s_ref[hh] = s_old * decay_col + _dot_tn(k_state[:, kc], v)


def _gla_kernel(gqf, gkf, gvf, bf, gqb, gkb, gvb, bb, of_ref, ob_ref, sf_ref, sb_ref):
    @pl.when(pl.program_id(0) == 0)
    def _():
        sf_ref[...] = jnp.zeros_like(sf_ref)
        sb_ref[...] = jnp.zeros_like(sb_ref)

    _gla_direction(0, gqf, gkf, gvf, bf, of_ref, sf_ref)
    _gla_direction(1, gqb, gkb, gvb, bb, ob_ref, sb_ref)


def _gla_scan(gq, gk, gv, b_fwd, b_bwd):
    t_tot = gq.shape[0]
    nt = t_tot // TM
    fwd = lambda t: t
    bwd = lambda t: jnp.where(t == 0, 0, nt - t)
    spec = lambda w, f: pl.BlockSpec((TM, w), lambda t: (f(t), 0))
    ins = []
    for f in (fwd, bwd):
        ins += [spec(GLA_K, f), spec(GLA_K, f), spec(GLA_V, f), spec(GLA_K, f)]
    return pl.pallas_call(
        _gla_kernel,
        grid=(nt,),
        in_specs=ins,
        out_specs=[spec(GLA_V, fwd), spec(GLA_V, bwd)],
        out_shape=[jax.ShapeDtypeStruct((t_tot, GLA_V), BF16)] * 2,
        scratch_shapes=[pltpu.VMEM((GLA_HEADS, GLA_DK, GLA_DV), F32)] * 2,
        compiler_params=_cparams(1),
        name="gla_scan",
    )(gq, gk, gv, b_fwd, gq, gk, gv, b_bwd)


def _merge_kernel(*refs, n_x):
    x_refs = refs[:n_x]
    mod_ref, g1_ref, wg_ref, attn_ref, of_ref, ob_ref, gain_ref, wa_ref, wl_ref, wo_ref, o_ref = refs[n_x:]
    gain = gain_ref[...]
    for t in range(SUB_TILES):
        rows = slice(t * TM, (t + 1) * TM)
        mod = _tile_mod(mod_ref, t)
        x = _tile_input(x_refs, t)
        h = _norm_modulate(x, g1_ref[...], mod[:, 0:D_MODEL], mod[:, D_MODEL:2 * D_MODEL]).astype(BF16)
        parts = []
        for hh in range(GLA_HEADS):
            vc = slice(hh * GLA_DV, (hh + 1) * GLA_DV)
            o = of_ref[rows, vc].astype(F32) + ob_ref[rows, vc].astype(F32)
            ms = jnp.mean(o * o, axis=-1, keepdims=True)
            on = o * lax.rsqrt(ms + EPS) * gain
            gr = _dot(h, wg_ref[:, vc])
            parts.append((on * (gr * jax.nn.sigmoid(gr))).astype(BF16))
        gla = jnp.concatenate(parts, axis=1)
        gate_a = jax.nn.sigmoid(_dot(h, wg_ref[:, GLA_V:GLA_V + D_MODEL]))
        gate_b = jax.nn.sigmoid(_dot(h, wg_ref[:, GLA_V + D_MODEL:GLA_V + 2 * D_MODEL]))
        y = gate_a * _dot(attn_ref[rows, :], wa_ref[...]) + gate_b * _dot(gla, wl_ref[...])
        gt1 = mod[:, 2 * D_MODEL:3 * D_MODEL]
        o_ref[rows, :] = x + gt1 * _dot(y.astype(BF16), wo_ref[...])


def _merge_out(layer, xs, mod, g1, w_g, attn, o_f, o_b, gain, wa, wl, wo):
    t_tot = sum(a.shape[0] for a in xs)
    x_args = _tile_input_args(xs)
    return pl.pallas_call(
        functools.partial(_merge_kernel, n_x=len(x_args)),
        grid=(t_tot // (SUB_TILES * TM),),
        in_specs=_tile_input_specs(xs) + [
            _mod_pair_spec(layer), _layer_spec(g1, layer), _layer_spec(w_g, layer),
            _step_spec(ATT_Q), _step_spec(GLA_V), _step_spec(GLA_V), _layer_spec(gain, layer),
            _layer_spec(wa, layer), _layer_spec(wl, layer), _layer_spec(wo, layer)],
        out_specs=_step_spec(D_MODEL),
        out_shape=jax.ShapeDtypeStruct((t_tot, D_MODEL), F32),
        compiler_params=_cparams(1),
        name="merge_out",
    )(*x_args, mod, g1, w_g, attn, o_f, o_b, gain, wa, wl, wo)


FF_CHUNK = 1024


def _mlp_kernel(x_ref, mod_ref, g2_ref, w1_ref, w2_ref, o_ref, *, sub):
    for t in range(sub):
        rows = slice(t * TM, (t + 1) * TM)
        mod = _tile_mod(mod_ref, t)
        x = x_ref[rows, :]
        h = _norm_modulate(x, g2_ref[...], mod[:, 3 * D_MODEL:4 * D_MODEL], mod[:, 4 * D_MODEL:5 * D_MODEL])
        h = h.astype(BF16)
        acc = jnp.zeros((TM, D_MODEL), F32)
        for j in range(D_FF // FF_CHUNK):
            cs = slice(j * FF_CHUNK, (j + 1) * FF_CHUNK)
            u = jnp.maximum(_dot(h, w1_ref[:, cs]), 0.0)
            acc = acc + _dot((u * u).astype(BF16), w2_ref[cs, :])
        o_ref[rows, :] = x + mod[:, 5 * D_MODEL:6 * D_MODEL] * acc


def _mlp(layer, xc, mod, g2, w1, w2, latent_only):
    t_tot = xc.shape[0]
    if latent_only:
        sub = 1
        out_rows = t_tot - TM
        out_spec = pl.BlockSpec((TM, D_MODEL), lambda i: (jnp.maximum(i - 1, 0), 0))
    else:
        sub = SUB_TILES
        out_rows = t_tot
        out_spec = pl.BlockSpec((sub * TM, D_MODEL), lambda i: (i, 0))
    return pl.pallas_call(
        functools.partial(_mlp_kernel, sub=sub),
        grid=(t_tot // (sub * TM),),
        in_specs=[pl.BlockSpec((sub * TM, D_MODEL), lambda i: (i, 0)), _mod_pair_spec(layer), _layer_spec(g2, layer),
                  _layer_spec(w1, layer), _layer_spec(w2, layer)],
        out_specs=out_spec,
        out_shape=jax.ShapeDtypeStruct((out_rows, D_MODEL), F32),
        compiler_params=_cparams(1),
        name="mlp",
    )(xc, mod, g2, w1, w2)


def _rope_tables(n_lat, n_ctx):
    half = ROPE_AXIS_DIM // 2
    lane = np.arange(LANES)
    dd = lane % HEAD_DIM
    inv = (ROPE_BASE ** (-(dd % half).astype(np.float64) * 2.0 / ROPE_AXIS_DIM)).astype(np.float32)
    is_row = dd < ROPE_AXIS_DIM
    sign = np.where((dd % ROPE_AXIS_DIM) < half, -1.0, 1.0)
    rows = n_lat // GRID_W
    ang_r = np.arange(rows, dtype=np.float32).astype(np.float64)[:, None] * inv[None, :]
    ang_c = np.arange(GRID_W, dtype=np.float32).astype(np.float64)[:, None] * inv[None, :]
    m = jnp.asarray(is_row)[None, None, :]
    pick = lambda fr, fc: jnp.where(m, jnp.asarray(fr, F32)[:, None, :], jnp.asarray(fc, F32)[None, :, :])
    cos = pick(np.cos(ang_r), np.cos(ang_c)).reshape(n_lat, LANES)
    sin = pick(np.sin(ang_r) * sign, np.sin(ang_c) * sign).reshape(n_lat, LANES)
    cos = jnp.concatenate([jnp.ones((n_ctx, LANES), F32), cos], axis=0)
    sin = jnp.concatenate([jnp.zeros((n_ctx, LANES), F32), sin], axis=0)
    return cos, sin


def _tile_cumsum_matrices():
    i = np.arange(TM)[:, None]
    j = np.arange(TM)[None, :]
    return jnp.asarray(np.stack([j <= i, j >= i]).astype(np.float32) / GLA_NORMALIZER, BF16)


def kernel(x, c, ctx, c_ctx, w_mod, b_mod, g_norm1, w_in, q_gain, k_gain, sink, w_decay, b_decay,
           gla_gain, w_branch_attn, w_branch_gla, w_out, g_norm2, w_ff1, w_ff2):
    assert x.shape[0] == 1 and ctx.shape[0] == 1
    n_lat, n_ctx = x.shape[1], ctx.shape[1]
    assert n_ctx == TM and (n_ctx + n_lat) % (SUB_TILES * TM) == 0
    depth = w_mod.shape[0]

    cvec = jnp.zeros((MOD_ROWS, D_MODEL), F32).at[0].set(c[0]).at[1].set(c_ctx)
    mod = _mod_vectors(cvec, w_mod, b_mod).reshape(depth * MOD_ROWS, 1, N_MOD * D_MODEL)
    cos, sin = _rope_tables(n_lat, n_ctx)
    tri = _tile_cumsum_matrices()

    w_in_bf = w_in.astype(BF16)
    w_g = jnp.concatenate([w_in_bf[:, :, C_GR:C_GA], w_in_bf[:, :, C_MERGE:]], axis=2)
    wa = w_branch_attn.astype(BF16)
    wl = w_branch_gla.astype(BF16)
    wo = w_out.astype(BF16)
    w1 = w_ff1.astype(BF16)
    w2 = w_ff2.astype(BF16)
    wd = jnp.zeros((depth, 2, LANES, GLA_K), F32)
    wd = wd.at[:, 0, 0:GLA_RANK].set(w_decay[:, 0]).at[:, 1, GLA_RANK:2 * GLA_RANK].set(w_decay[:, 1]).astype(BF16)
    bd = b_decay.reshape(depth, 2, 1, GLA_K)
    g1 = g_norm1.reshape(depth, 1, D_MODEL)
    g2 = g_norm2.reshape(depth, 1, D_MODEL)
    qg = jnp.tile(q_gain, (1, LANES // HEAD_DIM)).reshape(depth, 1, LANES)
    kg = jnp.tile(k_gain, (1, LANES // HEAD_DIM)).reshape(depth, 1, LANES)
    gain = gla_gain.reshape(depth, 1, GLA_DV)
    sink_lanes = jnp.repeat((sink * LOG2E).reshape(depth, N_KV_HEADS, 1, GROUP), ATT_BLOCK, axis=3)

    xs = (ctx[0], x[0])
    for l in range(depth):
        q, k, vt, gq, gk, gv, b_fwd, b_bwd = _in_proj(l, xs, mod, g1, w_in_bf, qg, kg, cos, sin, wd, bd, tri)
        attn = _window_attn(l, q, k, vt, sink_lanes, n_ctx)
        o_f, o_b = _gla_scan(gq, gk, gv, b_fwd, b_bwd)
        xc = _merge_out(l, xs, mod, g1, w_g, attn, o_f, o_b, gain, wa, wl, wo)
        xc = _mlp(l, xc, mod, g2, w1, w2, latent_only=(l == depth - 1))
        xs = (xc,)
    return xc[None]
```

```python
import functools

import numpy as np
import jax
import jax.numpy as jnp
from jax import lax
from jax.experimental import pallas as pl
from jax.experimental.pallas import tpu as pltpu

F32 = jnp.float32
BF16 = jnp.bfloat16

D_MODEL = 1024
N_HEADS = 16
N_KV_HEADS = 4
HEAD_DIM = 64
GROUP = N_HEADS // N_KV_HEADS
ATT_BLOCK = 128
GRID_W = 64
ROPE_BASE = 10000.0
ROPE_AXIS_DIM = HEAD_DIM // 2
GLA_HEADS = 4
GLA_DK = 128
GLA_DV = 256
GLA_RANK = 16
GLA_NORMALIZER = 16.0
GLA_CHUNK = 64
D_FF = 4 * D_MODEL
N_MOD = 6
EPS = 1e-6
NEG = -1e30
LOG2E = 1.4426950408889634

ATT_Q = N_HEADS * HEAD_DIM
ATT_KV = N_KV_HEADS * HEAD_DIM
GLA_K = GLA_HEADS * GLA_DK
GLA_V = GLA_HEADS * GLA_DV

LANES = 128
TM = 256
SUB_TILES = 5
CHUNKS_PER_TILE = TM // GLA_CHUNK
MOD_ROWS = 8
VMEM_LIMIT = 56 * 1024 * 1024


def _cparams(n_axes):
    return pltpu.CompilerParams(dimension_semantics=("arbitrary",) * n_axes,
                                vmem_limit_bytes=VMEM_LIMIT)


def _dot(a, b):
    return jnp.dot(a, b, preferred_element_type=F32)


def _dot_nt(a, b):
    return lax.dot_general(a, b, (((1,), (1,)), ((), ())), preferred_element_type=F32)


def _dot_tn(a, b):
    return lax.dot_general(a, b, (((0,), (0,)), ((), ())), preferred_element_type=F32)


def _split2(x):
    x1 = x.astype(BF16)
    return x1, (x - x1.astype(F32)).astype(BF16)


def _layer_spec(arr, layer):
    rest = arr.shape[1:]
    return pl.BlockSpec((None,) + rest, lambda *_: (layer,) + (0,) * len(rest))


def _mod_pair_spec(layer):
    return pl.BlockSpec((2, 1, N_MOD * D_MODEL), lambda i: (layer * (MOD_ROWS // 2), 0, 0))


def _tile_mod(mod_ref, t):
    if t > 0:
        return mod_ref[0]
    return jnp.where(pl.program_id(0) == 0, mod_ref[1], mod_ref[0])


MOD_TN = 1536


def _mod_kernel(c_ref, w_ref, b_ref, o_ref):
    c = c_ref[...]
    s1, s2 = _split2(c * jax.nn.sigmoid(c))
    w1, w2 = _split2(w_ref[0])
    o_ref[0] = _dot(s1, w1) + _dot(s2, w1) + _dot(s1, w2) + b_ref[0]


def _mod_vectors(cvec, w_mod, b_mod):
    depth = w_mod.shape[0]
    n_out = w_mod.shape[2]
    return pl.pallas_call(
        _mod_kernel,
        grid=(depth, n_out // MOD_TN),
        in_specs=[pl.BlockSpec((MOD_ROWS, D_MODEL), lambda l, j: (0, 0)),
                  pl.BlockSpec((1, D_MODEL, MOD_TN), lambda l, j: (l, 0, j)),
                  pl.BlockSpec((1, 1, MOD_TN), lambda l, j: (l, 0, j))],
        out_specs=pl.BlockSpec((1, MOD_ROWS, MOD_TN), lambda l, j: (l, 0, j)),
        out_shape=jax.ShapeDtypeStruct((depth, MOD_ROWS, n_out), F32),
        compiler_params=_cparams(2),
        name="mod_vectors",
    )(cvec, w_mod, b_mod.reshape(depth, 1, n_out))


def _norm_modulate(x, g, shift, scale):
    ms = jnp.mean(x * x, axis=-1, keepdims=True)
    y = x * lax.rsqrt(ms + EPS) * g
    return y * (1.0 + scale) + shift


C_Q = 0
C_K = C_Q + ATT_Q
C_V = C_K + ATT_KV
C_GQ = C_V + ATT_KV
C_GK = C_GQ + GLA_K
C_GV = C_GK + GLA_K
C_GR = C_GV + GLA_V
C_GA = C_GR + GLA_V
C_END = C_GA + LANES
C_MERGE = C_GA + 2 * GLA_RANK
Q_SCALE = HEAD_DIM ** -0.5 * LOG2E


def _headnorm_rope(z, gain, cos, sin, lane):
    lo = lane < HEAD_DIM
    sq = z * z
    s_lo = jnp.sum(jnp.where(lo, sq, 0.0), axis=-1, keepdims=True)
    s_hi = jnp.sum(jnp.where(lo, 0.0, sq), axis=-1, keepdims=True)
    ms = jnp.where(lo, s_lo, s_hi) * (1.0 / HEAD_DIM)
    y = z * lax.rsqrt(ms + EPS) * gain
    half = ROPE_AXIS_DIM // 2
    nxt = pltpu.roll(y, LANES - half, 1)
    prv = pltpu.roll(y, half, 1)
    partner = jnp.where((lane % ROPE_AXIS_DIM) < half, nxt, prv)
    return y * cos + partner * sin


def _tile_input(x_refs, t):
    if len(x_refs) == 1:
        return x_refs[0][t * TM:(t + 1) * TM, :]
    if t > 0:
        return x_refs[1 + t][...]
    return jnp.where(pl.program_id(0) == 0, x_refs[0][...], x_refs[1][...])


def _tile_input_specs(xs):
    if len(xs) == 1:
        return [_step_spec(D_MODEL)]
    lat = lambda t: pl.BlockSpec((TM, D_MODEL), lambda i: (jnp.maximum(SUB_TILES * i + t - 1, 0), 0))
    return [pl.BlockSpec((TM, D_MODEL), lambda i: (0, 0))] + [lat(t) for t in range(SUB_TILES)]


def _tile_input_args(xs):
    return xs if len(xs) == 1 else (xs[0],) + (xs[1],) * SUB_TILES


def _step_spec(width):
    return pl.BlockSpec((SUB_TILES * TM, width), lambda i: (i, 0))


def _in_proj_kernel(*refs, n_x):
    x_refs = refs[:n_x]
    (mod_ref, g1_ref, w_ref, qg_ref, kg_ref, cos_ref, sin_ref, wd_ref, bd_ref, tri_ref,
     q_ref, k_ref, vt_ref, gq_ref, gk_ref, gv_ref, bf_ref, bb_ref) = refs[n_x:]
    for t in range(SUB_TILES):
        rows = slice(t * TM, (t + 1) * TM)
        _in_proj_tile(_tile_input(x_refs, t), _tile_mod(mod_ref, t), g1_ref, w_ref, qg_ref, kg_ref,
                      cos_ref[rows, :], sin_ref[rows, :], wd_ref, bd_ref, tri_ref,
                      q_ref.at[rows, :], k_ref.at[rows, :], vt_ref.at[:, rows], gq_ref.at[rows, :],
                      gk_ref.at[rows, :], gv_ref.at[rows, :], bf_ref.at[rows, :], bb_ref.at[rows, :])


def _in_proj_tile(x, mod, g1_ref, w_ref, qg_ref, kg_ref, cos, sin, wd_ref, bd_ref, tri_ref,
                  q_ref, k_ref, vt_ref, gq_ref, gk_ref, gv_ref, bf_ref, bb_ref):
    h = _norm_modulate(x, g1_ref[...], mod[:, 0:D_MODEL], mod[:, D_MODEL:2 * D_MODEL])
    h = h.astype(BF16)
    lane = lax.broadcasted_iota(jnp.int32, (TM, LANES), 1)
    lo = lane < HEAD_DIM
    qg = qg_ref[...]
    kg = kg_ref[...]
    ga = _dot(h, w_ref[:, C_GA:C_GA + LANES]).astype(BF16)
    decay_pre = [_dot(ga, wd_ref[d]) + bd_ref[d] for d in range(2)]
    for j in range(N_KV_HEADS // 2):
        w = 2 * GROUP * HEAD_DIM
        z = _dot(h, w_ref[:, C_Q + j * w:C_Q + (j + 1) * w])
        r = [_headnorm_rope(z[:, i * LANES:(i + 1) * LANES], qg, cos, sin, lane) * Q_SCALE
             for i in range(w // LANES)]
        for hh in range(GROUP):
            a = r[hh // 2]
            b = r[GROUP // 2 + hh // 2]
            if hh % 2 == 0:
                out = jnp.where(lo, a, pltpu.roll(b, HEAD_DIM, 1))
            else:
                out = jnp.where(lo, pltpu.roll(a, HEAD_DIM, 1), b)
            c0 = (j * GROUP + hh) * LANES
            q_ref[:, c0:c0 + LANES] = out.astype(BF16)
    for d, b_ref in enumerate((bf_ref, bb_ref)):
        x = decay_pre[d]
        u = jnp.exp2(jnp.abs(x) * (-LOG2E))
        g1, g2 = _split2(jnp.minimum(x, 0.0) * LOG2E - jnp.log2(1.0 + u))
        tri = tri_ref[d]
        b_ref[...] = _dot(tri, g1) + _dot(tri, g2)
    z = _dot(h, w_ref[:, C_K:C_K + ATT_KV])
    for e in range(2):
        r = _headnorm_rope(z[:, e * LANES:(e + 1) * LANES], kg, cos, sin, lane)
        k_ref[:, e * LANES:(e + 1) * LANES] = r.astype(BF16)
    z = _dot(h, w_ref[:, C_V:C_V + ATT_KV])
    vt_ref[...] = z.T.astype(BF16)
    gq_ref[...] = _dot(h, w_ref[:, C_GQ:C_GQ + GLA_K]).astype(BF16)
    gk_ref[...] = _dot(h, w_ref[:, C_GK:C_GK + GLA_K]).astype(BF16)
    for j in range(GLA_V // 512):
        gv_ref[:, j * 512:(j + 1) * 512] = _dot(h, w_ref[:, C_GV + j * 512:C_GV + (j + 1) * 512]).astype(BF16)


def _in_proj(layer, xs, mod, g1, w_in_bf, qg, kg, cos, sin, wd, bd, tri):
    t_tot = sum(a.shape[0] for a in xs)
    depth, d_model, _ = w_in_bf.shape
    x_args = _tile_input_args(xs)
    return pl.pallas_call(
        functools.partial(_in_proj_kernel, n_x=len(x_args)),
        grid=(t_tot // (SUB_TILES * TM),),
        in_specs=_tile_input_specs(xs) + [
            _mod_pair_spec(layer), _layer_spec(g1, layer),
            pl.BlockSpec((None, d_model, C_END), lambda i: (layer, 0, 0)),
            _layer_spec(qg, layer), _layer_spec(kg, layer), _step_spec(LANES), _step_spec(LANES),
            _layer_spec(wd, layer), _layer_spec(bd, layer), pl.BlockSpec(tri.shape, lambda i: (0, 0, 0))],
        out_specs=[_step_spec(ATT_Q), _step_spec(ATT_KV), pl.BlockSpec((ATT_KV, SUB_TILES * TM), lambda i: (0, i)),
                   _step_spec(GLA_K), _step_spec(GLA_K), _step_spec(GLA_V), _step_spec(GLA_K), _step_spec(GLA_K)],
        out_shape=[jax.ShapeDtypeStruct((t_tot, ATT_Q), BF16),
                   jax.ShapeDtypeStruct((t_tot, ATT_KV), BF16),
                   jax.ShapeDtypeStruct((ATT_KV, t_tot), BF16),
                   jax.ShapeDtypeStruct((t_tot, GLA_K), BF16),
                   jax.ShapeDtypeStruct((t_tot, GLA_K), BF16),
                   jax.ShapeDtypeStruct((t_tot, GLA_V), BF16),
                   jax.ShapeDtypeStruct((t_tot, GLA_K), F32),
                   jax.ShapeDtypeStruct((t_tot, GLA_K), F32)],
        compiler_params=_cparams(1),
        name="in_proj",
    )(*x_args, mod, g1, w_in_bf, qg, kg, cos, sin, wd, bd, tri)


ONES_ROWS = 16


ATT_STEP = SUB_TILES * TM


def _attn_blocks(q_ref, sink_ref, o_ref, key_lists):
    lane_q = lax.broadcasted_iota(jnp.int32, (ATT_BLOCK, LANES), 1)

    def scores(blk, g):
        j, e = divmod(g, 2)
        cs = slice(j * LANES, (j + 1) * LANES)
        qrows = slice(blk * ATT_BLOCK, (blk + 1) * ATT_BLOCK)
        half = (lane_q < HEAD_DIM) if e == 0 else (lane_q >= HEAD_DIM)
        qs = []
        for hh in range(GROUP):
            qc = q_ref[qrows, (j * GROUP + hh) * LANES:(j * GROUP + hh + 1) * LANES]
            qs.append(jnp.where(half, qc, jnp.zeros_like(qc)))
        qg = jnp.concatenate(qs, axis=0)
        keys = key_lists[blk]
        k_all = jnp.concatenate([kr[r0:r0 + ATT_BLOCK, cs] for kr, _, r0, _ in keys], axis=0)
        s = _dot_nt(k_all, qg)
        parts = []
        for i, (_, _, _, mk) in enumerate(keys):
            sp = s[i * ATT_BLOCK:(i + 1) * ATT_BLOCK]
            parts.append(sp if mk is None else jnp.where(mk, sp, NEG))
        return parts

    def finish(blk, g, parts):
        keys = key_lists[blk]
        sk = sink_ref[g]
        tile_max = parts[0]
        for sp in parts[1:]:
            tile_max = jnp.maximum(tile_max, sp)
        m = jnp.maximum(sk, jnp.max(tile_max, axis=0, keepdims=True))
        p = jnp.concatenate([jnp.exp2(sp - m) for sp in parts], axis=0).astype(BF16)
        rs = slice(g * HEAD_DIM, (g + 1) * HEAD_DIM)
        vt_all = jnp.concatenate([vr[rs, r0:r0 + ATT_BLOCK] for _, vr, r0, _ in keys], axis=1)
        ones = jnp.ones((ONES_ROWS, len(keys) * ATT_BLOCK), BF16)
        ot = _dot(jnp.concatenate([vt_all, ones], axis=0), p)
        denom = ot[HEAD_DIM:HEAD_DIM + 1] + jnp.exp2(sk - m)
        ot = ot[0:HEAD_DIM] * (1.0 / denom)
        qrows = slice(blk * ATT_BLOCK, (blk + 1) * ATT_BLOCK)
        for pair in range(GROUP // 2):
            x2 = jnp.concatenate([ot[:, (2 * pair) * ATT_BLOCK:(2 * pair + 1) * ATT_BLOCK],
                                  ot[:, (2 * pair + 1) * ATT_BLOCK:(2 * pair + 2) * ATT_BLOCK]], axis=0)
            c0 = g * GROUP * HEAD_DIM + pair * LANES
            o_ref[qrows, c0:c0 + LANES] = x2.T.astype(BF16)

    units = [(blk, g) for blk in range(len(key_lists)) for g in range(N_KV_HEADS)]
    pending = scores(*units[0])
    for i, unit in enumerate(units):
        upcoming = scores(*units[i + 1]) if i + 1 < len(units) else None
        finish(*unit, pending)
        pending = upcoming


def _attn_kernel(q_ref, kp_ref, kc_ref, kn_ref, kx_ref, vp_ref, vc_ref, vn_ref, vx_ref, sink_ref,
                 o_ref, *, n_steps, ctx_blocks):
    s = pl.program_id(0)
    n_ctx = kx_ref.shape[0]
    ctx_keys = [(kx_ref, vx_ref, r0, None) for r0 in range(0, n_ctx, ATT_BLOCK)]
    gw = GROUP * ATT_BLOCK
    key_i = lax.broadcasted_iota(jnp.int32, (ATT_BLOCK, gw), 0)
    qry_i = lax.broadcasted_iota(jnp.int32, (ATT_BLOCK, gw), 1) % ATT_BLOCK
    before = key_i >= qry_i
    after = key_i <= qry_i
    everything = key_i >= 0
    past_first = s > 0
    before_last = s < n_steps - 1
    blocks_per_step = ATT_STEP // ATT_BLOCK
    key_lists = []
    for b in range(blocks_per_step):
        prev = (kc_ref, vc_ref, (b - 1) * ATT_BLOCK) if b > 0 else (kp_ref, vp_ref, 0)
        cur = (kc_ref, vc_ref, b * ATT_BLOCK)
        nxt = (kc_ref, vc_ref, (b + 1) * ATT_BLOCK) if b + 1 < blocks_per_step else (kn_ref, vn_ref, 0)
        m_prev = jnp.logical_and(before, past_first) if b <= ctx_blocks else before
        m_cur = jnp.logical_and(everything, past_first) if b < ctx_blocks else None
        m_next = jnp.logical_and(after, past_first) if b < ctx_blocks else after
        if b + 1 == blocks_per_step:
            m_next = jnp.logical_and(after, before_last)
        key_lists.append([prev + (m_prev,), cur + (m_cur,), nxt + (m_next,)] + ctx_keys)
    _attn_blocks(q_ref, sink_ref, o_ref, key_lists)


def _window_attn(layer, q, k, vt, sink_lanes, n_ctx):
    t_tot = q.shape[0]
    assert t_tot % ATT_STEP == 0
    ns = t_tot // ATT_STEP
    nb = t_tot // ATT_BLOCK
    cb = n_ctx // ATT_BLOCK
    assert cb + 1 < ATT_STEP // ATT_BLOCK
    per = ATT_STEP // ATT_BLOCK
    prev_i = lambda s: jnp.clip(per * s - 1, 0, nb - 1)
    next_i = lambda s: jnp.clip(per * s + per, 0, nb - 1)
    ident = lambda s: s
    kspec = lambda rows, f: pl.BlockSpec((rows, ATT_KV), lambda s: (f(s), 0))
    vspec = lambda rows, f: pl.BlockSpec((ATT_KV, rows), lambda s: (0, f(s)))
    return pl.pallas_call(
        functools.partial(_attn_kernel, n_steps=ns, ctx_blocks=cb),
        grid=(ns,),
        in_specs=[pl.BlockSpec((ATT_STEP, ATT_Q), lambda s: (s, 0)),
                  kspec(ATT_BLOCK, prev_i), kspec(ATT_STEP, ident), kspec(ATT_BLOCK, next_i),
                  pl.BlockSpec((n_ctx, ATT_KV), lambda s: (0, 0)),
                  vspec(ATT_BLOCK, prev_i), vspec(ATT_STEP, ident), vspec(ATT_BLOCK, next_i),
                  pl.BlockSpec((ATT_KV, n_ctx), lambda s: (0, 0)),
                  _layer_spec(sink_lanes, layer)],
        out_specs=pl.BlockSpec((ATT_STEP, ATT_Q), lambda s: (s, 0)),
        out_shape=jax.ShapeDtypeStruct((t_tot, ATT_Q), BF16),
        compiler_params=_cparams(1),
        name="window_attn",
    )(q, k, k, k, k, vt, vt, vt, vt, sink_lanes)


def _gla_direction(d, gq_ref, gk_ref, gv_ref, b_ref, o_ref, s_ref):
    nc = CHUNKS_PER_TILE
    rows = lambda c: slice(c * GLA_CHUNK, (c + 1) * GLA_CHUNK)
    bcum = b_ref[...]
    zero_row = jnp.zeros((1, GLA_K), F32)
    if d == 0:
        earlier = lambda c: range(0, c)
        b_start = lambda c: zero_row if c == 0 else bcum[c * GLA_CHUNK - 1:c * GLA_CHUNK]
        b_end = lambda c: bcum[(c + 1) * GLA_CHUNK - 1:(c + 1) * GLA_CHUNK]
        last_chunk = nc - 1
    else:
        earlier = lambda c: range(c + 1, nc)
        b_start = lambda c: zero_row if c == nc - 1 else bcum[(c + 1) * GLA_CHUNK:(c + 1) * GLA_CHUNK + 1]
        b_end = lambda c: bcum[c * GLA_CHUNK:c * GLA_CHUNK + 1]
        last_chunk = 0
    total = b_end(last_chunk)
    gq = gq_ref[...].astype(F32) * (GLA_DK ** -0.5)
    gk = gk_ref[...].astype(F32)
    q_loc, q_tile, k_in, k_out, k_raw = [], [], [], [], []
    for c in range(nc):
        b_loc = bcum[rows(c)] - b_start(c)
        ql = gq[rows(c)] * jnp.exp2(b_loc)
        q_loc.append(ql.astype(BF16))
        q_tile.append((ql * jnp.exp2(b_start(c))).astype(BF16))
        k_in.append((gk[rows(c)] * jnp.exp2(-b_loc)).astype(BF16))
        k_out.append(gk[rows(c)] * jnp.exp2(b_end(c) - bcum[rows(c)]))
        k_raw.append(gk_ref[rows(c), :])
    k_out_bf = [k.astype(BF16) for k in k_out]

    def key_seen_from(c, src):
        if src == c:
            return k_in[c]
        if src not in earlier(c):
            return k_raw[src]
        if abs(src - c) == 1:
            return k_out_bf[src]
        return (k_out[src] * jnp.exp2(b_start(c) - b_end(src))).astype(BF16)

    k_dst = [jnp.concatenate([key_seen_from(c, src) for src in range(nc)], axis=0) for c in range(nc)]
    k_state = jnp.concatenate(
        [k_out_bf[c] if c == last_chunk else (k_out[c] * jnp.exp2(total - b_end(c))).astype(BF16)
         for c in range(nc)], axis=0)
    q_tile = jnp.concatenate(q_tile, axis=0)
    ti = lax.broadcasted_iota(jnp.int32, (TM, TM), 0)
    tj = lax.broadcasted_iota(jnp.int32, (TM, TM), 1)
    causal = (tj <= ti) if d == 0 else (tj >= ti)
    decay = jnp.exp2(total)

    for hh in range(GLA_HEADS):
        kc = slice(hh * GLA_DK, (hh + 1) * GLA_DK)
        vc = slice(hh * GLA_DV, (hh + 1) * GLA_DV)
        a = jnp.concatenate([_dot_nt(q_loc[c][:, kc], k_dst[c][:, kc]) for c in range(nc)], axis=0)
        a = jnp.where(causal, a, 0.0).astype(BF16)
        v = gv_ref[:, vc]
        s_old = s_ref[hh]
        o_ref[:, vc] = (_dot(a, v) + _dot(q_tile[:, kc], s_old.astype(BF16))).astype(BF16)
        decay_col = jnp.broadcast_to(decay[:, kc], (GLA_DK, GLA_DK)).T
        decay_col = jnp.concatenate([decay_col] * (GLA_DV // GLA_DK), axis=1)
        s_ref[hh] = s_old * decay_col + _dot_tn(k_state[:, kc], v)


def _backward_scan_tile(pos, n_tiles):
    return jnp.where(pos == 0, 0, n_tiles - pos)


def _gla_kernel(*refs):
    gqf, gkf, gvf, bf = refs[0:4]
    bwd_in = refs[4:4 + 4 * SUB_TILES]
    of_ref, ob_ref, sf_ref, sb_ref = refs[4 + 4 * SUB_TILES:]

    @pl.when(pl.program_id(0) == 0)
    def _():
        sf_ref[...] = jnp.zeros_like(sf_ref)
        sb_ref[...] = jnp.zeros_like(sb_ref)

    for t in range(SUB_TILES):
        rows = slice(t * TM, (t + 1) * TM)
        _gla_direction(0, gqf.at[rows, :], gkf.at[rows, :], gvf.at[rows, :], bf.at[rows, :],
                       of_ref.at[rows, :], sf_ref)
        gqb, gkb, gvb, bb = bwd_in[4 * t:4 * t + 4]
        _gla_direction(1, gqb, gkb, gvb, bb, ob_ref.at[rows, :], sb_ref)


def _gla_scan(gq, gk, gv, b_fwd, b_bwd):
    t_tot = gq.shape[0]
    nt = t_tot // TM
    bwd_spec = lambda w, t: pl.BlockSpec(
        (TM, w), lambda s: (_backward_scan_tile(SUB_TILES * s + t, nt), 0))
    ins = [_step_spec(GLA_K), _step_spec(GLA_K), _step_spec(GLA_V), _step_spec(GLA_K)]
    args = [gq, gk, gv, b_fwd]
    for t in range(SUB_TILES):
        ins += [bwd_spec(GLA_K, t), bwd_spec(GLA_K, t), bwd_spec(GLA_V, t), bwd_spec(GLA_K, t)]
        args += [gq, gk, gv, b_bwd]
    return pl.pallas_call(
        _gla_kernel,
        grid=(t_tot // (SUB_TILES * TM),),
        in_specs=ins,
        out_specs=[_step_spec(GLA_V), _step_spec(GLA_V)],
        out_shape=[jax.ShapeDtypeStruct((t_tot, GLA_V), BF16)] * 2,
        scratch_shapes=[pltpu.VMEM((GLA_HEADS, GLA_DK, GLA_DV), F32)] * 2,
        compiler_params=_cparams(1),
        name="gla_scan",
    )(*args)


def _merge_kernel(*refs, n_x):
    x_refs = refs[:n_x]
    mod_ref, g1_ref, wg_ref, attn_ref, of_ref = refs[n_x:n_x + 5]
    ob_refs = refs[n_x + 5:n_x + 5 + SUB_TILES]
    gain_ref, wa_ref, wl_ref, wo_ref, o_ref = refs[n_x + 5 + SUB_TILES:]
    gain = gain_ref[...]
    for t in range(SUB_TILES):
        rows = slice(t * TM, (t + 1) * TM)
        mod = _tile_mod(mod_ref, t)
        x = _tile_input(x_refs, t)
        h = _norm_modulate(x, g1_ref[...], mod[:, 0:D_MODEL], mod[:, D_MODEL:2 * D_MODEL]).astype(BF16)
        parts = []
        for hh in range(GLA_HEADS):
            vc = slice(hh * GLA_DV, (hh + 1) * GLA_DV)
            o = of_ref[rows, vc].astype(F32) + ob_refs[t][:, vc].astype(F32)
            ms = jnp.mean(o * o, axis=-1, keepdims=True)
            on = o * lax.rsqrt(ms + EPS) * gain
            gr = _dot(h, wg_ref[:, vc])
            parts.append((on * (gr * jax.nn.sigmoid(gr))).astype(BF16))
        gla = jnp.concatenate(parts, axis=1)
        gate_a = jax.nn.sigmoid(_dot(h, wg_ref[:, GLA_V:GLA_V + D_MODEL]))
        gate_b = jax.nn.sigmoid(_dot(h, wg_ref[:, GLA_V + D_MODEL:GLA_V + 2 * D_MODEL]))
        y = gate_a * _dot(attn_ref[rows, :], wa_ref[...]) + gate_b * _dot(gla, wl_ref[...])
        gt1 = mod[:, 2 * D_MODEL:3 * D_MODEL]
        o_ref[rows, :] = x + gt1 * _dot(y.astype(BF16), wo_ref[...])


def _merge_out(layer, xs, mod, g1, w_g, attn, o_f, o_b, gain, wa, wl, wo):
    t_tot = sum(a.shape[0] for a in xs)
    nt = t_tot // TM
    x_args = _tile_input_args(xs)
    ob_spec = lambda t: pl.BlockSpec((TM, GLA_V), lambda i: (_backward_scan_tile(SUB_TILES * i + t, nt), 0))
    return pl.pallas_call(
        functools.partial(_merge_kernel, n_x=len(x_args)),
        grid=(t_tot // (SUB_TILES * TM),),
        in_specs=_tile_input_specs(xs) + [
            _mod_pair_spec(layer), _layer_spec(g1, layer), _layer_spec(w_g, layer),
            _step_spec(ATT_Q), _step_spec(GLA_V)] + [ob_spec(t) for t in range(SUB_TILES)] + [
            _layer_spec(gain, layer), _layer_spec(wa, layer), _layer_spec(wl, layer), _layer_spec(wo, layer)],
        out_specs=_step_spec(D_MODEL),
        out_shape=jax.ShapeDtypeStruct((t_tot, D_MODEL), F32),
        compiler_params=_cparams(1),
        name="merge_out",
    )(*x_args, mod, g1, w_g, attn, o_f, *([o_b] * SUB_TILES), gain, wa, wl, wo)


FF_CHUNK = 1024


def _mlp_kernel(x_ref, mod_ref, g2_ref, w1_ref, w2_ref, o_ref, *, sub):
    for t in range(sub):
        rows = slice(t * TM, (t + 1) * TM)
        mod = _tile_mod(mod_ref, t)
        x = x_ref[rows, :]
        h = _norm_modulate(x, g2_ref[...], mod[:, 3 * D_MODEL:4 * D_MODEL], mod[:, 4 * D_MODEL:5 * D_MODEL])
        h = h.astype(BF16)
        acc = jnp.zeros((TM, D_MODEL), F32)
        for j in range(D_FF // FF_CHUNK):
            cs = slice(j * FF_CHUNK, (j + 1) * FF_CHUNK)
            u = jnp.maximum(_dot(h, w1_ref[:, cs]), 0.0)
            acc = acc + _dot((u * u).astype(BF16), w2_ref[cs, :])
        o_ref[rows, :] = x + mod[:, 5 * D_MODEL:6 * D_MODEL] * acc


def _mlp(layer, xc, mod, g2, w1, w2, latent_only):
    t_tot = xc.shape[0]
    if latent_only:
        sub = 1
        out_rows = t_tot - TM
        out_spec = pl.BlockSpec((TM, D_MODEL), lambda i: (jnp.maximum(i - 1, 0), 0))
    else:
        sub = SUB_TILES
        out_rows = t_tot
        out_spec = pl.BlockSpec((sub * TM, D_MODEL), lambda i: (i, 0))
    return pl.pallas_call(
        functools.partial(_mlp_kernel, sub=sub),
        grid=(t_tot // (sub * TM),),
        in_specs=[pl.BlockSpec((sub * TM, D_MODEL), lambda i: (i, 0)), _mod_pair_spec(layer), _layer_spec(g2, layer),
                  _layer_spec(w1, layer), _layer_spec(w2, layer)],
        out_specs=out_spec,
        out_shape=jax.ShapeDtypeStruct((out_rows, D_MODEL), F32),
        compiler_params=_cparams(1),
        name="mlp",
    )(xc, mod, g2, w1, w2)


def _rope_tables(n_lat, n_ctx):
    half = ROPE_AXIS_DIM // 2
    lane = np.arange(LANES)
    dd = lane % HEAD_DIM
    inv = (ROPE_BASE ** (-(dd % half).astype(np.float64) * 2.0 / ROPE_AXIS_DIM)).astype(np.float32)
    is_row = dd < ROPE_AXIS_DIM
    sign = np.where((dd % ROPE_AXIS_DIM) < half, -1.0, 1.0)
    rows = n_lat // GRID_W
    ang_r = np.arange(rows, dtype=np.float32).astype(np.float64)[:, None] * inv[None, :]
    ang_c = np.arange(GRID_W, dtype=np.float32).astype(np.float64)[:, None] * inv[None, :]
    m = jnp.asarray(is_row)[None, None, :]
    pick = lambda fr, fc: jnp.where(m, jnp.asarray(fr, F32)[:, None, :], jnp.asarray(fc, F32)[None, :, :])
    cos = pick(np.cos(ang_r), np.cos(ang_c)).reshape(n_lat, LANES)
    sin = pick(np.sin(ang_r) * sign, np.sin(ang_c) * sign).reshape(n_lat, LANES)
    cos = jnp.concatenate([jnp.ones((n_ctx, LANES), F32), cos], axis=0)
    sin = jnp.concatenate([jnp.zeros((n_ctx, LANES), F32), sin], axis=0)
    return cos, sin


def _tile_cumsum_matrices():
    i = np.arange(TM)[:, None]
    j = np.arange(TM)[None, :]
    return jnp.asarray(np.stack([j <= i, j >= i]).astype(np.float32) / GLA_NORMALIZER, BF16)


def kernel(x, c, ctx, c_ctx, w_mod, b_mod, g_norm1, w_in, q_gain, k_gain, sink, w_decay, b_decay,
           gla_gain, w_branch_attn, w_branch_gla, w_out, g_norm2, w_ff1, w_ff2):
    assert x.shape[0] == 1 and ctx.shape[0] == 1
    n_lat, n_ctx = x.shape[1], ctx.shape[1]
    assert n_ctx == TM and (n_ctx + n_lat) % (SUB_TILES * TM) == 0
    depth = w_mod.shape[0]

    cvec = jnp.zeros((MOD_ROWS, D_MODEL), F32).at[0].set(c[0]).at[1].set(c_ctx)
    mod = _mod_vectors(cvec, w_mod, b_mod).reshape(depth * MOD_ROWS, 1, N_MOD * D_MODEL)
    cos, sin = _rope_tables(n_lat, n_ctx)
    tri = _tile_cumsum_matrices()

    w_in_bf = w_in.astype(BF16)
    w_g = jnp.concatenate([w_in_bf[:, :, C_GR:C_GA], w_in_bf[:, :, C_MERGE:]], axis=2)
    wa = w_branch_attn.astype(BF16)
    wl = w_branch_gla.astype(BF16)
    wo = w_out.astype(BF16)
    w1 = w_ff1.astype(BF16)
    w2 = w_ff2.astype(BF16)
    wd = jnp.zeros((depth, 2, LANES, GLA_K), F32)
    wd = wd.at[:, 0, 0:GLA_RANK].set(w_decay[:, 0]).at[:, 1, GLA_RANK:2 * GLA_RANK].set(w_decay[:, 1]).astype(BF16)
    bd = b_decay.reshape(depth, 2, 1, GLA_K)
    g1 = g_norm1.reshape(depth, 1, D_MODEL)
    g2 = g_norm2.reshape(depth, 1, D_MODEL)
    qg = jnp.tile(q_gain, (1, LANES // HEAD_DIM)).reshape(depth, 1, LANES)
    kg = jnp.tile(k_gain, (1, LANES // HEAD_DIM)).reshape(depth, 1, LANES)
    gain = gla_gain.reshape(depth, 1, GLA_DV)
    sink_lanes = jnp.repeat((sink * LOG2E).reshape(depth, N_KV_HEADS, 1, GROUP), ATT_BLOCK, axis=3)

    xs = (ctx[0], x[0])
    for l in range(depth):
        q, k, vt, gq, gk, gv, b_fwd, b_bwd = _in_proj(l, xs, mod, g1, w_in_bf, qg, kg, cos, sin, wd, bd, tri)
        attn = _window_attn(l, q, k, vt, sink_lanes, n_ctx)
        o_f, o_b = _gla_scan(gq, gk, gv, b_fwd, b_bwd)
        xc = _merge_out(l, xs, mod, g1, w_g, attn, o_f, o_b, gain, wa, wl, wo)
        xc = _mlp(l, xc, mod, g2, w1, w2, latent_only=(l == depth - 1))
        xs = (xc,)
    return xc[None]
```

```python
import functools

import numpy as np
import jax
import jax.numpy as jnp
from jax import lax
from jax.experimental import pallas as pl
from jax.experimental.pallas import tpu as pltpu

F32 = jnp.float32
BF16 = jnp.bfloat16

D_MODEL = 1024
N_HEADS = 16
N_KV_HEADS = 4
HEAD_DIM = 64
GROUP = N_HEADS // N_KV_HEADS
ATT_BLOCK = 128
GRID_W = 64
ROPE_BASE = 10000.0
ROPE_AXIS_DIM = HEAD_DIM // 2
GLA_HEADS = 4
GLA_DK = 128
GLA_DV = 256
GLA_RANK = 16
GLA_NORMALIZER = 16.0
GLA_CHUNK = 64
D_FF = 4 * D_MODEL
N_MOD = 6
EPS = 1e-6
NEG = -1e30
LOG2E = 1.4426950408889634

ATT_Q = N_HEADS * HEAD_DIM
ATT_KV = N_KV_HEADS * HEAD_DIM
GLA_K = GLA_HEADS * GLA_DK
GLA_V = GLA_HEADS * GLA_DV

LANES = 128
TM = 256
SUB_TILES = 5
CHUNKS_PER_TILE = TM // GLA_CHUNK
MOD_ROWS = 8
VMEM_LIMIT = 56 * 1024 * 1024


def _cparams(n_axes):
    return pltpu.CompilerParams(dimension_semantics=("arbitrary",) * n_axes,
                                vmem_limit_bytes=VMEM_LIMIT)


def _dot(a, b):
    return jnp.dot(a, b, preferred_element_type=F32)


def _dot_nt(a, b):
    return lax.dot_general(a, b, (((1,), (1,)), ((), ())), preferred_element_type=F32)


def _dot_tn(a, b):
    return lax.dot_general(a, b, (((0,), (0,)), ((), ())), preferred_element_type=F32)


def _split2(x):
    x1 = x.astype(BF16)
    return x1, (x - x1.astype(F32)).astype(BF16)


def _layer_spec(arr, layer):
    rest = arr.shape[1:]
    return pl.BlockSpec((None,) + rest, lambda *_: (layer,) + (0,) * len(rest))


def _mod_pair_spec(layer):
    return pl.BlockSpec((2, 1, N_MOD * D_MODEL), lambda i: (layer * (MOD_ROWS // 2), 0, 0))


def _tile_mod(mod_ref, t):
    if t > 0:
        return mod_ref[0]
    return jnp.where(pl.program_id(0) == 0, mod_ref[1], mod_ref[0])


MOD_TN = 1536


def _mod_kernel(c_ref, w_ref, b_ref, o_ref):
    c = c_ref[...]
    s1, s2 = _split2(c * jax.nn.sigmoid(c))
    w1, w2 = _split2(w_ref[0])
    o_ref[0] = _dot(s1, w1) + _dot(s2, w1) + _dot(s1, w2) + b_ref[0]


def _mod_vectors(cvec, w_mod, b_mod):
    depth = w_mod.shape[0]
    n_out = w_mod.shape[2]
    return pl.pallas_call(
        _mod_kernel,
        grid=(depth, n_out // MOD_TN),
        in_specs=[pl.BlockSpec((MOD_ROWS, D_MODEL), lambda l, j: (0, 0)),
                  pl.BlockSpec((1, D_MODEL, MOD_TN), lambda l, j: (l, 0, j)),
                  pl.BlockSpec((1, 1, MOD_TN), lambda l, j: (l, 0, j))],
        out_specs=pl.BlockSpec((1, MOD_ROWS, MOD_TN), lambda l, j: (l, 0, j)),
        out_shape=jax.ShapeDtypeStruct((depth, MOD_ROWS, n_out), F32),
        compiler_params=_cparams(2),
        name="mod_vectors",
    )(cvec, w_mod, b_mod.reshape(depth, 1, n_out))


def _norm_modulate(x, g, shift, scale):
    ms = jnp.mean(x * x, axis=-1, keepdims=True)
    y = x * lax.rsqrt(ms + EPS) * g
    return y * (1.0 + scale) + shift


C_Q = 0
C_K = C_Q + ATT_Q
C_V = C_K + ATT_KV
C_GQ = C_V + ATT_KV
C_GK = C_GQ + GLA_K
C_GV = C_GK + GLA_K
C_GR = C_GV + GLA_V
C_GA = C_GR + GLA_V
C_END = C_GA + LANES
C_MERGE = C_GA + 2 * GLA_RANK
Q_SCALE = HEAD_DIM ** -0.5 * LOG2E


def _headnorm_rope(z, gain, cos, sin, lane):
    lo = lane < HEAD_DIM
    sq = z * z
    s_lo = jnp.sum(jnp.where(lo, sq, 0.0), axis=-1, keepdims=True)
    s_hi = jnp.sum(jnp.where(lo, 0.0, sq), axis=-1, keepdims=True)
    ms = jnp.where(lo, s_lo, s_hi) * (1.0 / HEAD_DIM)
    y = z * lax.rsqrt(ms + EPS) * gain
    half = ROPE_AXIS_DIM // 2
    nxt = pltpu.roll(y, LANES - half, 1)
    prv = pltpu.roll(y, half, 1)
    partner = jnp.where((lane % ROPE_AXIS_DIM) < half, nxt, prv)
    return y * cos + partner * sin


def _tile_input(x_refs, t):
    if len(x_refs) == 1:
        return x_refs[0][t * TM:(t + 1) * TM, :]
    if t > 0:
        return x_refs[1 + t][...]
    return jnp.where(pl.program_id(0) == 0, x_refs[0][...], x_refs[1][...])


def _tile_input_specs(xs):
    if len(xs) == 1:
        return [_step_spec(D_MODEL)]
    lat = lambda t: pl.BlockSpec((TM, D_MODEL), lambda i: (jnp.maximum(SUB_TILES * i + t - 1, 0), 0))
    return [pl.BlockSpec((TM, D_MODEL), lambda i: (0, 0))] + [lat(t) for t in range(SUB_TILES)]


def _tile_input_args(xs):
    return xs if len(xs) == 1 else (xs[0],) + (xs[1],) * SUB_TILES


def _step_spec(width):
    return pl.BlockSpec((SUB_TILES * TM, width), lambda i: (i, 0))


def _in_proj_kernel(*refs, n_x):
    x_refs = refs[:n_x]
    (mod_ref, g1_ref, w_ref, qg_ref, kg_ref, cos_ref, sin_ref, wd_ref, bd_ref, tri_ref,
     q_ref, k_ref, vt_ref, gq_ref, gk_ref, gv_ref, bf_ref, bb_ref) = refs[n_x:]
    for t in range(SUB_TILES):
        rows = slice(t * TM, (t + 1) * TM)
        _in_proj_tile(_tile_input(x_refs, t), _tile_mod(mod_ref, t), g1_ref, w_ref, qg_ref, kg_ref,
                      cos_ref[rows, :], sin_ref[rows, :], wd_ref, bd_ref, tri_ref,
                      q_ref.at[rows, :], k_ref.at[rows, :], vt_ref.at[:, rows], gq_ref.at[rows, :],
                      gk_ref.at[rows, :], gv_ref.at[rows, :], bf_ref.at[rows, :], bb_ref.at[rows, :])


def _in_proj_tile(x, mod, g1_ref, w_ref, qg_ref, kg_ref, cos, sin, wd_ref, bd_ref, tri_ref,
                  q_ref, k_ref, vt_ref, gq_ref, gk_ref, gv_ref, bf_ref, bb_ref):
    h = _norm_modulate(x, g1_ref[...], mod[:, 0:D_MODEL], mod[:, D_MODEL:2 * D_MODEL])
    h = h.astype(BF16)
    lane = lax.broadcasted_iota(jnp.int32, (TM, LANES), 1)
    lo = lane < HEAD_DIM
    qg = qg_ref[...]
    kg = kg_ref[...]
    ga = _dot(h, w_ref[:, C_GA:C_GA + LANES]).astype(BF16)
    decay_pre = [_dot(ga, wd_ref[d]) + bd_ref[d] for d in range(2)]
    for j in range(N_KV_HEADS // 2):
        w = 2 * GROUP * HEAD_DIM
        z = _dot(h, w_ref[:, C_Q + j * w:C_Q + (j + 1) * w])
        r = [_headnorm_rope(z[:, i * LANES:(i + 1) * LANES], qg, cos, sin, lane) * Q_SCALE
             for i in range(w // LANES)]
        for hh in range(GROUP):
            a = r[hh // 2]
            b = r[GROUP // 2 + hh // 2]
            if hh % 2 == 0:
                out = jnp.where(lo, a, pltpu.roll(b, HEAD_DIM, 1))
            else:
                out = jnp.where(lo, pltpu.roll(a, HEAD_DIM, 1), b)
            c0 = (j * GROUP + hh) * LANES
            q_ref[:, c0:c0 + LANES] = out.astype(BF16)
    for d, b_ref in enumerate((bf_ref, bb_ref)):
        x = decay_pre[d]
        u = jnp.exp2(jnp.abs(x) * (-LOG2E))
        g1, g2 = _split2(jnp.minimum(x, 0.0) * LOG2E - jnp.log2(1.0 + u))
        tri = tri_ref[d]
        b_ref[...] = _dot(tri, g1) + _dot(tri, g2)
    z = _dot(h, w_ref[:, C_K:C_K + ATT_KV])
    for e in range(2):
        r = _headnorm_rope(z[:, e * LANES:(e + 1) * LANES], kg, cos, sin, lane)
        k_ref[:, e * LANES:(e + 1) * LANES] = r.astype(BF16)
    z = _dot(h, w_ref[:, C_V:C_V + ATT_KV])
    vt_ref[...] = z.T.astype(BF16)
    gq_ref[...] = _dot(h, w_ref[:, C_GQ:C_GQ + GLA_K]).astype(BF16)
    gk_ref[...] = _dot(h, w_ref[:, C_GK:C_GK + GLA_K]).astype(BF16)
    for j in range(GLA_V // 512):
        gv_ref[:, j * 512:(j + 1) * 512] = _dot(h, w_ref[:, C_GV + j * 512:C_GV + (j + 1) * 512]).astype(BF16)


def _in_proj(layer, xs, mod, g1, w_in_bf, qg, kg, cos, sin, wd, bd, tri):
    t_tot = sum(a.shape[0] for a in xs)
    depth, d_model, _ = w_in_bf.shape
    x_args = _tile_input_args(xs)
    return pl.pallas_call(
        functools.partial(_in_proj_kernel, n_x=len(x_args)),
        grid=(t_tot // (SUB_TILES * TM),),
        in_specs=_tile_input_specs(xs) + [
            _mod_pair_spec(layer), _layer_spec(g1, layer),
            pl.BlockSpec((None, d_model, C_END), lambda i: (layer, 0, 0)),
            _layer_spec(qg, layer), _layer_spec(kg, layer), _step_spec(LANES), _step_spec(LANES),
            _layer_spec(wd, layer), _layer_spec(bd, layer), pl.BlockSpec(tri.shape, lambda i: (0, 0, 0))],
        out_specs=[_step_spec(ATT_Q), _step_spec(ATT_KV), pl.BlockSpec((ATT_KV, SUB_TILES * TM), lambda i: (0, i)),
                   _step_spec(GLA_K), _step_spec(GLA_K), _step_spec(GLA_V), _step_spec(GLA_K), _step_spec(GLA_K)],
        out_shape=[jax.ShapeDtypeStruct((t_tot, ATT_Q), BF16),
                   jax.ShapeDtypeStruct((t_tot, ATT_KV), BF16),
                   jax.ShapeDtypeStruct((ATT_KV, t_tot), BF16),
                   jax.ShapeDtypeStruct((t_tot, GLA_K), BF16),
                   jax.ShapeDtypeStruct((t_tot, GLA_K), BF16),
                   jax.ShapeDtypeStruct((t_tot, GLA_V), BF16),
                   jax.ShapeDtypeStruct((t_tot, GLA_K), F32),
                   jax.ShapeDtypeStruct((t_tot, GLA_K), F32)],
        compiler_params=_cparams(1),
        name="in_proj",
    )(*x_args, mod, g1, w_in_bf, qg, kg, cos, sin, wd, bd, tri)


ONES_ROWS = 16
BOUND_MARGIN = 1.01
MAX_SAFE_BOUND = 60.0


ATT_STEP = SUB_TILES * TM


def _attn_blocks(q_ref, sink_ref, stab_ref, o_ref, key_lists):
    lane_q = lax.broadcasted_iota(jnp.int32, (ATT_BLOCK, LANES), 1)

    def scores(blk, g):
        j, e = divmod(g, 2)
        cs = slice(j * LANES, (j + 1) * LANES)
        qrows = slice(blk * ATT_BLOCK, (blk + 1) * ATT_BLOCK)
        half = (lane_q < HEAD_DIM) if e == 0 else (lane_q >= HEAD_DIM)
        qs = []
        for hh in range(GROUP):
            qc = q_ref[qrows, (j * GROUP + hh) * LANES:(j * GROUP + hh + 1) * LANES]
            qs.append(jnp.where(half, qc, jnp.zeros_like(qc)))
        qg = jnp.concatenate(qs, axis=0)
        keys = key_lists[blk]
        k_all = jnp.concatenate([kr[r0:r0 + ATT_BLOCK, cs] for kr, _, r0, _ in keys], axis=0)
        s = _dot_nt(k_all, qg)
        parts = []
        for i, (_, _, _, mk) in enumerate(keys):
            sp = s[i * ATT_BLOCK:(i + 1) * ATT_BLOCK]
            parts.append(sp if mk is None else jnp.where(mk, sp, NEG))
        return parts

    def finish(blk, g, parts):
        keys = key_lists[blk]
        sk = sink_ref[g]
        if stab_ref is None:
            tile_max = parts[0]
            for sp in parts[1:]:
                tile_max = jnp.maximum(tile_max, sp)
            m = jnp.maximum(sk, jnp.max(tile_max, axis=0, keepdims=True))
        else:
            m = stab_ref[g]
        p = jnp.concatenate([jnp.exp2(sp - m) for sp in parts], axis=0).astype(BF16)
        rs = slice(g * HEAD_DIM, (g + 1) * HEAD_DIM)
        vt_all = jnp.concatenate([vr[rs, r0:r0 + ATT_BLOCK] for _, vr, r0, _ in keys], axis=1)
        ones = jnp.ones((ONES_ROWS, len(keys) * ATT_BLOCK), BF16)
        ot = _dot(jnp.concatenate([vt_all, ones], axis=0), p)
        denom = ot[HEAD_DIM:HEAD_DIM + 1] + jnp.exp2(sk - m)
        ot = ot[0:HEAD_DIM] * (1.0 / denom)
        qrows = slice(blk * ATT_BLOCK, (blk + 1) * ATT_BLOCK)
        for pair in range(GROUP // 2):
            x2 = jnp.concatenate([ot[:, (2 * pair) * ATT_BLOCK:(2 * pair + 1) * ATT_BLOCK],
                                  ot[:, (2 * pair + 1) * ATT_BLOCK:(2 * pair + 2) * ATT_BLOCK]], axis=0)
            c0 = g * GROUP * HEAD_DIM + pair * LANES
            o_ref[qrows, c0:c0 + LANES] = x2.T.astype(BF16)

    units = [(blk, g) for blk in range(len(key_lists)) for g in range(N_KV_HEADS)]
    pending = scores(*units[0])
    for i, unit in enumerate(units):
        upcoming = scores(*units[i + 1]) if i + 1 < len(units) else None
        finish(*unit, pending)
        pending = upcoming


def _attn_kernel(q_ref, kp_ref, kc_ref, kn_ref, kx_ref, vp_ref, vc_ref, vn_ref, vx_ref, sink_ref,
                 *rest, n_steps, ctx_blocks):
    stab_ref, o_ref = rest if len(rest) == 2 else (None, rest[0])
    s = pl.program_id(0)
    n_ctx = kx_ref.shape[0]
    ctx_keys = [(kx_ref, vx_ref, r0, None) for r0 in range(0, n_ctx, ATT_BLOCK)]
    gw = GROUP * ATT_BLOCK
    key_i = lax.broadcasted_iota(jnp.int32, (ATT_BLOCK, gw), 0)
    qry_i = lax.broadcasted_iota(jnp.int32, (ATT_BLOCK, gw), 1) % ATT_BLOCK
    before = key_i >= qry_i
    after = key_i <= qry_i
    everything = key_i >= 0
    past_first = s > 0
    before_last = s < n_steps - 1
    blocks_per_step = ATT_STEP // ATT_BLOCK
    key_lists = []
    for b in range(blocks_per_step):
        prev = (kc_ref, vc_ref, (b - 1) * ATT_BLOCK) if b > 0 else (kp_ref, vp_ref, 0)
        cur = (kc_ref, vc_ref, b * ATT_BLOCK)
        nxt = (kc_ref, vc_ref, (b + 1) * ATT_BLOCK) if b + 1 < blocks_per_step else (kn_ref, vn_ref, 0)
        m_prev = jnp.logical_and(before, past_first) if b <= ctx_blocks else before
        m_cur = jnp.logical_and(everything, past_first) if b < ctx_blocks else None
        m_next = jnp.logical_and(after, past_first) if b < ctx_blocks else after
        if b + 1 == blocks_per_step:
            m_next = jnp.logical_and(after, before_last)
        key_lists.append([prev + (m_prev,), cur + (m_cur,), nxt + (m_next,)] + ctx_keys)
    _attn_blocks(q_ref, sink_ref, stab_ref, o_ref, key_lists)


def _window_attn(layer, q, k, vt, sink_lanes, stab_lanes, n_ctx):
    t_tot = q.shape[0]
    stab = [] if stab_lanes is None else [stab_lanes]
    assert t_tot % ATT_STEP == 0
    ns = t_tot // ATT_STEP
    nb = t_tot // ATT_BLOCK
    cb = n_ctx // ATT_BLOCK
    assert cb + 1 < ATT_STEP // ATT_BLOCK
    per = ATT_STEP // ATT_BLOCK
    prev_i = lambda s: jnp.clip(per * s - 1, 0, nb - 1)
    next_i = lambda s: jnp.clip(per * s + per, 0, nb - 1)
    ident = lambda s: s
    kspec = lambda rows, f: pl.BlockSpec((rows, ATT_KV), lambda s: (f(s), 0))
    vspec = lambda rows, f: pl.BlockSpec((ATT_KV, rows), lambda s: (0, f(s)))
    return pl.pallas_call(
        functools.partial(_attn_kernel, n_steps=ns, ctx_blocks=cb),
        grid=(ns,),
        in_specs=[pl.BlockSpec((ATT_STEP, ATT_Q), lambda s: (s, 0)),
                  kspec(ATT_BLOCK, prev_i), kspec(ATT_STEP, ident), kspec(ATT_BLOCK, next_i),
                  pl.BlockSpec((n_ctx, ATT_KV), lambda s: (0, 0)),
                  vspec(ATT_BLOCK, prev_i), vspec(ATT_STEP, ident), vspec(ATT_BLOCK, next_i),
                  pl.BlockSpec((ATT_KV, n_ctx), lambda s: (0, 0)),
                  _layer_spec(sink_lanes, layer)] + [_layer_spec(a, layer) for a in stab],
        out_specs=pl.BlockSpec((ATT_STEP, ATT_Q), lambda s: (s, 0)),
        out_shape=jax.ShapeDtypeStruct((t_tot, ATT_Q), BF16),
        compiler_params=_cparams(1),
        name="window_attn" if stab_lanes is None else "window_attn_bounded",
    )(q, k, k, k, k, vt, vt, vt, vt, sink_lanes, *stab)


def _gla_direction(d, gq_ref, gk_ref, gv_ref, b_ref, o_ref, s_ref):
    nc = CHUNKS_PER_TILE
    rows = lambda c: slice(c * GLA_CHUNK, (c + 1) * GLA_CHUNK)
    bcum = b_ref[...]
    zero_row = jnp.zeros((1, GLA_K), F32)
    if d == 0:
        earlier = lambda c: range(0, c)
        b_start = lambda c: zero_row if c == 0 else bcum[c * GLA_CHUNK - 1:c * GLA_CHUNK]
        b_end = lambda c: bcum[(c + 1) * GLA_CHUNK - 1:(c + 1) * GLA_CHUNK]
        last_chunk = nc - 1
    else:
        earlier = lambda c: range(c + 1, nc)
        b_start = lambda c: zero_row if c == nc - 1 else bcum[(c + 1) * GLA_CHUNK:(c + 1) * GLA_CHUNK + 1]
        b_end = lambda c: bcum[c * GLA_CHUNK:c * GLA_CHUNK + 1]
        last_chunk = 0
    total = b_end(last_chunk)
    gq = gq_ref[...].astype(F32) * (GLA_DK ** -0.5)
    gk = gk_ref[...].astype(F32)
    q_loc, q_tile, k_in, k_out, k_raw = [], [], [], [], []
    for c in range(nc):
        b_loc = bcum[rows(c)] - b_start(c)
        ql = gq[rows(c)] * jnp.exp2(b_loc)
        q_loc.append(ql.astype(BF16))
        q_tile.append((ql * jnp.exp2(b_start(c))).astype(BF16))
        k_in.append((gk[rows(c)] * jnp.exp2(-b_loc)).astype(BF16))
        k_out.append(gk[rows(c)] * jnp.exp2(b_end(c) - bcum[rows(c)]))
        k_raw.append(gk_ref[rows(c), :])
    k_out_bf = [k.astype(BF16) for k in k_out]

    def key_seen_from(c, src):
        if src == c:
            return k_in[c]
        if src not in earlier(c):
            return k_raw[src]
        if abs(src - c) == 1:
            return k_out_bf[src]
        return (k_out[src] * jnp.exp2(b_start(c) - b_end(src))).astype(BF16)

    k_dst = [jnp.concatenate([key_seen_from(c, src) for src in range(nc)], axis=0) for c in range(nc)]
    k_state = jnp.concatenate(
        [k_out_bf[c] if c == last_chunk else (k_out[c] * jnp.exp2(total - b_end(c))).astype(BF16)
         for c in range(nc)], axis=0)
    q_tile = jnp.concatenate(q_tile, axis=0)
    ti = lax.broadcasted_iota(jnp.int32, (TM, TM), 0)
    tj = lax.broadcasted_iota(jnp.int32, (TM, TM), 1)
    causal = (tj <= ti) if d == 0 else (tj >= ti)
    decay = jnp.exp2(total)

    for hh in range(GLA_HEADS):
        kc = slice(hh * GLA_DK, (hh + 1) * GLA_DK)
        vc = slice(hh * GLA_DV, (hh + 1) * GLA_DV)
        a = jnp.concatenate([_dot_nt(q_loc[c][:, kc], k_dst[c][:, kc]) for c in range(nc)], axis=0)
        a = jnp.where(causal, a, 0.0).astype(BF16)
        v = gv_ref[:, vc]
        s_old = s_ref[hh]
        o_ref[:, vc] = (_dot(a, v) + _dot(q_tile[:, kc], s_old.astype(BF16))).astype(BF16)
        decay_col = jnp.broadcast_to(decay[:, kc], (GLA_DK, GLA_DK)).T
        decay_col = jnp.concatenate([decay_col] * (GLA_DV // GLA_DK), axis=1)
        s_ref[hh] = s_old * decay_col + _dot_tn(k_state[:, kc], v)


def _backward_scan_tile(pos, n_tiles):
    return jnp.where(pos == 0, 0, n_tiles - pos)


def _gla_kernel(*refs):
    gqf, gkf, gvf, bf = refs[0:4]
    bwd_in = refs[4:4 + 4 * SUB_TILES]
    of_ref, ob_ref, sf_ref, sb_ref = refs[4 + 4 * SUB_TILES:]

    @pl.when(pl.program_id(0) == 0)
    def _():
        sf_ref[...] = jnp.zeros_like(sf_ref)
        sb_ref[...] = jnp.zeros_like(sb_ref)

    for t in range(SUB_TILES):
        rows = slice(t * TM, (t + 1) * TM)
        _gla_direction(0, gqf.at[rows, :], gkf.at[rows, :], gvf.at[rows, :], bf.at[rows, :],
                       of_ref.at[rows, :], sf_ref)
        gqb, gkb, gvb, bb = bwd_in[4 * t:4 * t + 4]
        _gla_direction(1, gqb, gkb, gvb, bb, ob_ref.at[rows, :], sb_ref)


def _gla_scan(gq, gk, gv, b_fwd, b_bwd):
    t_tot = gq.shape[0]
    nt = t_tot // TM
    bwd_spec = lambda w, t: pl.BlockSpec(
        (TM, w), lambda s: (_backward_scan_tile(SUB_TILES * s + t, nt), 0))
    ins = [_step_spec(GLA_K), _step_spec(GLA_K), _step_spec(GLA_V), _step_spec(GLA_K)]
    args = [gq, gk, gv, b_fwd]
    for t in range(SUB_TILES):
        ins += [bwd_spec(GLA_K, t), bwd_spec(GLA_K, t), bwd_spec(GLA_V, t), bwd_spec(GLA_K, t)]
        args += [gq, gk, gv, b_bwd]
    return pl.pallas_call(
        _gla_kernel,
        grid=(t_tot // (SUB_TILES * TM),),
        in_specs=ins,
        out_specs=[_step_spec(GLA_V), _step_spec(GLA_V)],
        out_shape=[jax.ShapeDtypeStruct((t_tot, GLA_V), BF16)] * 2,
        scratch_shapes=[pltpu.VMEM((GLA_HEADS, GLA_DK, GLA_DV), F32)] * 2,
        compiler_params=_cparams(1),
        name="gla_scan",
    )(*args)


def _merge_kernel(*refs, n_x):
    x_refs = refs[:n_x]
    mod_ref, g1_ref, wg_ref, attn_ref, of_ref = refs[n_x:n_x + 5]
    ob_refs = refs[n_x + 5:n_x + 5 + SUB_TILES]
    gain_ref, wa_ref, wl_ref, wo_ref, o_ref = refs[n_x + 5 + SUB_TILES:]
    gain = gain_ref[...]
    for t in range(SUB_TILES):
        rows = slice(t * TM, (t + 1) * TM)
        mod = _tile_mod(mod_ref, t)
        x = _tile_input(x_refs, t)
        h = _norm_modulate(x, g1_ref[...], mod[:, 0:D_MODEL], mod[:, D_MODEL:2 * D_MODEL]).astype(BF16)
        parts = []
        for hh in range(GLA_HEADS):
            vc = slice(hh * GLA_DV, (hh + 1) * GLA_DV)
            o = of_ref[rows, vc].astype(F32) + ob_refs[t][:, vc].astype(F32)
            ms = jnp.mean(o * o, axis=-1, keepdims=True)
            on = o * lax.rsqrt(ms + EPS) * gain
            gr = _dot(h, wg_ref[:, vc])
            parts.append((on * (gr * jax.nn.sigmoid(gr))).astype(BF16))
        gla = jnp.concatenate(parts, axis=1)
        gate_a = jax.nn.sigmoid(_dot(h, wg_ref[:, GLA_V:GLA_V + D_MODEL]))
        gate_b = jax.nn.sigmoid(_dot(h, wg_ref[:, GLA_V + D_MODEL:GLA_V + 2 * D_MODEL]))
        y = gate_a * _dot(attn_ref[rows, :], wa_ref[...]) + gate_b * _dot(gla, wl_ref[...])
        gt1 = mod[:, 2 * D_MODEL:3 * D_MODEL]
        o_ref[rows, :] = x + gt1 * _dot(y.astype(BF16), wo_ref[...])


def _merge_out(layer, xs, mod, g1, w_g, attn, o_f, o_b, gain, wa, wl, wo):
    t_tot = sum(a.shape[0] for a in xs)
    nt = t_tot // TM
    x_args = _tile_input_args(xs)
    ob_spec = lambda t: pl.BlockSpec((TM, GLA_V), lambda i: (_backward_scan_tile(SUB_TILES * i + t, nt), 0))
    return pl.pallas_call(
        functools.partial(_merge_kernel, n_x=len(x_args)),
        grid=(t_tot // (SUB_TILES * TM),),
        in_specs=_tile_input_specs(xs) + [
            _mod_pair_spec(layer), _layer_spec(g1, layer), _layer_spec(w_g, layer),
            _step_spec(ATT_Q), _step_spec(GLA_V)] + [ob_spec(t) for t in range(SUB_TILES)] + [
            _layer_spec(gain, layer), _layer_spec(wa, layer), _layer_spec(wl, layer), _layer_spec(wo, layer)],
        out_specs=_step_spec(D_MODEL),
        out_shape=jax.ShapeDtypeStruct((t_tot, D_MODEL), F32),
        compiler_params=_cparams(1),
        name="merge_out",
    )(*x_args, mod, g1, w_g, attn, o_f, *([o_b] * SUB_TILES), gain, wa, wl, wo)


FF_CHUNK = 1024


def _mlp_kernel(x_ref, mod_ref, g2_ref, w1_ref, w2_ref, o_ref, *, sub):
    for t in range(sub):
        rows = slice(t * TM, (t + 1) * TM)
        mod = _tile_mod(mod_ref, t)
        x = x_ref[rows, :]
        h = _norm_modulate(x, g2_ref[...], mod[:, 3 * D_MODEL:4 * D_MODEL], mod[:, 4 * D_MODEL:5 * D_MODEL])
        h = h.astype(BF16)
        acc = jnp.zeros((TM, D_MODEL), F32)
        for j in range(D_FF // FF_CHUNK):
            cs = slice(j * FF_CHUNK, (j + 1) * FF_CHUNK)
            u = jnp.maximum(_dot(h, w1_ref[:, cs]), 0.0)
            acc = acc + _dot((u * u).astype(BF16), w2_ref[cs, :])
        o_ref[rows, :] = x + mod[:, 5 * D_MODEL:6 * D_MODEL] * acc


def _mlp(layer, xc, mod, g2, w1, w2, latent_only):
    t_tot = xc.shape[0]
    if latent_only:
        sub = 1
        out_rows = t_tot - TM
        out_spec = pl.BlockSpec((TM, D_MODEL), lambda i: (jnp.maximum(i - 1, 0), 0))
    else:
        sub = SUB_TILES
        out_rows = t_tot
        out_spec = pl.BlockSpec((sub * TM, D_MODEL), lambda i: (i, 0))
    return pl.pallas_call(
        functools.partial(_mlp_kernel, sub=sub),
        grid=(t_tot // (sub * TM),),
        in_specs=[pl.BlockSpec((sub * TM, D_MODEL), lambda i: (i, 0)), _mod_pair_spec(layer), _layer_spec(g2, layer),
                  _layer_spec(w1, layer), _layer_spec(w2, layer)],
        out_specs=out_spec,
        out_shape=jax.ShapeDtypeStruct((out_rows, D_MODEL), F32),
        compiler_params=_cparams(1),
        name="mlp",
    )(xc, mod, g2, w1, w2)


def _rope_tables(n_lat, n_ctx):
    half = ROPE_AXIS_DIM // 2
    lane = np.arange(LANES)
    dd = lane % HEAD_DIM
    inv = (ROPE_BASE ** (-(dd % half).astype(np.float64) * 2.0 / ROPE_AXIS_DIM)).astype(np.float32)
    is_row = dd < ROPE_AXIS_DIM
    sign = np.where((dd % ROPE_AXIS_DIM) < half, -1.0, 1.0)
    rows = n_lat // GRID_W
    ang_r = np.arange(rows, dtype=np.float32).astype(np.float64)[:, None] * inv[None, :]
    ang_c = np.arange(GRID_W, dtype=np.float32).astype(np.float64)[:, None] * inv[None, :]
    m = jnp.asarray(is_row)[None, None, :]
    pick = lambda fr, fc: jnp.where(m, jnp.asarray(fr, F32)[:, None, :], jnp.asarray(fc, F32)[None, :, :])
    cos = pick(np.cos(ang_r), np.cos(ang_c)).reshape(n_lat, LANES)
    sin = pick(np.sin(ang_r) * sign, np.sin(ang_c) * sign).reshape(n_lat, LANES)
    cos = jnp.concatenate([jnp.ones((n_ctx, LANES), F32), cos], axis=0)
    sin = jnp.concatenate([jnp.zeros((n_ctx, LANES), F32), sin], axis=0)
    return cos, sin


def _tile_cumsum_matrices():
    i = np.arange(TM)[:, None]
    j = np.arange(TM)[None, :]
    return jnp.asarray(np.stack([j <= i, j >= i]).astype(np.float32) / GLA_NORMALIZER, BF16)


def kernel(x, c, ctx, c_ctx, w_mod, b_mod, g_norm1, w_in, q_gain, k_gain, sink, w_decay, b_decay,
           gla_gain, w_branch_attn, w_branch_gla, w_out, g_norm2, w_ff1, w_ff2):
    assert x.shape[0] == 1 and ctx.shape[0] == 1
    n_lat, n_ctx = x.shape[1], ctx.shape[1]
    assert n_ctx == TM and (n_ctx + n_lat) % (SUB_TILES * TM) == 0
    depth = w_mod.shape[0]

    cvec = jnp.zeros((MOD_ROWS, D_MODEL), F32).at[0].set(c[0]).at[1].set(c_ctx)
    mod = _mod_vectors(cvec, w_mod, b_mod).reshape(depth * MOD_ROWS, 1, N_MOD * D_MODEL)
    cos, sin = _rope_tables(n_lat, n_ctx)
    tri = _tile_cumsum_matrices()

    w_in_bf = w_in.astype(BF16)
    w_g = jnp.concatenate([w_in_bf[:, :, C_GR:C_GA], w_in_bf[:, :, C_MERGE:]], axis=2)
    wa = w_branch_attn.astype(BF16)
    wl = w_branch_gla.astype(BF16)
    wo = w_out.astype(BF16)
    w1 = w_ff1.astype(BF16)
    w2 = w_ff2.astype(BF16)
    wd = jnp.zeros((depth, 2, LANES, GLA_K), F32)
    wd = wd.at[:, 0, 0:GLA_RANK].set(w_decay[:, 0]).at[:, 1, GLA_RANK:2 * GLA_RANK].set(w_decay[:, 1]).astype(BF16)
    bd = b_decay.reshape(depth, 2, 1, GLA_K)
    g1 = g_norm1.reshape(depth, 1, D_MODEL)
    g2 = g_norm2.reshape(depth, 1, D_MODEL)
    qg = jnp.tile(q_gain, (1, LANES // HEAD_DIM)).reshape(depth, 1, LANES)
    kg = jnp.tile(k_gain, (1, LANES // HEAD_DIM)).reshape(depth, 1, LANES)
    gain = gla_gain.reshape(depth, 1, GLA_DV)
    sink_lanes = jnp.repeat((sink * LOG2E).reshape(depth, N_KV_HEADS, 1, GROUP), ATT_BLOCK, axis=3)
    logit_bound = (HEAD_DIM * Q_SCALE * BOUND_MARGIN) * jnp.max(jnp.abs(q_gain), axis=1) * jnp.max(jnp.abs(k_gain), axis=1)
    stab_lanes = jnp.maximum(sink_lanes, logit_bound.reshape(depth, 1, 1, 1))

    xs = (ctx[0], x[0])
    for l in range(depth):
        q, k, vt, gq, gk, gv, b_fwd, b_bwd = _in_proj(l, xs, mod, g1, w_in_bf, qg, kg, cos, sin, wd, bd, tri)
        attn = lax.cond(
            logit_bound[l] <= MAX_SAFE_BOUND,
            lambda *a: _window_attn(l, *a, stab_lanes, n_ctx),
            lambda *a: _window_attn(l, *a, None, n_ctx),
            q, k, vt, sink_lanes)
        o_f, o_b = _gla_scan(gq, gk, gv, b_fwd, b_bwd)
        xc = _merge_out(l, xs, mod, g1, w_g, attn, o_f, o_b, gain, wa, wl, wo)
        xc = _mlp(l, xc, mod, g2, w1, w2, latent_only=(l == depth - 1))
        xs = (xc,)
    return xc[None]
```

```python
import functools

import numpy as np
import jax
import jax.numpy as jnp
from jax import lax
from jax.experimental import pallas as pl
from jax.experimental.pallas import tpu as pltpu

F32 = jnp.float32
BF16 = jnp.bfloat16

D_MODEL = 1024
N_HEADS = 16
N_KV_HEADS = 4
HEAD_DIM = 64
GROUP = N_HEADS // N_KV_HEADS
ATT_BLOCK = 128
GRID_W = 64
ROPE_BASE = 10000.0
ROPE_AXIS_DIM = HEAD_DIM // 2
GLA_HEADS = 4
GLA_DK = 128
GLA_DV = 256
GLA_RANK = 16
GLA_NORMALIZER = 16.0
GLA_CHUNK = 64
D_FF = 4 * D_MODEL
N_MOD = 6
EPS = 1e-6
NEG = -1e30
LOG2E = 1.4426950408889634

ATT_Q = N_HEADS * HEAD_DIM
ATT_KV = N_KV_HEADS * HEAD_DIM
GLA_K = GLA_HEADS * GLA_DK
GLA_V = GLA_HEADS * GLA_DV

LANES = 128
TM = 256
SUB_TILES = 5
CHUNKS_PER_TILE = TM // GLA_CHUNK
MOD_ROWS = 8
VMEM_LIMIT = 56 * 1024 * 1024


def _cparams(n_axes):
    return pltpu.CompilerParams(dimension_semantics=("arbitrary",) * n_axes,
                                vmem_limit_bytes=VMEM_LIMIT)


def _dot(a, b):
    return jnp.dot(a, b, preferred_element_type=F32)


def _dot_nt(a, b):
    return lax.dot_general(a, b, (((1,), (1,)), ((), ())), preferred_element_type=F32)


def _dot_tn(a, b):
    return lax.dot_general(a, b, (((0,), (0,)), ((), ())), preferred_element_type=F32)


def _split2(x):
    x1 = x.astype(BF16)
    return x1, (x - x1.astype(F32)).astype(BF16)


def _layer_spec(arr, layer):
    rest = arr.shape[1:]
    return pl.BlockSpec((None,) + rest, lambda *_: (layer,) + (0,) * len(rest))


def _mod_pair_spec(layer):
    return pl.BlockSpec((2, 1, N_MOD * D_MODEL), lambda i: (layer * (MOD_ROWS // 2), 0, 0))


def _tile_mod(mod_ref, t):
    if t > 0:
        return mod_ref[0]
    return jnp.where(pl.program_id(0) == 0, mod_ref[1], mod_ref[0])


MOD_TN = 1536


def _mod_kernel(c_ref, w_ref, b_ref, o_ref):
    c = c_ref[...]
    s1, s2 = _split2(c * jax.nn.sigmoid(c))
    w1, w2 = _split2(w_ref[0])
    o_ref[0] = _dot(s1, w1) + _dot(s2, w1) + _dot(s1, w2) + b_ref[0]


def _mod_vectors(cvec, w_mod, b_mod):
    depth = w_mod.shape[0]
    n_out = w_mod.shape[2]
    return pl.pallas_call(
        _mod_kernel,
        grid=(depth, n_out // MOD_TN),
        in_specs=[pl.BlockSpec((MOD_ROWS, D_MODEL), lambda l, j: (0, 0)),
                  pl.BlockSpec((1, D_MODEL, MOD_TN), lambda l, j: (l, 0, j)),
                  pl.BlockSpec((1, 1, MOD_TN), lambda l, j: (l, 0, j))],
        out_specs=pl.BlockSpec((1, MOD_ROWS, MOD_TN), lambda l, j: (l, 0, j)),
        out_shape=jax.ShapeDtypeStruct((depth, MOD_ROWS, n_out), F32),
        compiler_params=_cparams(2),
        name="mod_vectors",
    )(cvec, w_mod, b_mod.reshape(depth, 1, n_out))


def _norm_modulate(x, g, shift, scale):
    ms = jnp.mean(x * x, axis=-1, keepdims=True)
    y = x * lax.rsqrt(ms + EPS) * g
    return y * (1.0 + scale) + shift


C_Q = 0
C_K = C_Q + ATT_Q
C_V = C_K + ATT_KV
C_GQ = C_V + ATT_KV
C_GK = C_GQ + GLA_K
C_GV = C_GK + GLA_K
C_GR = C_GV + GLA_V
C_GA = C_GR + GLA_V
C_END = C_GA + LANES
C_MERGE = C_GA + 2 * GLA_RANK
Q_SCALE = HEAD_DIM ** -0.5 * LOG2E


def _headnorm_rope(z, gain, cos, sin, lane):
    lo = lane < HEAD_DIM
    sq = z * z
    s_lo = jnp.sum(jnp.where(lo, sq, 0.0), axis=-1, keepdims=True)
    s_hi = jnp.sum(jnp.where(lo, 0.0, sq), axis=-1, keepdims=True)
    ms = jnp.where(lo, s_lo, s_hi) * (1.0 / HEAD_DIM)
    y = z * lax.rsqrt(ms + EPS) * gain
    half = ROPE_AXIS_DIM // 2
    nxt = pltpu.roll(y, LANES - half, 1)
    prv = pltpu.roll(y, half, 1)
    partner = jnp.where((lane % ROPE_AXIS_DIM) < half, nxt, prv)
    return y * cos + partner * sin


def _tile_input(x_refs, t):
    if len(x_refs) == 1:
        return x_refs[0][t * TM:(t + 1) * TM, :]
    if t > 0:
        return x_refs[1 + t][...]
    return jnp.where(pl.program_id(0) == 0, x_refs[0][...], x_refs[1][...])


def _tile_input_specs(xs):
    if len(xs) == 1:
        return [_step_spec(D_MODEL)]
    lat = lambda t: pl.BlockSpec((TM, D_MODEL), lambda i: (jnp.maximum(SUB_TILES * i + t - 1, 0), 0))
    return [pl.BlockSpec((TM, D_MODEL), lambda i: (0, 0))] + [lat(t) for t in range(SUB_TILES)]


def _tile_input_args(xs):
    return xs if len(xs) == 1 else (xs[0],) + (xs[1],) * SUB_TILES


def _step_spec(width):
    return pl.BlockSpec((SUB_TILES * TM, width), lambda i: (i, 0))


def _in_proj_kernel(*refs, n_x):
    x_refs = refs[:n_x]
    (mod_ref, g1_ref, w_ref, qg_ref, kg_ref, cos_ref, sin_ref, wd_ref, bd_ref, tri_ref,
     q_ref, k_ref, vt_ref, gq_ref, gk_ref, gv_ref, bf_ref, bb_ref) = refs[n_x:]
    for t in range(SUB_TILES):
        rows = slice(t * TM, (t + 1) * TM)
        _in_proj_tile(_tile_input(x_refs, t), _tile_mod(mod_ref, t), g1_ref, w_ref, qg_ref, kg_ref,
                      cos_ref[rows, :], sin_ref[rows, :], wd_ref, bd_ref, tri_ref,
                      q_ref.at[rows, :], k_ref.at[rows, :], vt_ref.at[:, rows], gq_ref.at[rows, :],
                      gk_ref.at[rows, :], gv_ref.at[rows, :], bf_ref.at[rows, :], bb_ref.at[rows, :])


def _in_proj_tile(x, mod, g1_ref, w_ref, qg_ref, kg_ref, cos, sin, wd_ref, bd_ref, tri_ref,
                  q_ref, k_ref, vt_ref, gq_ref, gk_ref, gv_ref, bf_ref, bb_ref):
    h = _norm_modulate(x, g1_ref[...], mod[:, 0:D_MODEL], mod[:, D_MODEL:2 * D_MODEL])
    h = h.astype(BF16)
    lane = lax.broadcasted_iota(jnp.int32, (TM, LANES), 1)
    lo = lane < HEAD_DIM
    qg = qg_ref[...]
    kg = kg_ref[...]
    ga = _dot(h, w_ref[:, C_GA:C_GA + LANES]).astype(BF16)
    decay_pre = [_dot(ga, wd_ref[d]) + bd_ref[d] for d in range(2)]
    for j in range(N_KV_HEADS // 2):
        w = 2 * GROUP * HEAD_DIM
        z = _dot(h, w_ref[:, C_Q + j * w:C_Q + (j + 1) * w])
        r = [_headnorm_rope(z[:, i * LANES:(i + 1) * LANES], qg, cos, sin, lane) * Q_SCALE
             for i in range(w // LANES)]
        for hh in range(GROUP):
            a = r[hh // 2]
            b = r[GROUP // 2 + hh // 2]
            if hh % 2 == 0:
                out = jnp.where(lo, a, pltpu.roll(b, HEAD_DIM, 1))
            else:
                out = jnp.where(lo, pltpu.roll(a, HEAD_DIM, 1), b)
            c0 = (j * GROUP + hh) * LANES
            q_ref[:, c0:c0 + LANES] = out.astype(BF16)
    for d, b_ref in enumerate((bf_ref, bb_ref)):
        x = decay_pre[d]
        u = jnp.exp2(jnp.abs(x) * (-LOG2E))
        g1, g2 = _split2(jnp.minimum(x, 0.0) * LOG2E - jnp.log2(1.0 + u))
        tri = tri_ref[d]
        b_ref[...] = _dot(tri, g1) + _dot(tri, g2)
    z = _dot(h, w_ref[:, C_K:C_K + ATT_KV])
    for e in range(2):
        r = _headnorm_rope(z[:, e * LANES:(e + 1) * LANES], kg, cos, sin, lane)
        k_ref[:, e * LANES:(e + 1) * LANES] = r.astype(BF16)
    z = _dot(h, w_ref[:, C_V:C_V + ATT_KV])
    vt_ref[...] = z.T.astype(BF16)
    gq_ref[...] = _dot(h, w_ref[:, C_GQ:C_GQ + GLA_K]).astype(BF16)
    gk_ref[...] = _dot(h, w_ref[:, C_GK:C_GK + GLA_K]).astype(BF16)
    for j in range(GLA_V // 512):
        gv_ref[:, j * 512:(j + 1) * 512] = _dot(h, w_ref[:, C_GV + j * 512:C_GV + (j + 1) * 512]).astype(BF16)


def _in_proj(layer, xs, mod, g1, w_in_bf, qg, kg, cos, sin, wd, bd, tri):
    t_tot = sum(a.shape[0] for a in xs)
    depth, d_model, _ = w_in_bf.shape
    x_args = _tile_input_args(xs)
    return pl.pallas_call(
        functools.partial(_in_proj_kernel, n_x=len(x_args)),
        grid=(t_tot // (SUB_TILES * TM),),
        in_specs=_tile_input_specs(xs) + [
            _mod_pair_spec(layer), _layer_spec(g1, layer),
            pl.BlockSpec((None, d_model, C_END), lambda i: (layer, 0, 0)),
            _layer_spec(qg, layer), _layer_spec(kg, layer), _step_spec(LANES), _step_spec(LANES),
            _layer_spec(wd, layer), _layer_spec(bd, layer), pl.BlockSpec(tri.shape, lambda i: (0, 0, 0))],
        out_specs=[_step_spec(ATT_Q), _step_spec(ATT_KV), pl.BlockSpec((ATT_KV, SUB_TILES * TM), lambda i: (0, i)),
                   _step_spec(GLA_K), _step_spec(GLA_K), _step_spec(GLA_V), _step_spec(GLA_K), _step_spec(GLA_K)],
        out_shape=[jax.ShapeDtypeStruct((t_tot, ATT_Q), BF16),
                   jax.ShapeDtypeStruct((t_tot, ATT_KV), BF16),
                   jax.ShapeDtypeStruct((ATT_KV, t_tot), BF16),
                   jax.ShapeDtypeStruct((t_tot, GLA_K), BF16),
                   jax.ShapeDtypeStruct((t_tot, GLA_K), BF16),
                   jax.ShapeDtypeStruct((t_tot, GLA_V), BF16),
                   jax.ShapeDtypeStruct((t_tot, GLA_K), F32),
                   jax.ShapeDtypeStruct((t_tot, GLA_K), F32)],
        compiler_params=_cparams(1),
        name="in_proj",
    )(*x_args, mod, g1, w_in_bf, qg, kg, cos, sin, wd, bd, tri)


ONES_ROWS = 16
BOUND_MARGIN = 1.01
MAX_SAFE_BOUND = 60.0


ATT_STEP = SUB_TILES * TM


def _attn_blocks(q_ref, sink_ref, stab_ref, o_ref, key_lists):
    lane_q = lax.broadcasted_iota(jnp.int32, (ATT_BLOCK, LANES), 1)

    def scores(blk, g):
        j, e = divmod(g, 2)
        cs = slice(j * LANES, (j + 1) * LANES)
        qrows = slice(blk * ATT_BLOCK, (blk + 1) * ATT_BLOCK)
        half = (lane_q < HEAD_DIM) if e == 0 else (lane_q >= HEAD_DIM)
        qs = []
        for hh in range(GROUP):
            qc = q_ref[qrows, (j * GROUP + hh) * LANES:(j * GROUP + hh + 1) * LANES]
            qs.append(jnp.where(half, qc, jnp.zeros_like(qc)))
        qg = jnp.concatenate(qs, axis=0)
        keys = key_lists[blk]
        k_all = jnp.concatenate([kr[r0:r0 + ATT_BLOCK, cs] for kr, _, r0, _ in keys], axis=0)
        s = _dot_nt(k_all, qg)
        parts = []
        for i, (_, _, _, mk) in enumerate(keys):
            sp = s[i * ATT_BLOCK:(i + 1) * ATT_BLOCK]
            parts.append(sp if mk is None else jnp.where(mk, sp, NEG))
        return parts

    def finish(blk, g, parts):
        keys = key_lists[blk]
        sk = sink_ref[g]
        if stab_ref is None:
            tile_max = parts[0]
            for sp in parts[1:]:
                tile_max = jnp.maximum(tile_max, sp)
            m = jnp.maximum(sk, jnp.max(tile_max, axis=0, keepdims=True))
        else:
            m = stab_ref[g]
        p = jnp.concatenate([jnp.exp2(sp - m) for sp in parts], axis=0).astype(BF16)
        rs = slice(g * HEAD_DIM, (g + 1) * HEAD_DIM)
        vt_all = jnp.concatenate([vr[rs, r0:r0 + ATT_BLOCK] for _, vr, r0, _ in keys], axis=1)
        ones = jnp.ones((ONES_ROWS, len(keys) * ATT_BLOCK), BF16)
        ot = _dot(jnp.concatenate([vt_all, ones], axis=0), p)
        denom = ot[HEAD_DIM:HEAD_DIM + 1] + jnp.exp2(sk - m)
        ot = ot[0:HEAD_DIM] * (1.0 / denom)
        qrows = slice(blk * ATT_BLOCK, (blk + 1) * ATT_BLOCK)
        for pair in range(GROUP // 2):
            x2 = jnp.concatenate([ot[:, (2 * pair) * ATT_BLOCK:(2 * pair + 1) * ATT_BLOCK],
                                  ot[:, (2 * pair + 1) * ATT_BLOCK:(2 * pair + 2) * ATT_BLOCK]], axis=0)
            c0 = g * GROUP * HEAD_DIM + pair * LANES
            o_ref[qrows, c0:c0 + LANES] = x2.T.astype(BF16)

    units = [(blk, g) for blk in range(len(key_lists)) for g in range(N_KV_HEADS)]
    pending = scores(*units[0])
    for i, unit in enumerate(units):
        upcoming = scores(*units[i + 1]) if i + 1 < len(units) else None
        finish(*unit, pending)
        pending = upcoming


def _attn_kernel(bounded_ref, q_ref, kp_ref, kc_ref, kn_ref, kx_ref, vp_ref, vc_ref, vn_ref, vx_ref, sink_ref,
                 stab_ref, o_ref, *, layer, n_steps, ctx_blocks):
    s = pl.program_id(0)
    n_ctx = kx_ref.shape[0]
    ctx_keys = [(kx_ref, vx_ref, r0, None) for r0 in range(0, n_ctx, ATT_BLOCK)]
    gw = GROUP * ATT_BLOCK
    key_i = lax.broadcasted_iota(jnp.int32, (ATT_BLOCK, gw), 0)
    qry_i = lax.broadcasted_iota(jnp.int32, (ATT_BLOCK, gw), 1) % ATT_BLOCK
    before = key_i >= qry_i
    after = key_i <= qry_i
    everything = key_i >= 0
    past_first = s > 0
    before_last = s < n_steps - 1
    blocks_per_step = ATT_STEP // ATT_BLOCK
    key_lists = []
    for b in range(blocks_per_step):
        prev = (kc_ref, vc_ref, (b - 1) * ATT_BLOCK) if b > 0 else (kp_ref, vp_ref, 0)
        cur = (kc_ref, vc_ref, b * ATT_BLOCK)
        nxt = (kc_ref, vc_ref, (b + 1) * ATT_BLOCK) if b + 1 < blocks_per_step else (kn_ref, vn_ref, 0)
        m_prev = jnp.logical_and(before, past_first) if b <= ctx_blocks else before
        m_cur = jnp.logical_and(everything, past_first) if b < ctx_blocks else None
        m_next = jnp.logical_and(after, past_first) if b < ctx_blocks else after
        if b + 1 == blocks_per_step:
            m_next = jnp.logical_and(after, before_last)
        key_lists.append([prev + (m_prev,), cur + (m_cur,), nxt + (m_next,)] + ctx_keys)
    bounded = bounded_ref[layer] != 0

    @pl.when(bounded)
    def _():
        _attn_blocks(q_ref, sink_ref, stab_ref, o_ref, key_lists)

    @pl.when(jnp.logical_not(bounded))
    def _():
        _attn_blocks(q_ref, sink_ref, None, o_ref, key_lists)


def _window_attn(layer, bounded, q, k, vt, sink_lanes, stab_lanes, n_ctx):
    t_tot = q.shape[0]
    assert t_tot % ATT_STEP == 0
    ns = t_tot // ATT_STEP
    nb = t_tot // ATT_BLOCK
    cb = n_ctx // ATT_BLOCK
    assert cb + 1 < ATT_STEP // ATT_BLOCK
    per = ATT_STEP // ATT_BLOCK
    prev_i = lambda s: jnp.clip(per * s - 1, 0, nb - 1)
    next_i = lambda s: jnp.clip(per * s + per, 0, nb - 1)
    ident = lambda s: s
    kspec = lambda rows, f: pl.BlockSpec((rows, ATT_KV), lambda s: (f(s), 0))
    vspec = lambda rows, f: pl.BlockSpec((ATT_KV, rows), lambda s: (0, f(s)))
    return pl.pallas_call(
        functools.partial(_attn_kernel, layer=layer, n_steps=ns, ctx_blocks=cb),
        grid=(ns,),
        in_specs=[pl.BlockSpec(memory_space=pltpu.SMEM),
                  pl.BlockSpec((ATT_STEP, ATT_Q), lambda s: (s, 0)),
                  kspec(ATT_BLOCK, prev_i), kspec(ATT_STEP, ident), kspec(ATT_BLOCK, next_i),
                  pl.BlockSpec((n_ctx, ATT_KV), lambda s: (0, 0)),
                  vspec(ATT_BLOCK, prev_i), vspec(ATT_STEP, ident), vspec(ATT_BLOCK, next_i),
                  pl.BlockSpec((ATT_KV, n_ctx), lambda s: (0, 0)),
                  _layer_spec(sink_lanes, layer), _layer_spec(stab_lanes, layer)],
        out_specs=pl.BlockSpec((ATT_STEP, ATT_Q), lambda s: (s, 0)),
        out_shape=jax.ShapeDtypeStruct((t_tot, ATT_Q), BF16),
        compiler_params=_cparams(1),
        name="window_attn",
    )(bounded, q, k, k, k, k, vt, vt, vt, vt, sink_lanes, stab_lanes)


def _gla_direction(d, gq_ref, gk_ref, gv_ref, b_ref, o_ref, s_ref):
    nc = CHUNKS_PER_TILE
    rows = lambda c: slice(c * GLA_CHUNK, (c + 1) * GLA_CHUNK)
    bcum = b_ref[...]
    zero_row = jnp.zeros((1, GLA_K), F32)
    if d == 0:
        earlier = lambda c: range(0, c)
        b_start = lambda c: zero_row if c == 0 else bcum[c * GLA_CHUNK - 1:c * GLA_CHUNK]
        b_end = lambda c: bcum[(c + 1) * GLA_CHUNK - 1:(c + 1) * GLA_CHUNK]
        last_chunk = nc - 1
    else:
        earlier = lambda c: range(c + 1, nc)
        b_start = lambda c: zero_row if c == nc - 1 else bcum[(c + 1) * GLA_CHUNK:(c + 1) * GLA_CHUNK + 1]
        b_end = lambda c: bcum[c * GLA_CHUNK:c * GLA_CHUNK + 1]
        last_chunk = 0
    total = b_end(last_chunk)
    gq = gq_ref[...].astype(F32) * (GLA_DK ** -0.5)
    gk = gk_ref[...].astype(F32)
    q_loc, q_tile, k_in, k_out, k_raw = [], [], [], [], []
    for c in range(nc):
        b_loc = bcum[rows(c)] - b_start(c)
        ql = gq[rows(c)] * jnp.exp2(b_loc)
        q_loc.append(ql.astype(BF16))
        q_tile.append((ql * jnp.exp2(b_start(c))).astype(BF16))
        k_in.append((gk[rows(c)] * jnp.exp2(-b_loc)).astype(BF16))
        k_out.append(gk[rows(c)] * jnp.exp2(b_end(c) - bcum[rows(c)]))
        k_raw.append(gk_ref[rows(c), :])
    k_out_bf = [k.astype(BF16) for k in k_out]

    def key_seen_from(c, src):
        if src == c:
            return k_in[c]
        if src not in earlier(c):
            return k_raw[src]
        if abs(src - c) == 1:
            return k_out_bf[src]
        return (k_out[src] * jnp.exp2(b_start(c) - b_end(src))).astype(BF16)

    k_dst = [jnp.concatenate([key_seen_from(c, src) for src in range(nc)], axis=0) for c in range(nc)]
    k_state = jnp.concatenate(
        [k_out_bf[c] if c == last_chunk else (k_out[c] * jnp.exp2(total - b_end(c))).astype(BF16)
         for c in range(nc)], axis=0)
    q_tile = jnp.concatenate(q_tile, axis=0)
    ti = lax.broadcasted_iota(jnp.int32, (TM, TM), 0)
    tj = lax.broadcasted_iota(jnp.int32, (TM, TM), 1)
    causal = (tj <= ti) if d == 0 else (tj >= ti)
    decay = jnp.exp2(total)

    for hh in range(GLA_HEADS):
        kc = slice(hh * GLA_DK, (hh + 1) * GLA_DK)
        vc = slice(hh * GLA_DV, (hh + 1) * GLA_DV)
        a = jnp.concatenate([_dot_nt(q_loc[c][:, kc], k_dst[c][:, kc]) for c in range(nc)], axis=0)
        a = jnp.where(causal, a, 0.0).astype(BF16)
        v = gv_ref[:, vc]
        s_old = s_ref[hh]
        o_ref[:, vc] = (_dot(a, v) + _dot(q_tile[:, kc], s_old.astype(BF16))).astype(BF16)
        decay_col = jnp.broadcast_to(decay[:, kc], (GLA_DK, GLA_DK)).T
        decay_col = jnp.concatenate([decay_col] * (GLA_DV // GLA_DK), axis=1)
        s_ref[hh] = s_old * decay_col + _dot_tn(k_state[:, kc], v)


def _backward_scan_tile(pos, n_tiles):
    return jnp.where(pos == 0, 0, n_tiles - pos)


def _gla_kernel(*refs):
    gqf, gkf, gvf, bf = refs[0:4]
    bwd_in = refs[4:4 + 4 * SUB_TILES]
    of_ref, ob_ref, sf_ref, sb_ref = refs[4 + 4 * SUB_TILES:]

    @pl.when(pl.program_id(0) == 0)
    def _():
        sf_ref[...] = jnp.zeros_like(sf_ref)
        sb_ref[...] = jnp.zeros_like(sb_ref)

    for t in range(SUB_TILES):
        rows = slice(t * TM, (t + 1) * TM)
        _gla_direction(0, gqf.at[rows, :], gkf.at[rows, :], gvf.at[rows, :], bf.at[rows, :],
                       of_ref.at[rows, :], sf_ref)
        gqb, gkb, gvb, bb = bwd_in[4 * t:4 * t + 4]
        _gla_direction(1, gqb, gkb, gvb, bb, ob_ref.at[rows, :], sb_ref)


def _gla_scan(gq, gk, gv, b_fwd, b_bwd):
    t_tot = gq.shape[0]
    nt = t_tot // TM
    bwd_spec = lambda w, t: pl.BlockSpec(
        (TM, w), lambda s: (_backward_scan_tile(SUB_TILES * s + t, nt), 0))
    ins = [_step_spec(GLA_K), _step_spec(GLA_K), _step_spec(GLA_V), _step_spec(GLA_K)]
    args = [gq, gk, gv, b_fwd]
    for t in range(SUB_TILES):
        ins += [bwd_spec(GLA_K, t), bwd_spec(GLA_K, t), bwd_spec(GLA_V, t), bwd_spec(GLA_K, t)]
        args += [gq, gk, gv, b_bwd]
    return pl.pallas_call(
        _gla_kernel,
        grid=(t_tot // (SUB_TILES * TM),),
        in_specs=ins,
        out_specs=[_step_spec(GLA_V), _step_spec(GLA_V)],
        out_shape=[jax.ShapeDtypeStruct((t_tot, GLA_V), BF16)] * 2,
        scratch_shapes=[pltpu.VMEM((GLA_HEADS, GLA_DK, GLA_DV), F32)] * 2,
        compiler_params=_cparams(1),
        name="gla_scan",
    )(*args)


def _merge_kernel(*refs, n_x):
    x_refs = refs[:n_x]
    mod_ref, g1_ref, wg_ref, attn_ref, of_ref = refs[n_x:n_x + 5]
    ob_refs = refs[n_x + 5:n_x + 5 + SUB_TILES]
    gain_ref, wa_ref, wl_ref, wo_ref, o_ref = refs[n_x + 5 + SUB_TILES:]
    gain = gain_ref[...]
    for t in range(SUB_TILES):
        rows = slice(t * TM, (t + 1) * TM)
        mod = _tile_mod(mod_ref, t)
        x = _tile_input(x_refs, t)
        h = _norm_modulate(x, g1_ref[...], mod[:, 0:D_MODEL], mod[:, D_MODEL:2 * D_MODEL]).astype(BF16)
        parts = []
        for hh in range(GLA_HEADS):
            vc = slice(hh * GLA_DV, (hh + 1) * GLA_DV)
            o = of_ref[rows, vc].astype(F32) + ob_refs[t][:, vc].astype(F32)
            ms = jnp.mean(o * o, axis=-1, keepdims=True)
            on = o * lax.rsqrt(ms + EPS) * gain
            gr = _dot(h, wg_ref[:, vc])
            parts.append((on * (gr * jax.nn.sigmoid(gr))).astype(BF16))
        gla = jnp.concatenate(parts, axis=1)
        gate_a = jax.nn.sigmoid(_dot(h, wg_ref[:, GLA_V:GLA_V + D_MODEL]))
        gate_b = jax.nn.sigmoid(_dot(h, wg_ref[:, GLA_V + D_MODEL:GLA_V + 2 * D_MODEL]))
        y = gate_a * _dot(attn_ref[rows, :], wa_ref[...]) + gate_b * _dot(gla, wl_ref[...])
        gt1 = mod[:, 2 * D_MODEL:3 * D_MODEL]
        o_ref[rows, :] = x + gt1 * _dot(y.astype(BF16), wo_ref[...])


def _merge_out(layer, xs, mod, g1, w_g, attn, o_f, o_b, gain, wa, wl, wo):
    t_tot = sum(a.shape[0] for a in xs)
    nt = t_tot // TM
    x_args = _tile_input_args(xs)
    ob_spec = lambda t: pl.BlockSpec((TM, GLA_V), lambda i: (_backward_scan_tile(SUB_TILES * i + t, nt), 0))
    return pl.pallas_call(
        functools.partial(_merge_kernel, n_x=len(x_args)),
        grid=(t_tot // (SUB_TILES * TM),),
        in_specs=_tile_input_specs(xs) + [
            _mod_pair_spec(layer), _layer_spec(g1, layer), _layer_spec(w_g, layer),
            _step_spec(ATT_Q), _step_spec(GLA_V)] + [ob_spec(t) for t in range(SUB_TILES)] + [
            _layer_spec(gain, layer), _layer_spec(wa, layer), _layer_spec(wl, layer), _layer_spec(wo, layer)],
        out_specs=_step_spec(D_MODEL),
        out_shape=jax.ShapeDtypeStruct((t_tot, D_MODEL), F32),
        compiler_params=_cparams(1),
        name="merge_out",
    )(*x_args, mod, g1, w_g, attn, o_f, *([o_b] * SUB_TILES), gain, wa, wl, wo)


FF_CHUNK = 1024


def _mlp_kernel(x_ref, mod_ref, g2_ref, w1_ref, w2_ref, o_ref, *, sub):
    for t in range(sub):
        rows = slice(t * TM, (t + 1) * TM)
        mod = _tile_mod(mod_ref, t)
        x = x_ref[rows, :]
        h = _norm_modulate(x, g2_ref[...], mod[:, 3 * D_MODEL:4 * D_MODEL], mod[:, 4 * D_MODEL:5 * D_MODEL])
        h = h.astype(BF16)
        acc = jnp.zeros((TM, D_MODEL), F32)
        for j in range(D_FF // FF_CHUNK):
            cs = slice(j * FF_CHUNK, (j + 1) * FF_CHUNK)
            u = jnp.maximum(_dot(h, w1_ref[:, cs]), 0.0)
            acc = acc + _dot((u * u).astype(BF16), w2_ref[cs, :])
        o_ref[rows, :] = x + mod[:, 5 * D_MODEL:6 * D_MODEL] * acc


def _mlp(layer, xc, mod, g2, w1, w2, latent_only):
    t_tot = xc.shape[0]
    if latent_only:
        sub = 1
        out_rows = t_tot - TM
        out_spec = pl.BlockSpec((TM, D_MODEL), lambda i: (jnp.maximum(i - 1, 0), 0))
    else:
        sub = SUB_TILES
        out_rows = t_tot
        out_spec = pl.BlockSpec((sub * TM, D_MODEL), lambda i: (i, 0))
    return pl.pallas_call(
        functools.partial(_mlp_kernel, sub=sub),
        grid=(t_tot // (sub * TM),),
        in_specs=[pl.BlockSpec((sub * TM, D_MODEL), lambda i: (i, 0)), _mod_pair_spec(layer), _layer_spec(g2, layer),
                  _layer_spec(w1, layer), _layer_spec(w2, layer)],
        out_specs=out_spec,
        out_shape=jax.ShapeDtypeStruct((out_rows, D_MODEL), F32),
        compiler_params=_cparams(1),
        name="mlp",
    )(xc, mod, g2, w1, w2)


def _rope_tables(n_lat, n_ctx):
    half = ROPE_AXIS_DIM // 2
    lane = np.arange(LANES)
    dd = lane % HEAD_DIM
    inv = (ROPE_BASE ** (-(dd % half).astype(np.float64) * 2.0 / ROPE_AXIS_DIM)).astype(np.float32)
    is_row = dd < ROPE_AXIS_DIM
    sign = np.where((dd % ROPE_AXIS_DIM) < half, -1.0, 1.0)
    rows = n_lat // GRID_W
    ang_r = np.arange(rows, dtype=np.float32).astype(np.float64)[:, None] * inv[None, :]
    ang_c = np.arange(GRID_W, dtype=np.float32).astype(np.float64)[:, None] * inv[None, :]
    m = jnp.asarray(is_row)[None, None, :]
    pick = lambda fr, fc: jnp.where(m, jnp.asarray(fr, F32)[:, None, :], jnp.asarray(fc, F32)[None, :, :])
    cos = pick(np.cos(ang_r), np.cos(ang_c)).reshape(n_lat, LANES)
    sin = pick(np.sin(ang_r) * sign, np.sin(ang_c) * sign).reshape(n_lat, LANES)
    cos = jnp.concatenate([jnp.ones((n_ctx, LANES), F32), cos], axis=0)
    sin = jnp.concatenate([jnp.zeros((n_ctx, LANES), F32), sin], axis=0)
    return cos, sin


def _tile_cumsum_matrices():
    i = np.arange(TM)[:, None]
    j = np.arange(TM)[None, :]
    return jnp.asarray(np.stack([j <= i, j >= i]).astype(np.float32) / GLA_NORMALIZER, BF16)


def kernel(x, c, ctx, c_ctx, w_mod, b_mod, g_norm1, w_in, q_gain, k_gain, sink, w_decay, b_decay,
           gla_gain, w_branch_attn, w_branch_gla, w_out, g_norm2, w_ff1, w_ff2):
    assert x.shape[0] == 1 and ctx.shape[0] == 1
    n_lat, n_ctx = x.shape[1], ctx.shape[1]
    assert n_ctx == TM and (n_ctx + n_lat) % (SUB_TILES * TM) == 0
    depth = w_mod.shape[0]

    cvec = jnp.zeros((MOD_ROWS, D_MODEL), F32).at[0].set(c[0]).at[1].set(c_ctx)
    mod = _mod_vectors(cvec, w_mod, b_mod).reshape(depth * MOD_ROWS, 1, N_MOD * D_MODEL)
    cos, sin = _rope_tables(n_lat, n_ctx)
    tri = _tile_cumsum_matrices()

    w_in_bf = w_in.astype(BF16)
    w_g = jnp.concatenate([w_in_bf[:, :, C_GR:C_GA], w_in_bf[:, :, C_MERGE:]], axis=2)
    wa = w_branch_attn.astype(BF16)
    wl = w_branch_gla.astype(BF16)
    wo = w_out.astype(BF16)
    w1 = w_ff1.astype(BF16)
    w2 = w_ff2.astype(BF16)
    wd = jnp.zeros((depth, 2, LANES, GLA_K), F32)
    wd = wd.at[:, 0, 0:GLA_RANK].set(w_decay[:, 0]).at[:, 1, GLA_RANK:2 * GLA_RANK].set(w_decay[:, 1]).astype(BF16)
    bd = b_decay.reshape(depth, 2, 1, GLA_K)
    g1 = g_norm1.reshape(depth, 1, D_MODEL)
    g2 = g_norm2.reshape(depth, 1, D_MODEL)
    qg = jnp.tile(q_gain, (1, LANES // HEAD_DIM)).reshape(depth, 1, LANES)
    kg = jnp.tile(k_gain, (1, LANES // HEAD_DIM)).reshape(depth, 1, LANES)
    gain = gla_gain.reshape(depth, 1, GLA_DV)
    sink_lanes = jnp.repeat((sink * LOG2E).reshape(depth, N_KV_HEADS, 1, GROUP), ATT_BLOCK, axis=3)
    logit_bound = (HEAD_DIM * Q_SCALE * BOUND_MARGIN) * jnp.max(jnp.abs(q_gain), axis=1) * jnp.max(jnp.abs(k_gain), axis=1)
    stab_lanes = jnp.maximum(sink_lanes, logit_bound.reshape(depth, 1, 1, 1))
    bounded = (logit_bound <= MAX_SAFE_BOUND).astype(jnp.int32)

    xs = (ctx[0], x[0])
    for l in range(depth):
        q, k, vt, gq, gk, gv, b_fwd, b_bwd = _in_proj(l, xs, mod, g1, w_in_bf, qg, kg, cos, sin, wd, bd, tri)
        attn = _window_attn(l, bounded, q, k, vt, sink_lanes, stab_lanes, n_ctx)
        o_f, o_b = _gla_scan(gq, gk, gv, b_fwd, b_bwd)
        xc = _merge_out(l, xs, mod, g1, w_g, attn, o_f, o_b, gain, wa, wl, wo)
        xc = _mlp(l, xc, mod, g2, w1, w2, latent_only=(l == depth - 1))
        xs = (xc,)
    return xc[None]
```

```python
import functools

import numpy as np
import jax
import jax.numpy as jnp
from jax import lax
from jax.experimental import pallas as pl
from jax.experimental.pallas import tpu as pltpu

F32 = jnp.float32
BF16 = jnp.bfloat16

D_MODEL = 1024
N_HEADS = 16
N_KV_HEADS = 4
HEAD_DIM = 64
GROUP = N_HEADS // N_KV_HEADS
ATT_BLOCK = 128
GRID_W = 64
ROPE_BASE = 10000.0
ROPE_AXIS_DIM = HEAD_DIM // 2
GLA_HEADS = 4
GLA_DK = 128
GLA_DV = 256
GLA_RANK = 16
GLA_NORMALIZER = 16.0
GLA_CHUNK = 64
D_FF = 4 * D_MODEL
N_MOD = 6
EPS = 1e-6
NEG = -1e30
LOG2E = 1.4426950408889634

ATT_Q = N_HEADS * HEAD_DIM
ATT_KV = N_KV_HEADS * HEAD_DIM
GLA_K = GLA_HEADS * GLA_DK
GLA_V = GLA_HEADS * GLA_DV

LANES = 128
TM = 256
SUB_TILES = 5
CHUNKS_PER_TILE = TM // GLA_CHUNK
MOD_ROWS = 8
VMEM_LIMIT = 56 * 1024 * 1024


def _cparams(n_axes):
    return pltpu.CompilerParams(dimension_semantics=("arbitrary",) * n_axes,
                                vmem_limit_bytes=VMEM_LIMIT)


def _dot(a, b):
    return jnp.dot(a, b, preferred_element_type=F32)


def _dot_nt(a, b):
    return lax.dot_general(a, b, (((1,), (1,)), ((), ())), preferred_element_type=F32)


def _dot_tn(a, b):
    return lax.dot_general(a, b, (((0,), (0,)), ((), ())), preferred_element_type=F32)


def _split2(x):
    x1 = x.astype(BF16)
    return x1, (x - x1.astype(F32)).astype(BF16)


def _layer_spec(arr, layer):
    rest = arr.shape[1:]
    return pl.BlockSpec((None,) + rest, lambda *_: (layer,) + (0,) * len(rest))


def _mod_pair_spec(layer):
    return pl.BlockSpec((2, 1, N_MOD * D_MODEL), lambda i: (layer * (MOD_ROWS // 2), 0, 0))


def _tile_mod(mod_ref, t):
    if t > 0:
        return mod_ref[0]
    return jnp.where(pl.program_id(0) == 0, mod_ref[1], mod_ref[0])


MOD_TN = 1536


def _mod_kernel(c_ref, w_ref, b_ref, o_ref):
    c = c_ref[...]
    s1, s2 = _split2(c * jax.nn.sigmoid(c))
    w1, w2 = _split2(w_ref[0])
    o_ref[0] = _dot(s1, w1) + _dot(s2, w1) + _dot(s1, w2) + b_ref[0]


def _mod_vectors(cvec, w_mod, b_mod):
    depth = w_mod.shape[0]
    n_out = w_mod.shape[2]
    return pl.pallas_call(
        _mod_kernel,
        grid=(depth, n_out // MOD_TN),
        in_specs=[pl.BlockSpec((MOD_ROWS, D_MODEL), lambda l, j: (0, 0)),
                  pl.BlockSpec((1, D_MODEL, MOD_TN), lambda l, j: (l, 0, j)),
                  pl.BlockSpec((1, 1, MOD_TN), lambda l, j: (l, 0, j))],
        out_specs=pl.BlockSpec((1, MOD_ROWS, MOD_TN), lambda l, j: (l, 0, j)),
        out_shape=jax.ShapeDtypeStruct((depth, MOD_ROWS, n_out), F32),
        compiler_params=_cparams(2),
        name="mod_vectors",
    )(cvec, w_mod, b_mod.reshape(depth, 1, n_out))


CAST_ROWS = 256


def _cast_kernel(x_ref, o_ref):
    o_ref[...] = x_ref[...].astype(BF16)


def _to_bf16(w):
    depth, rows, cols = w.shape
    spec = pl.BlockSpec((1, CAST_ROWS, cols), lambda l, r: (l, r, 0))
    return pl.pallas_call(
        _cast_kernel,
        grid=(depth, rows // CAST_ROWS),
        in_specs=[spec],
        out_specs=spec,
        out_shape=jax.ShapeDtypeStruct(w.shape, BF16),
        compiler_params=_cparams(2),
        name="to_bf16",
    )(w)


def _norm_modulate(x, g, shift, scale):
    ms = jnp.mean(x * x, axis=-1, keepdims=True)
    y = x * lax.rsqrt(ms + EPS) * g
    return y * (1.0 + scale) + shift


C_Q = 0
C_K = C_Q + ATT_Q
C_V = C_K + ATT_KV
C_GQ = C_V + ATT_KV
C_GK = C_GQ + GLA_K
C_GV = C_GK + GLA_K
C_GR = C_GV + GLA_V
C_GA = C_GR + GLA_V
C_END = C_GA + LANES
C_MERGE = C_GA + 2 * GLA_RANK
Q_SCALE = HEAD_DIM ** -0.5 * LOG2E


def _headnorm_rope(z, gain, cos, sin, lane):
    lo = lane < HEAD_DIM
    sq = z * z
    s_lo = jnp.sum(jnp.where(lo, sq, 0.0), axis=-1, keepdims=True)
    s_hi = jnp.sum(jnp.where(lo, 0.0, sq), axis=-1, keepdims=True)
    ms = jnp.where(lo, s_lo, s_hi) * (1.0 / HEAD_DIM)
    y = z * lax.rsqrt(ms + EPS) * gain
    half = ROPE_AXIS_DIM // 2
    nxt = pltpu.roll(y, LANES - half, 1)
    prv = pltpu.roll(y, half, 1)
    partner = jnp.where((lane % ROPE_AXIS_DIM) < half, nxt, prv)
    return y * cos + partner * sin


def _tile_input(x_refs, t):
    if len(x_refs) == 1:
        return x_refs[0][t * TM:(t + 1) * TM, :]
    if t > 0:
        return x_refs[1 + t][...]
    return jnp.where(pl.program_id(0) == 0, x_refs[0][...], x_refs[1][...])


def _tile_input_specs(xs):
    if len(xs) == 1:
        return [_step_spec(D_MODEL)]
    lat = lambda t: pl.BlockSpec((TM, D_MODEL), lambda i: (jnp.maximum(SUB_TILES * i + t - 1, 0), 0))
    return [pl.BlockSpec((TM, D_MODEL), lambda i: (0, 0))] + [lat(t) for t in range(SUB_TILES)]


def _tile_input_args(xs):
    return xs if len(xs) == 1 else (xs[0],) + (xs[1],) * SUB_TILES


def _step_spec(width):
    return pl.BlockSpec((SUB_TILES * TM, width), lambda i: (i, 0))


def _in_proj_kernel(*refs, n_x):
    x_refs = refs[:n_x]
    (mod_ref, g1_ref, w_ref, qg_ref, kg_ref, cos_ref, sin_ref, wd_ref, bd_ref, tri_ref,
     q_ref, k_ref, vt_ref, gq_ref, gk_ref, gv_ref, bf_ref, bb_ref) = refs[n_x:]
    for t in range(SUB_TILES):
        rows = slice(t * TM, (t + 1) * TM)
        _in_proj_tile(_tile_input(x_refs, t), _tile_mod(mod_ref, t), g1_ref, w_ref, qg_ref, kg_ref,
                      cos_ref[rows, :], sin_ref[rows, :], wd_ref, bd_ref, tri_ref,
                      q_ref.at[rows, :], k_ref.at[rows, :], vt_ref.at[:, rows], gq_ref.at[rows, :],
                      gk_ref.at[rows, :], gv_ref.at[rows, :], bf_ref.at[rows, :], bb_ref.at[rows, :])


def _in_proj_tile(x, mod, g1_ref, w_ref, qg_ref, kg_ref, cos, sin, wd_ref, bd_ref, tri_ref,
                  q_ref, k_ref, vt_ref, gq_ref, gk_ref, gv_ref, bf_ref, bb_ref):
    h = _norm_modulate(x, g1_ref[...], mod[:, 0:D_MODEL], mod[:, D_MODEL:2 * D_MODEL])
    h = h.astype(BF16)
    lane = lax.broadcasted_iota(jnp.int32, (TM, LANES), 1)
    lo = lane < HEAD_DIM
    qg = qg_ref[...]
    kg = kg_ref[...]
    ga = _dot(h, w_ref[:, C_GA:C_GA + LANES]).astype(BF16)
    decay_pre = [_dot(ga, wd_ref[d]) + bd_ref[d] for d in range(2)]
    for j in range(N_KV_HEADS // 2):
        w = 2 * GROUP * HEAD_DIM
        z = _dot(h, w_ref[:, C_Q + j * w:C_Q + (j + 1) * w])
        r = [_headnorm_rope(z[:, i * LANES:(i + 1) * LANES], qg, cos, sin, lane) * Q_SCALE
             for i in range(w // LANES)]
        for hh in range(GROUP):
            a = r[hh // 2]
            b = r[GROUP // 2 + hh // 2]
            if hh % 2 == 0:
                out = jnp.where(lo, a, pltpu.roll(b, HEAD_DIM, 1))
            else:
                out = jnp.where(lo, pltpu.roll(a, HEAD_DIM, 1), b)
            c0 = (j * GROUP + hh) * LANES
            q_ref[:, c0:c0 + LANES] = out.astype(BF16)
    for d, b_ref in enumerate((bf_ref, bb_ref)):
        x = decay_pre[d]
        u = jnp.exp2(jnp.abs(x) * (-LOG2E))
        g = (jnp.minimum(x, 0.0) * LOG2E - jnp.log2(1.0 + u)).astype(BF16)
        b_ref[...] = _dot(tri_ref[d], g)
    z = _dot(h, w_ref[:, C_K:C_K + ATT_KV])
    for e in range(2):
        r = _headnorm_rope(z[:, e * LANES:(e + 1) * LANES], kg, cos, sin, lane)
        k_ref[:, e * LANES:(e + 1) * LANES] = r.astype(BF16)
    z = _dot(h, w_ref[:, C_V:C_V + ATT_KV])
    vt_ref[...] = z.T.astype(BF16)
    gq_ref[...] = _dot(h, w_ref[:, C_GQ:C_GQ + GLA_K]).astype(BF16)
    gk_ref[...] = _dot(h, w_ref[:, C_GK:C_GK + GLA_K]).astype(BF16)
    for j in range(GLA_V // 512):
        gv_ref[:, j * 512:(j + 1) * 512] = _dot(h, w_ref[:, C_GV + j * 512:C_GV + (j + 1) * 512]).astype(BF16)


def _in_proj(layer, xs, mod, g1, w_in_bf, qg, kg, cos, sin, wd, bd, tri):
    t_tot = sum(a.shape[0] for a in xs)
    depth, d_model, _ = w_in_bf.shape
    x_args = _tile_input_args(xs)
    return pl.pallas_call(
        functools.partial(_in_proj_kernel, n_x=len(x_args)),
        grid=(t_tot // (SUB_TILES * TM),),
        in_specs=_tile_input_specs(xs) + [
            _mod_pair_spec(layer), _layer_spec(g1, layer),
            pl.BlockSpec((None, d_model, C_END), lambda i: (layer, 0, 0)),
            _layer_spec(qg, layer), _layer_spec(kg, layer), _step_spec(LANES), _step_spec(LANES),
            _layer_spec(wd, layer), _layer_spec(bd, layer), pl.BlockSpec(tri.shape, lambda i: (0, 0, 0))],
        out_specs=[_step_spec(ATT_Q), _step_spec(ATT_KV), pl.BlockSpec((ATT_KV, SUB_TILES * TM), lambda i: (0, i)),
                   _step_spec(GLA_K), _step_spec(GLA_K), _step_spec(GLA_V), _step_spec(GLA_K), _step_spec(GLA_K)],
        out_shape=[jax.ShapeDtypeStruct((t_tot, ATT_Q), BF16),
                   jax.ShapeDtypeStruct((t_tot, ATT_KV), BF16),
                   jax.ShapeDtypeStruct((ATT_KV, t_tot), BF16),
                   jax.ShapeDtypeStruct((t_tot, GLA_K), BF16),
                   jax.ShapeDtypeStruct((t_tot, GLA_K), BF16),
                   jax.ShapeDtypeStruct((t_tot, GLA_V), BF16),
                   jax.ShapeDtypeStruct((t_tot, GLA_K), F32),
                   jax.ShapeDtypeStruct((t_tot, GLA_K), F32)],
        compiler_params=_cparams(1),
        name="in_proj",
    )(*x_args, mod, g1, w_in_bf, qg, kg, cos, sin, wd, bd, tri)


ONES_ROWS = 16
BOUND_MARGIN = 1.01
MAX_SAFE_BOUND = 60.0


ATT_STEP = SUB_TILES * TM


def _attn_blocks(q_ref, sink_ref, stab_ref, o_ref, key_lists):
    lane_q = lax.broadcasted_iota(jnp.int32, (ATT_BLOCK, LANES), 1)

    def scores(blk, g):
        j, e = divmod(g, 2)
        cs = slice(j * LANES, (j + 1) * LANES)
        qrows = slice(blk * ATT_BLOCK, (blk + 1) * ATT_BLOCK)
        half = (lane_q < HEAD_DIM) if e == 0 else (lane_q >= HEAD_DIM)
        qs = []
        for hh in range(GROUP):
            qc = q_ref[qrows, (j * GROUP + hh) * LANES:(j * GROUP + hh + 1) * LANES]
            qs.append(jnp.where(half, qc, jnp.zeros_like(qc)))
        qg = jnp.concatenate(qs, axis=0)
        keys = key_lists[blk]
        k_all = jnp.concatenate([kr[r0:r0 + ATT_BLOCK, cs] for kr, _, r0, _ in keys], axis=0)
        s = _dot_nt(k_all, qg)
        parts = []
        for i, (_, _, _, mk) in enumerate(keys):
            sp = s[i * ATT_BLOCK:(i + 1) * ATT_BLOCK]
            parts.append(sp if mk is None else jnp.where(mk, sp, NEG))
        return parts

    def finish(blk, g, parts):
        keys = key_lists[blk]
        sk = sink_ref[g]
        if stab_ref is None:
            tile_max = parts[0]
            for sp in parts[1:]:
                tile_max = jnp.maximum(tile_max, sp)
            m = jnp.maximum(sk, jnp.max(tile_max, axis=0, keepdims=True))
        else:
            m = stab_ref[g]
        p = jnp.concatenate([jnp.exp2(sp - m) for sp in parts], axis=0).astype(BF16)
        rs = slice(g * HEAD_DIM, (g + 1) * HEAD_DIM)
        vt_all = jnp.concatenate([vr[rs, r0:r0 + ATT_BLOCK] for _, vr, r0, _ in keys], axis=1)
        ones = jnp.ones((ONES_ROWS, len(keys) * ATT_BLOCK), BF16)
        ot = _dot(jnp.concatenate([vt_all, ones], axis=0), p)
        denom = ot[HEAD_DIM:HEAD_DIM + 1] + jnp.exp2(sk - m)
        ot = ot[0:HEAD_DIM] * (1.0 / denom)
        qrows = slice(blk * ATT_BLOCK, (blk + 1) * ATT_BLOCK)
        for pair in range(GROUP // 2):
            x2 = jnp.concatenate([ot[:, (2 * pair) * ATT_BLOCK:(2 * pair + 1) * ATT_BLOCK],
                                  ot[:, (2 * pair + 1) * ATT_BLOCK:(2 * pair + 2) * ATT_BLOCK]], axis=0)
            c0 = g * GROUP * HEAD_DIM + pair * LANES
            o_ref[qrows, c0:c0 + LANES] = x2.T.astype(BF16)

    units = [(blk, g) for blk in range(len(key_lists)) for g in range(N_KV_HEADS)]
    pending = scores(*units[0])
    for i, unit in enumerate(units):
        upcoming = scores(*units[i + 1]) if i + 1 < len(units) else None
        finish(*unit, pending)
        pending = upcoming


def _attn_kernel(q_ref, kp_ref, kc_ref, kn_ref, kx_ref, vp_ref, vc_ref, vn_ref, vx_ref, sink_ref,
                 *rest, n_steps, ctx_blocks):
    stab_ref, o_ref = rest if len(rest) == 2 else (None, rest[0])
    s = pl.program_id(0)
    n_ctx = kx_ref.shape[0]
    ctx_keys = [(kx_ref, vx_ref, r0, None) for r0 in range(0, n_ctx, ATT_BLOCK)]
    gw = GROUP * ATT_BLOCK
    key_i = lax.broadcasted_iota(jnp.int32, (ATT_BLOCK, gw), 0)
    qry_i = lax.broadcasted_iota(jnp.int32, (ATT_BLOCK, gw), 1) % ATT_BLOCK
    before = key_i >= qry_i
    after = key_i <= qry_i
    everything = key_i >= 0
    past_first = s > 0
    before_last = s < n_steps - 1
    blocks_per_step = ATT_STEP // ATT_BLOCK
    key_lists = []
    for b in range(blocks_per_step):
        prev = (kc_ref, vc_ref, (b - 1) * ATT_BLOCK) if b > 0 else (kp_ref, vp_ref, 0)
        cur = (kc_ref, vc_ref, b * ATT_BLOCK)
        nxt = (kc_ref, vc_ref, (b + 1) * ATT_BLOCK) if b + 1 < blocks_per_step else (kn_ref, vn_ref, 0)
        m_prev = jnp.logical_and(before, past_first) if b <= ctx_blocks else before
        m_cur = jnp.logical_and(everything, past_first) if b < ctx_blocks else None
        m_next = jnp.logical_and(after, past_first) if b < ctx_blocks else after
        if b + 1 == blocks_per_step:
            m_next = jnp.logical_and(after, before_last)
        key_lists.append([prev + (m_prev,), cur + (m_cur,), nxt + (m_next,)] + ctx_keys)
    _attn_blocks(q_ref, sink_ref, stab_ref, o_ref, key_lists)


def _window_attn(layer, q, k, vt, sink_lanes, stab_lanes, n_ctx):
    t_tot = q.shape[0]
    stab = [] if stab_lanes is None else [stab_lanes]
    assert t_tot % ATT_STEP == 0
    ns = t_tot // ATT_STEP
    nb = t_tot // ATT_BLOCK
    cb = n_ctx // ATT_BLOCK
    assert cb + 1 < ATT_STEP // ATT_BLOCK
    per = ATT_STEP // ATT_BLOCK
    prev_i = lambda s: jnp.clip(per * s - 1, 0, nb - 1)
    next_i = lambda s: jnp.clip(per * s + per, 0, nb - 1)
    ident = lambda s: s
    kspec = lambda rows, f: pl.BlockSpec((rows, ATT_KV), lambda s: (f(s), 0))
    vspec = lambda rows, f: pl.BlockSpec((ATT_KV, rows), lambda s: (0, f(s)))
    return pl.pallas_call(
        functools.partial(_attn_kernel, n_steps=ns, ctx_blocks=cb),
        grid=(ns,),
        in_specs=[pl.BlockSpec((ATT_STEP, ATT_Q), lambda s: (s, 0)),
                  kspec(ATT_BLOCK, prev_i), kspec(ATT_STEP, ident), kspec(ATT_BLOCK, next_i),
                  pl.BlockSpec((n_ctx, ATT_KV), lambda s: (0, 0)),
                  vspec(ATT_BLOCK, prev_i), vspec(ATT_STEP, ident), vspec(ATT_BLOCK, next_i),
                  pl.BlockSpec((ATT_KV, n_ctx), lambda s: (0, 0)),
                  _layer_spec(sink_lanes, layer)] + [_layer_spec(a, layer) for a in stab],
        out_specs=pl.BlockSpec((ATT_STEP, ATT_Q), lambda s: (s, 0)),
        out_shape=jax.ShapeDtypeStruct((t_tot, ATT_Q), BF16),
        compiler_params=_cparams(1),
        name="window_attn" if stab_lanes is None else "window_attn_bounded",
    )(q, k, k, k, k, vt, vt, vt, vt, sink_lanes, *stab)


def _gla_direction(d, gq_ref, gk_ref, gv_ref, b_ref, o_ref, s_ref):
    nc = CHUNKS_PER_TILE
    rows = lambda c: slice(c * GLA_CHUNK, (c + 1) * GLA_CHUNK)
    bcum = b_ref[...]
    zero_row = jnp.zeros((1, GLA_K), F32)
    if d == 0:
        earlier = lambda c: range(0, c)
        b_start = lambda c: zero_row if c == 0 else bcum[c * GLA_CHUNK - 1:c * GLA_CHUNK]
        b_end = lambda c: bcum[(c + 1) * GLA_CHUNK - 1:(c + 1) * GLA_CHUNK]
        last_chunk = nc - 1
    else:
        earlier = lambda c: range(c + 1, nc)
        b_start = lambda c: zero_row if c == nc - 1 else bcum[(c + 1) * GLA_CHUNK:(c + 1) * GLA_CHUNK + 1]
        b_end = lambda c: bcum[c * GLA_CHUNK:c * GLA_CHUNK + 1]
        last_chunk = 0
    total = b_end(last_chunk)
    gq = gq_ref[...].astype(F32) * (GLA_DK ** -0.5)
    gk = gk_ref[...].astype(F32)
    q_loc, q_tile, k_in, k_out, k_raw = [], [], [], [], []
    for c in range(nc):
        b_loc = bcum[rows(c)] - b_start(c)
        ql = gq[rows(c)] * jnp.exp2(b_loc)
        q_loc.append(ql.astype(BF16))
        q_tile.append((ql * jnp.exp2(b_start(c))).astype(BF16))
        k_in.append((gk[rows(c)] * jnp.exp2(-b_loc)).astype(BF16))
        k_out.append(gk[rows(c)] * jnp.exp2(b_end(c) - bcum[rows(c)]))
        k_raw.append(gk_ref[rows(c), :])
    k_out_bf = [k.astype(BF16) for k in k_out]

    def key_seen_from(c, src):
        if src == c:
            return k_in[c]
        if src not in earlier(c):
            return k_raw[src]
        if abs(src - c) == 1:
            return k_out_bf[src]
        return (k_out[src] * jnp.exp2(b_start(c) - b_end(src))).astype(BF16)

    k_dst = [jnp.concatenate([key_seen_from(c, src) for src in range(nc)], axis=0) for c in range(nc)]
    k_state = jnp.concatenate(
        [k_out_bf[c] if c == last_chunk else (k_out[c] * jnp.exp2(total - b_end(c))).astype(BF16)
         for c in range(nc)], axis=0)
    q_tile = jnp.concatenate(q_tile, axis=0)
    ti = lax.broadcasted_iota(jnp.int32, (TM, TM), 0)
    tj = lax.broadcasted_iota(jnp.int32, (TM, TM), 1)
    causal = (tj <= ti) if d == 0 else (tj >= ti)
    decay = jnp.exp2(total)

    for hh in range(GLA_HEADS):
        kc = slice(hh * GLA_DK, (hh + 1) * GLA_DK)
        vc = slice(hh * GLA_DV, (hh + 1) * GLA_DV)
        a = jnp.concatenate([_dot_nt(q_loc[c][:, kc], k_dst[c][:, kc]) for c in range(nc)], axis=0)
        a = jnp.where(causal, a, 0.0).astype(BF16)
        v = gv_ref[:, vc]
        s_old = s_ref[hh]
        o_ref[:, vc] = (_dot(a, v) + _dot(q_tile[:, kc], s_old.astype(BF16))).astype(BF16)
        decay_col = jnp.broadcast_to(decay[:, kc], (GLA_DK, GLA_DK)).T
        decay_col = jnp.concatenate([decay_col] * (GLA_DV // GLA_DK), axis=1)
        s_ref[hh] = s_old * decay_col + _dot_tn(k_state[:, kc], v)


def _backward_scan_tile(pos, n_tiles):
    return jnp.where(pos == 0, 0, n_tiles - pos)


def _gla_kernel(*refs):
    gqf, gkf, gvf, bf = refs[0:4]
    bwd_in = refs[4:4 + 4 * SUB_TILES]
    of_ref, ob_ref, sf_ref, sb_ref = refs[4 + 4 * SUB_TILES:]

    @pl.when(pl.program_id(0) == 0)
    def _():
        sf_ref[...] = jnp.zeros_like(sf_ref)
        sb_ref[...] = jnp.zeros_like(sb_ref)

    for t in range(SUB_TILES):
        rows = slice(t * TM, (t + 1) * TM)
        _gla_direction(0, gqf.at[rows, :], gkf.at[rows, :], gvf.at[rows, :], bf.at[rows, :],
                       of_ref.at[rows, :], sf_ref)
        gqb, gkb, gvb, bb = bwd_in[4 * t:4 * t + 4]
        _gla_direction(1, gqb, gkb, gvb, bb, ob_ref.at[rows, :], sb_ref)


def _gla_scan(gq, gk, gv, b_fwd, b_bwd):
    t_tot = gq.shape[0]
    nt = t_tot // TM
    bwd_spec = lambda w, t: pl.BlockSpec(
        (TM, w), lambda s: (_backward_scan_tile(SUB_TILES * s + t, nt), 0))
    ins = [_step_spec(GLA_K), _step_spec(GLA_K), _step_spec(GLA_V), _step_spec(GLA_K)]
    args = [gq, gk, gv, b_fwd]
    for t in range(SUB_TILES):
        ins += [bwd_spec(GLA_K, t), bwd_spec(GLA_K, t), bwd_spec(GLA_V, t), bwd_spec(GLA_K, t)]
        args += [gq, gk, gv, b_bwd]
    return pl.pallas_call(
        _gla_kernel,
        grid=(t_tot // (SUB_TILES * TM),),
        in_specs=ins,
        out_specs=[_step_spec(GLA_V), _step_spec(GLA_V)],
        out_shape=[jax.ShapeDtypeStruct((t_tot, GLA_V), BF16)] * 2,
        scratch_shapes=[pltpu.VMEM((GLA_HEADS, GLA_DK, GLA_DV), F32)] * 2,
        compiler_params=_cparams(1),
        name="gla_scan",
    )(*args)


def _merge_kernel(*refs, n_x):
    x_refs = refs[:n_x]
    mod_ref, g1_ref, wg_ref, attn_ref, of_ref = refs[n_x:n_x + 5]
    ob_refs = refs[n_x + 5:n_x + 5 + SUB_TILES]
    gain_ref, wa_ref, wl_ref, wo_ref, o_ref = refs[n_x + 5 + SUB_TILES:]
    gain = gain_ref[...]
    for t in range(SUB_TILES):
        rows = slice(t * TM, (t + 1) * TM)
        mod = _tile_mod(mod_ref, t)
        x = _tile_input(x_refs, t)
        h = _norm_modulate(x, g1_ref[...], mod[:, 0:D_MODEL], mod[:, D_MODEL:2 * D_MODEL]).astype(BF16)
        parts = []
        for hh in range(GLA_HEADS):
            vc = slice(hh * GLA_DV, (hh + 1) * GLA_DV)
            o = of_ref[rows, vc].astype(F32) + ob_refs[t][:, vc].astype(F32)
            ms = jnp.mean(o * o, axis=-1, keepdims=True)
            on = o * lax.rsqrt(ms + EPS) * gain
            gr = _dot(h, wg_ref[:, vc])
            parts.append((on * (gr * jax.nn.sigmoid(gr))).astype(BF16))
        gla = jnp.concatenate(parts, axis=1)
        gate_a = jax.nn.sigmoid(_dot(h, wg_ref[:, GLA_V:GLA_V + D_MODEL]))
        gate_b = jax.nn.sigmoid(_dot(h, wg_ref[:, GLA_V + D_MODEL:GLA_V + 2 * D_MODEL]))
        y = gate_a * _dot(attn_ref[rows, :], wa_ref[...]) + gate_b * _dot(gla, wl_ref[...])
        gt1 = mod[:, 2 * D_MODEL:3 * D_MODEL]
        o_ref[rows, :] = x + gt1 * _dot(y.astype(BF16), wo_ref[...])


def _merge_out(layer, xs, mod, g1, w_g, attn, o_f, o_b, gain, wa, wl, wo):
    t_tot = sum(a.shape[0] for a in xs)
    nt = t_tot // TM
    x_args = _tile_input_args(xs)
    ob_spec = lambda t: pl.BlockSpec((TM, GLA_V), lambda i: (_backward_scan_tile(SUB_TILES * i + t, nt), 0))
    return pl.pallas_call(
        functools.partial(_merge_kernel, n_x=len(x_args)),
        grid=(t_tot // (SUB_TILES * TM),),
        in_specs=_tile_input_specs(xs) + [
            _mod_pair_spec(layer), _layer_spec(g1, layer), _layer_spec(w_g, layer),
            _step_spec(ATT_Q), _step_spec(GLA_V)] + [ob_spec(t) for t in range(SUB_TILES)] + [
            _layer_spec(gain, layer), _layer_spec(wa, layer), _layer_spec(wl, layer), _layer_spec(wo, layer)],
        out_specs=_step_spec(D_MODEL),
        out_shape=jax.ShapeDtypeStruct((t_tot, D_MODEL), F32),
        compiler_params=_cparams(1),
        name="merge_out",
    )(*x_args, mod, g1, w_g, attn, o_f, *([o_b] * SUB_TILES), gain, wa, wl, wo)


FF_CHUNK = 1024


def _mlp_kernel(x_ref, mod_ref, g2_ref, w1_ref, w2_ref, o_ref, *, sub):
    for t in range(sub):
        rows = slice(t * TM, (t + 1) * TM)
        mod = _tile_mod(mod_ref, t)
        x = x_ref[rows, :]
        h = _norm_modulate(x, g2_ref[...], mod[:, 3 * D_MODEL:4 * D_MODEL], mod[:, 4 * D_MODEL:5 * D_MODEL])
        h = h.astype(BF16)
        acc = jnp.zeros((TM, D_MODEL), F32)
        for j in range(D_FF // FF_CHUNK):
            cs = slice(j * FF_CHUNK, (j + 1) * FF_CHUNK)
            u = jnp.maximum(_dot(h, w1_ref[:, cs]), 0.0)
            acc = acc + _dot((u * u).astype(BF16), w2_ref[cs, :])
        o_ref[rows, :] = x + mod[:, 5 * D_MODEL:6 * D_MODEL] * acc


def _mlp(layer, xc, mod, g2, w1, w2, latent_only):
    t_tot = xc.shape[0]
    if latent_only:
        sub = 1
        out_rows = t_tot - TM
        out_spec = pl.BlockSpec((TM, D_MODEL), lambda i: (jnp.maximum(i - 1, 0), 0))
    else:
        sub = SUB_TILES
        out_rows = t_tot
        out_spec = pl.BlockSpec((sub * TM, D_MODEL), lambda i: (i, 0))
    return pl.pallas_call(
        functools.partial(_mlp_kernel, sub=sub),
        grid=(t_tot // (sub * TM),),
        in_specs=[pl.BlockSpec((sub * TM, D_MODEL), lambda i: (i, 0)), _mod_pair_spec(layer), _layer_spec(g2, layer),
                  _layer_spec(w1, layer), _layer_spec(w2, layer)],
        out_specs=out_spec,
        out_shape=jax.ShapeDtypeStruct((out_rows, D_MODEL), F32),
        compiler_params=_cparams(1),
        name="mlp",
    )(xc, mod, g2, w1, w2)


def _rope_tables(n_lat, n_ctx):
    half = ROPE_AXIS_DIM // 2
    lane = np.arange(LANES)
    dd = lane % HEAD_DIM
    inv = (ROPE_BASE ** (-(dd % half).astype(np.float64) * 2.0 / ROPE_AXIS_DIM)).astype(np.float32)
    is_row = dd < ROPE_AXIS_DIM
    sign = np.where((dd % ROPE_AXIS_DIM) < half, -1.0, 1.0)
    rows = n_lat // GRID_W
    ang_r = np.arange(rows, dtype=np.float32).astype(np.float64)[:, None] * inv[None, :]
    ang_c = np.arange(GRID_W, dtype=np.float32).astype(np.float64)[:, None] * inv[None, :]
    m = jnp.asarray(is_row)[None, None, :]
    pick = lambda fr, fc: jnp.where(m, jnp.asarray(fr, F32)[:, None, :], jnp.asarray(fc, F32)[None, :, :])
    cos = pick(np.cos(ang_r), np.cos(ang_c)).reshape(n_lat, LANES)
    sin = pick(np.sin(ang_r) * sign, np.sin(ang_c) * sign).reshape(n_lat, LANES)
    cos = jnp.concatenate([jnp.ones((n_ctx, LANES), F32), cos], axis=0)
    sin = jnp.concatenate([jnp.zeros((n_ctx, LANES), F32), sin], axis=0)
    return cos, sin


def _tile_cumsum_matrices():
    i = np.arange(TM)[:, None]
    j = np.arange(TM)[None, :]
    return jnp.asarray(np.stack([j <= i, j >= i]).astype(np.float32) / GLA_NORMALIZER, BF16)


def kernel(x, c, ctx, c_ctx, w_mod, b_mod, g_norm1, w_in, q_gain, k_gain, sink, w_decay, b_decay,
           gla_gain, w_branch_attn, w_branch_gla, w_out, g_norm2, w_ff1, w_ff2):
    assert x.shape[0] == 1 and ctx.shape[0] == 1
    n_lat, n_ctx = x.shape[1], ctx.shape[1]
    assert n_ctx == TM and (n_ctx + n_lat) % (SUB_TILES * TM) == 0
    depth = w_mod.shape[0]

    cvec = jnp.zeros((MOD_ROWS, D_MODEL), F32).at[0].set(c[0]).at[1].set(c_ctx)
    mod = _mod_vectors(cvec, w_mod, b_mod).reshape(depth * MOD_ROWS, 1, N_MOD * D_MODEL)
    cos, sin = _rope_tables(n_lat, n_ctx)
    tri = _tile_cumsum_matrices()

    w_in_bf = _to_bf16(w_in)
    w_g = jnp.concatenate([w_in_bf[:, :, C_GR:C_GA], w_in_bf[:, :, C_MERGE:]], axis=2)
    wa = w_branch_attn.astype(BF16)
    wl = w_branch_gla.astype(BF16)
    wo = w_out.astype(BF16)
    w1 = w_ff1.astype(BF16)
    w2 = w_ff2.astype(BF16)
    wd = jnp.zeros((depth, 2, LANES, GLA_K), F32)
    wd = wd.at[:, 0, 0:GLA_RANK].set(w_decay[:, 0]).at[:, 1, GLA_RANK:2 * GLA_RANK].set(w_decay[:, 1]).astype(BF16)
    bd = b_decay.reshape(depth, 2, 1, GLA_K)
    g1 = g_norm1.reshape(depth, 1, D_MODEL)
    g2 = g_norm2.reshape(depth, 1, D_MODEL)
    qg = jnp.tile(q_gain, (1, LANES // HEAD_DIM)).reshape(depth, 1, LANES)
    kg = jnp.tile(k_gain, (1, LANES // HEAD_DIM)).reshape(depth, 1, LANES)
    gain = gla_gain.reshape(depth, 1, GLA_DV)
    sink_lanes = jnp.repeat((sink * LOG2E).reshape(depth, N_KV_HEADS, 1, GROUP), ATT_BLOCK, axis=3)
    logit_bound = (HEAD_DIM * Q_SCALE * BOUND_MARGIN) * jnp.max(jnp.abs(q_gain), axis=1) * jnp.max(jnp.abs(k_gain), axis=1)
    stab_lanes = jnp.maximum(sink_lanes, logit_bound.reshape(depth, 1, 1, 1))

    def layers(xs, bounded_softmax):
        for l in range(depth):
            q, k, vt, gq, gk, gv, b_fwd, b_bwd = _in_proj(l, xs, mod, g1, w_in_bf, qg, kg, cos, sin, wd, bd, tri)
            attn = _window_attn(l, q, k, vt, sink_lanes, stab_lanes if bounded_softmax else None, n_ctx)
            o_f, o_b = _gla_scan(gq, gk, gv, b_fwd, b_bwd)
            xc = _merge_out(l, xs, mod, g1, w_g, attn, o_f, o_b, gain, wa, wl, wo)
            xc = _mlp(l, xc, mod, g2, w1, w2, latent_only=(l == depth - 1))
            xs = (xc,)
        return xc

    xc = lax.cond(jnp.all(logit_bound <= MAX_SAFE_BOUND),
                  lambda *xs: layers(xs, True), lambda *xs: layers(xs, False),
                  ctx[0], x[0])
    return xc[None]
```

```python
import functools

import numpy as np
import jax
import jax.numpy as jnp
from jax import lax
from jax.experimental import pallas as pl
from jax.experimental.pallas import tpu as pltpu

F32 = jnp.float32
BF16 = jnp.bfloat16

D_MODEL = 1024
N_HEADS = 16
N_KV_HEADS = 4
HEAD_DIM = 64
GROUP = N_HEADS // N_KV_HEADS
ATT_BLOCK = 128
GRID_W = 64
ROPE_BASE = 10000.0
ROPE_AXIS_DIM = HEAD_DIM // 2
GLA_HEADS = 4
GLA_DK = 128
GLA_DV = 256
GLA_RANK = 16
GLA_NORMALIZER = 16.0
GLA_CHUNK = 64
D_FF = 4 * D_MODEL
N_MOD = 6
EPS = 1e-6
NEG = -1e30
LOG2E = 1.4426950408889634

ATT_Q = N_HEADS * HEAD_DIM
ATT_KV = N_KV_HEADS * HEAD_DIM
GLA_K = GLA_HEADS * GLA_DK
GLA_V = GLA_HEADS * GLA_DV

LANES = 128
TM = 256
SUB_TILES = 5
CHUNKS_PER_TILE = TM // GLA_CHUNK
MOD_ROWS = 8
VMEM_LIMIT = 56 * 1024 * 1024


def _cparams(n_axes):
    return pltpu.CompilerParams(dimension_semantics=("arbitrary",) * n_axes,
                                vmem_limit_bytes=VMEM_LIMIT)


def _dot(a, b):
    return jnp.dot(a, b, preferred_element_type=F32)


def _dot_nt(a, b):
    return lax.dot_general(a, b, (((1,), (1,)), ((), ())), preferred_element_type=F32)


def _dot_tn(a, b):
    return lax.dot_general(a, b, (((0,), (0,)), ((), ())), preferred_element_type=F32)


def _split2(x):
    x1 = x.astype(BF16)
    return x1, (x - x1.astype(F32)).astype(BF16)


def _layer_spec(arr, layer):
    rest = arr.shape[1:]
    return pl.BlockSpec((None,) + rest, lambda *_: (layer,) + (0,) * len(rest))


def _mod_pair_spec(layer):
    return pl.BlockSpec((2, 1, N_MOD * D_MODEL), lambda i: (layer * (MOD_ROWS // 2), 0, 0))


def _tile_mod(mod_ref, t):
    if t > 0:
        return mod_ref[0]
    return jnp.where(pl.program_id(0) == 0, mod_ref[1], mod_ref[0])


MOD_TN = 3072


def _mod_kernel(c_ref, w_ref, b_ref, o_ref):
    c = c_ref[...]
    s1, s2 = _split2(c * jax.nn.sigmoid(c))
    w1, w2 = _split2(w_ref[0])
    o_ref[0] = _dot(s1, w1) + _dot(s2, w1) + _dot(s1, w2) + b_ref[0]


def _mod_vectors(cvec, w_mod, b_mod):
    depth = w_mod.shape[0]
    n_out = w_mod.shape[2]
    return pl.pallas_call(
        _mod_kernel,
        grid=(depth, n_out // MOD_TN),
        in_specs=[pl.BlockSpec((MOD_ROWS, D_MODEL), lambda l, j: (0, 0)),
                  pl.BlockSpec((1, D_MODEL, MOD_TN), lambda l, j: (l, 0, j)),
                  pl.BlockSpec((1, 1, MOD_TN), lambda l, j: (l, 0, j))],
        out_specs=pl.BlockSpec((1, MOD_ROWS, MOD_TN), lambda l, j: (l, 0, j)),
        out_shape=jax.ShapeDtypeStruct((depth, MOD_ROWS, n_out), F32),
        compiler_params=_cparams(2),
        name="mod_vectors",
    )(cvec, w_mod, b_mod.reshape(depth, 1, n_out))


CAST_ROWS = 256


def _cast_kernel(x_ref, o_ref):
    o_ref[...] = x_ref[...].astype(BF16)


def _to_bf16(w):
    depth, rows, cols = w.shape
    spec = pl.BlockSpec((1, CAST_ROWS, cols), lambda l, r: (l, r, 0))
    return pl.pallas_call(
        _cast_kernel,
        grid=(depth, rows // CAST_ROWS),
        in_specs=[spec],
        out_specs=spec,
        out_shape=jax.ShapeDtypeStruct(w.shape, BF16),
        compiler_params=_cparams(2),
        name="to_bf16",
    )(w)


def _norm_modulate(x, g, shift, scale):
    ms = jnp.mean(x * x, axis=-1, keepdims=True)
    y = x * lax.rsqrt(ms + EPS) * g
    return y * (1.0 + scale) + shift


C_Q = 0
C_K = C_Q + ATT_Q
C_V = C_K + ATT_KV
C_GQ = C_V + ATT_KV
C_GK = C_GQ + GLA_K
C_GV = C_GK + GLA_K
C_GR = C_GV + GLA_V
C_GA = C_GR + GLA_V
C_END = C_GA + LANES
C_MERGE = C_GA + 2 * GLA_RANK
Q_SCALE = HEAD_DIM ** -0.5 * LOG2E


def _headnorm_rope(z, gain, cos, sin, lane):
    lo = lane < HEAD_DIM
    sq = z * z
    s_lo = jnp.sum(jnp.where(lo, sq, 0.0), axis=-1, keepdims=True)
    s_hi = jnp.sum(jnp.where(lo, 0.0, sq), axis=-1, keepdims=True)
    ms = jnp.where(lo, s_lo, s_hi) * (1.0 / HEAD_DIM)
    y = z * lax.rsqrt(ms + EPS) * gain
    half = ROPE_AXIS_DIM // 2
    nxt = pltpu.roll(y, LANES - half, 1)
    prv = pltpu.roll(y, half, 1)
    partner = jnp.where((lane % ROPE_AXIS_DIM) < half, nxt, prv)
    return y * cos + partner * sin


def _tile_input(x_refs, t):
    if len(x_refs) == 1:
        return x_refs[0][t * TM:(t + 1) * TM, :]
    if t > 0:
        return x_refs[1 + t][...]
    return jnp.where(pl.program_id(0) == 0, x_refs[0][...], x_refs[1][...])


def _tile_input_specs(xs):
    if len(xs) == 1:
        return [_step_spec(D_MODEL)]
    lat = lambda t: pl.BlockSpec((TM, D_MODEL), lambda i: (jnp.maximum(SUB_TILES * i + t - 1, 0), 0))
    return [pl.BlockSpec((TM, D_MODEL), lambda i: (0, 0))] + [lat(t) for t in range(SUB_TILES)]


def _tile_input_args(xs):
    return xs if len(xs) == 1 else (xs[0],) + (xs[1],) * SUB_TILES


def _step_spec(width):
    return pl.BlockSpec((SUB_TILES * TM, width), lambda i: (i, 0))


def _in_proj_kernel(*refs, n_x):
    x_refs = refs[:n_x]
    (mod_ref, g1_ref, w_ref, qg_ref, kg_ref, cos_ref, sin_ref, wd_ref, bd_ref, tri_ref,
     q_ref, k_ref, vt_ref, gq_ref, gk_ref, gv_ref, bf_ref, bb_ref) = refs[n_x:]
    for t in range(SUB_TILES):
        rows = slice(t * TM, (t + 1) * TM)
        _in_proj_tile(_tile_input(x_refs, t), _tile_mod(mod_ref, t), g1_ref, w_ref, qg_ref, kg_ref,
                      cos_ref[rows, :], sin_ref[rows, :], wd_ref, bd_ref, tri_ref,
                      q_ref.at[rows, :], k_ref.at[rows, :], vt_ref.at[:, rows], gq_ref.at[rows, :],
                      gk_ref.at[rows, :], gv_ref.at[rows, :], bf_ref.at[rows, :], bb_ref.at[rows, :])


def _in_proj_tile(x, mod, g1_ref, w_ref, qg_ref, kg_ref, cos, sin, wd_ref, bd_ref, tri_ref,
                  q_ref, k_ref, vt_ref, gq_ref, gk_ref, gv_ref, bf_ref, bb_ref):
    h = _norm_modulate(x, g1_ref[...], mod[:, 0:D_MODEL], mod[:, D_MODEL:2 * D_MODEL])
    h = h.astype(BF16)
    lane = lax.broadcasted_iota(jnp.int32, (TM, LANES), 1)
    lo = lane < HEAD_DIM
    qg = qg_ref[...]
    kg = kg_ref[...]
    ga = _dot(h, w_ref[:, C_GA:C_GA + LANES]).astype(BF16)
    decay_pre = [_dot(ga, wd_ref[d]) + bd_ref[d] for d in range(2)]
    for j in range(N_KV_HEADS // 2):
        w = 2 * GROUP * HEAD_DIM
        z = _dot(h, w_ref[:, C_Q + j * w:C_Q + (j + 1) * w])
        r = [_headnorm_rope(z[:, i * LANES:(i + 1) * LANES], qg, cos, sin, lane) * Q_SCALE
             for i in range(w // LANES)]
        for hh in range(GROUP):
            a = r[hh // 2]
            b = r[GROUP // 2 + hh // 2]
            if hh % 2 == 0:
                out = jnp.where(lo, a, pltpu.roll(b, HEAD_DIM, 1))
            else:
                out = jnp.where(lo, pltpu.roll(a, HEAD_DIM, 1), b)
            c0 = (j * GROUP + hh) * LANES
            q_ref[:, c0:c0 + LANES] = out.astype(BF16)
    for d, b_ref in enumerate((bf_ref, bb_ref)):
        x = decay_pre[d]
        u = jnp.exp2(jnp.abs(x) * (-LOG2E))
        g = (jnp.minimum(x, 0.0) * LOG2E - jnp.log2(1.0 + u)).astype(BF16)
        b_ref[...] = _dot(tri_ref[d], g)
    z = _dot(h, w_ref[:, C_K:C_K + ATT_KV])
    for e in range(2):
        r = _headnorm_rope(z[:, e * LANES:(e + 1) * LANES], kg, cos, sin, lane)
        k_ref[:, e * LANES:(e + 1) * LANES] = r.astype(BF16)
    z = _dot(h, w_ref[:, C_V:C_V + ATT_KV])
    vt_ref[...] = z.T.astype(BF16)
    gq_ref[...] = _dot(h, w_ref[:, C_GQ:C_GQ + GLA_K]).astype(BF16)
    gk_ref[...] = _dot(h, w_ref[:, C_GK:C_GK + GLA_K]).astype(BF16)
    for j in range(GLA_V // 512):
        gv_ref[:, j * 512:(j + 1) * 512] = _dot(h, w_ref[:, C_GV + j * 512:C_GV + (j + 1) * 512]).astype(BF16)


def _in_proj(layer, xs, mod, g1, w_in_bf, qg, kg, cos, sin, wd, bd, tri):
    t_tot = sum(a.shape[0] for a in xs)
    depth, d_model, _ = w_in_bf.shape
    x_args = _tile_input_args(xs)
    return pl.pallas_call(
        functools.partial(_in_proj_kernel, n_x=len(x_args)),
        grid=(t_tot // (SUB_TILES * TM),),
        in_specs=_tile_input_specs(xs) + [
            _mod_pair_spec(layer), _layer_spec(g1, layer),
            pl.BlockSpec((None, d_model, C_END), lambda i: (layer, 0, 0)),
            _layer_spec(qg, layer), _layer_spec(kg, layer), _step_spec(LANES), _step_spec(LANES),
            _layer_spec(wd, layer), _layer_spec(bd, layer), pl.BlockSpec(tri.shape, lambda i: (0, 0, 0))],
        out_specs=[_step_spec(ATT_Q), _step_spec(ATT_KV), pl.BlockSpec((ATT_KV, SUB_TILES * TM), lambda i: (0, i)),
                   _step_spec(GLA_K), _step_spec(GLA_K), _step_spec(GLA_V), _step_spec(GLA_K), _step_spec(GLA_K)],
        out_shape=[jax.ShapeDtypeStruct((t_tot, ATT_Q), BF16),
                   jax.ShapeDtypeStruct((t_tot, ATT_KV), BF16),
                   jax.ShapeDtypeStruct((ATT_KV, t_tot), BF16),
                   jax.ShapeDtypeStruct((t_tot, GLA_K), BF16),
                   jax.ShapeDtypeStruct((t_tot, GLA_K), BF16),
                   jax.ShapeDtypeStruct((t_tot, GLA_V), BF16),
                   jax.ShapeDtypeStruct((t_tot, GLA_K), F32),
                   jax.ShapeDtypeStruct((t_tot, GLA_K), F32)],
        compiler_params=_cparams(1),
        name="in_proj",
    )(*x_args, mod, g1, w_in_bf, qg, kg, cos, sin, wd, bd, tri)


ONES_ROWS = 16
BOUND_MARGIN = 1.01
MAX_SAFE_BOUND = 60.0


ATT_STEP = SUB_TILES * TM


def _attn_blocks(q_ref, sink_ref, stab_ref, o_ref, key_lists):
    lane_q = lax.broadcasted_iota(jnp.int32, (ATT_BLOCK, LANES), 1)

    def scores(blk, g):
        j, e = divmod(g, 2)
        cs = slice(j * LANES, (j + 1) * LANES)
        qrows = slice(blk * ATT_BLOCK, (blk + 1) * ATT_BLOCK)
        half = (lane_q < HEAD_DIM) if e == 0 else (lane_q >= HEAD_DIM)
        qs = []
        for hh in range(GROUP):
            qc = q_ref[qrows, (j * GROUP + hh) * LANES:(j * GROUP + hh + 1) * LANES]
            qs.append(jnp.where(half, qc, jnp.zeros_like(qc)))
        qg = jnp.concatenate(qs, axis=0)
        keys = key_lists[blk]
        k_all = jnp.concatenate([kr[r0:r0 + ATT_BLOCK, cs] for kr, _, r0, _ in keys], axis=0)
        s = _dot_nt(k_all, qg)
        parts = []
        for i, (_, _, _, mk) in enumerate(keys):
            sp = s[i * ATT_BLOCK:(i + 1) * ATT_BLOCK]
            parts.append(sp if mk is None else jnp.where(mk, sp, NEG))
        return parts

    def finish(blk, g, parts):
        keys = key_lists[blk]
        sk = sink_ref[g]
        if stab_ref is None:
            tile_max = parts[0]
            for sp in parts[1:]:
                tile_max = jnp.maximum(tile_max, sp)
            m = jnp.maximum(sk, jnp.max(tile_max, axis=0, keepdims=True))
        else:
            m = stab_ref[g]
        p = jnp.concatenate([jnp.exp2(sp - m) for sp in parts], axis=0).astype(BF16)
        rs = slice(g * HEAD_DIM, (g + 1) * HEAD_DIM)
        vt_all = jnp.concatenate([vr[rs, r0:r0 + ATT_BLOCK] for _, vr, r0, _ in keys], axis=1)
        ones = jnp.ones((ONES_ROWS, len(keys) * ATT_BLOCK), BF16)
        ot = _dot(jnp.concatenate([vt_all, ones], axis=0), p)
        denom = ot[HEAD_DIM:HEAD_DIM + 1] + jnp.exp2(sk - m)
        ot = ot[0:HEAD_DIM] * (1.0 / denom)
        qrows = slice(blk * ATT_BLOCK, (blk + 1) * ATT_BLOCK)
        for pair in range(GROUP // 2):
            x2 = jnp.concatenate([ot[:, (2 * pair) * ATT_BLOCK:(2 * pair + 1) * ATT_BLOCK],
                                  ot[:, (2 * pair + 1) * ATT_BLOCK:(2 * pair + 2) * ATT_BLOCK]], axis=0)
            c0 = g * GROUP * HEAD_DIM + pair * LANES
            o_ref[qrows, c0:c0 + LANES] = x2.T.astype(BF16)

    units = [(blk, g) for blk in range(len(key_lists)) for g in range(N_KV_HEADS)]
    pending = scores(*units[0])
    for i, unit in enumerate(units):
        upcoming = scores(*units[i + 1]) if i + 1 < len(units) else None
        finish(*unit, pending)
        pending = upcoming


def _attn_kernel(q_ref, kp_ref, kc_ref, kn_ref, kx_ref, vp_ref, vc_ref, vn_ref, vx_ref, sink_ref,
                 *rest, n_steps, ctx_blocks):
    stab_ref, o_ref = rest if len(rest) == 2 else (None, rest[0])
    s = pl.program_id(0)
    n_ctx = kx_ref.shape[0]
    ctx_keys = [(kx_ref, vx_ref, r0, None) for r0 in range(0, n_ctx, ATT_BLOCK)]
    gw = GROUP * ATT_BLOCK
    key_i = lax.broadcasted_iota(jnp.int32, (ATT_BLOCK, gw), 0)
    qry_i = lax.broadcasted_iota(jnp.int32, (ATT_BLOCK, gw), 1) % ATT_BLOCK
    before = key_i >= qry_i
    after = key_i <= qry_i
    everything = key_i >= 0
    past_first = s > 0
    before_last = s < n_steps - 1
    blocks_per_step = ATT_STEP // ATT_BLOCK
    key_lists = []
    for b in range(blocks_per_step):
        prev = (kc_ref, vc_ref, (b - 1) * ATT_BLOCK) if b > 0 else (kp_ref, vp_ref, 0)
        cur = (kc_ref, vc_ref, b * ATT_BLOCK)
        nxt = (kc_ref, vc_ref, (b + 1) * ATT_BLOCK) if b + 1 < blocks_per_step else (kn_ref, vn_ref, 0)
        m_prev = jnp.logical_and(before, past_first) if b <= ctx_blocks else before
        m_cur = jnp.logical_and(everything, past_first) if b < ctx_blocks else None
        m_next = jnp.logical_and(after, past_first) if b < ctx_blocks else after
        if b + 1 == blocks_per_step:
            m_next = jnp.logical_and(after, before_last)
        key_lists.append([prev + (m_prev,), cur + (m_cur,), nxt + (m_next,)] + ctx_keys)
    _attn_blocks(q_ref, sink_ref, stab_ref, o_ref, key_lists)


def _window_attn(layer, q, k, vt, sink_lanes, stab_lanes, n_ctx):
    t_tot = q.shape[0]
    stab = [] if stab_lanes is None else [stab_lanes]
    assert t_tot % ATT_STEP == 0
    ns = t_tot // ATT_STEP
    nb = t_tot // ATT_BLOCK
    cb = n_ctx // ATT_BLOCK
    assert cb + 1 < ATT_STEP // ATT_BLOCK
    per = ATT_STEP // ATT_BLOCK
    prev_i = lambda s: jnp.clip(per * s - 1, 0, nb - 1)
    next_i = lambda s: jnp.clip(per * s + per, 0, nb - 1)
    ident = lambda s: s
    kspec = lambda rows, f: pl.BlockSpec((rows, ATT_KV), lambda s: (f(s), 0))
    vspec = lambda rows, f: pl.BlockSpec((ATT_KV, rows), lambda s: (0, f(s)))
    return pl.pallas_call(
        functools.partial(_attn_kernel, n_steps=ns, ctx_blocks=cb),
        grid=(ns,),
        in_specs=[pl.BlockSpec((ATT_STEP, ATT_Q), lambda s: (s, 0)),
                  kspec(ATT_BLOCK, prev_i), kspec(ATT_STEP, ident), kspec(ATT_BLOCK, next_i),
                  pl.BlockSpec((n_ctx, ATT_KV), lambda s: (0, 0)),
                  vspec(ATT_BLOCK, prev_i), vspec(ATT_STEP, ident), vspec(ATT_BLOCK, next_i),
                  pl.BlockSpec((ATT_KV, n_ctx), lambda s: (0, 0)),
                  _layer_spec(sink_lanes, layer)] + [_layer_spec(a, layer) for a in stab],
        out_specs=pl.BlockSpec((ATT_STEP, ATT_Q), lambda s: (s, 0)),
        out_shape=jax.ShapeDtypeStruct((t_tot, ATT_Q), BF16),
        compiler_params=_cparams(1),
        name="window_attn" if stab_lanes is None else "window_attn_bounded",
    )(q, k, k, k, k, vt, vt, vt, vt, sink_lanes, *stab)


def _gla_direction(d, gq_ref, gk_ref, gv_ref, b_ref, o_ref, s_ref):
    nc = CHUNKS_PER_TILE
    rows = lambda c: slice(c * GLA_CHUNK, (c + 1) * GLA_CHUNK)
    bcum = b_ref[...]
    zero_row = jnp.zeros((1, GLA_K), F32)
    if d == 0:
        earlier = lambda c: range(0, c)
        b_start = lambda c: zero_row if c == 0 else bcum[c * GLA_CHUNK - 1:c * GLA_CHUNK]
        b_end = lambda c: bcum[(c + 1) * GLA_CHUNK - 1:(c + 1) * GLA_CHUNK]
        last_chunk = nc - 1
    else:
        earlier = lambda c: range(c + 1, nc)
        b_start = lambda c: zero_row if c == nc - 1 else bcum[(c + 1) * GLA_CHUNK:(c + 1) * GLA_CHUNK + 1]
        b_end = lambda c: bcum[c * GLA_CHUNK:c * GLA_CHUNK + 1]
        last_chunk = 0
    total = b_end(last_chunk)
    gq = gq_ref[...].astype(F32) * (GLA_DK ** -0.5)
    gk = gk_ref[...].astype(F32)
    q_loc, q_tile, k_in, k_out, k_raw = [], [], [], [], []
    for c in range(nc):
        b_loc = bcum[rows(c)] - b_start(c)
        ql = gq[rows(c)] * jnp.exp2(b_loc)
        q_loc.append(ql.astype(BF16))
        q_tile.append((ql * jnp.exp2(b_start(c))).astype(BF16))
        k_in.append((gk[rows(c)] * jnp.exp2(-b_loc)).astype(BF16))
        k_out.append(gk[rows(c)] * jnp.exp2(b_end(c) - bcum[rows(c)]))
        k_raw.append(gk_ref[rows(c), :])
    k_out_bf = [k.astype(BF16) for k in k_out]

    def key_seen_from(c, src):
        if src == c:
            return k_in[c]
        if src not in earlier(c):
            return k_raw[src]
        if abs(src - c) == 1:
            return k_out_bf[src]
        return (k_out[src] * jnp.exp2(b_start(c) - b_end(src))).astype(BF16)

    k_dst = [jnp.concatenate([key_seen_from(c, src) for src in range(nc)], axis=0) for c in range(nc)]
    k_state = jnp.concatenate(
        [k_out_bf[c] if c == last_chunk else (k_out[c] * jnp.exp2(total - b_end(c))).astype(BF16)
         for c in range(nc)], axis=0)
    q_tile = jnp.concatenate(q_tile, axis=0)
    ti = lax.broadcasted_iota(jnp.int32, (TM, TM), 0)
    tj = lax.broadcasted_iota(jnp.int32, (TM, TM), 1)
    causal = (tj <= ti) if d == 0 else (tj >= ti)
    decay = jnp.exp2(total)

    for hh in range(GLA_HEADS):
        kc = slice(hh * GLA_DK, (hh + 1) * GLA_DK)
        vc = slice(hh * GLA_DV, (hh + 1) * GLA_DV)
        a = jnp.concatenate([_dot_nt(q_loc[c][:, kc], k_dst[c][:, kc]) for c in range(nc)], axis=0)
        a = jnp.where(causal, a, 0.0).astype(BF16)
        v = gv_ref[:, vc]
        s_old = s_ref[hh]
        o_ref[:, vc] = (_dot(a, v) + _dot(q_tile[:, kc], s_old.astype(BF16))).astype(BF16)
        decay_col = jnp.broadcast_to(decay[:, kc], (GLA_DK, GLA_DK)).T
        decay_col = jnp.concatenate([decay_col] * (GLA_DV // GLA_DK), axis=1)
        s_ref[hh] = s_old * decay_col + _dot_tn(k_state[:, kc], v)


def _backward_scan_tile(pos, n_tiles):
    return jnp.where(pos == 0, 0, n_tiles - pos)


def _gla_kernel(*refs):
    gqf, gkf, gvf, bf = refs[0:4]
    bwd_in = refs[4:4 + 4 * SUB_TILES]
    of_ref, ob_ref, sf_ref, sb_ref = refs[4 + 4 * SUB_TILES:]

    @pl.when(pl.program_id(0) == 0)
    def _():
        sf_ref[...] = jnp.zeros_like(sf_ref)
        sb_ref[...] = jnp.zeros_like(sb_ref)

    for t in range(SUB_TILES):
        rows = slice(t * TM, (t + 1) * TM)
        _gla_direction(0, gqf.at[rows, :], gkf.at[rows, :], gvf.at[rows, :], bf.at[rows, :],
                       of_ref.at[rows, :], sf_ref)
        gqb, gkb, gvb, bb = bwd_in[4 * t:4 * t + 4]
        _gla_direction(1, gqb, gkb, gvb, bb, ob_ref.at[rows, :], sb_ref)


def _gla_scan(gq, gk, gv, b_fwd, b_bwd):
    t_tot = gq.shape[0]
    nt = t_tot // TM
    bwd_spec = lambda w, t: pl.BlockSpec(
        (TM, w), lambda s: (_backward_scan_tile(SUB_TILES * s + t, nt), 0))
    ins = [_step_spec(GLA_K), _step_spec(GLA_K), _step_spec(GLA_V), _step_spec(GLA_K)]
    args = [gq, gk, gv, b_fwd]
    for t in range(SUB_TILES):
        ins += [bwd_spec(GLA_K, t), bwd_spec(GLA_K, t), bwd_spec(GLA_V, t), bwd_spec(GLA_K, t)]
        args += [gq, gk, gv, b_bwd]
    return pl.pallas_call(
        _gla_kernel,
        grid=(t_tot // (SUB_TILES * TM),),
        in_specs=ins,
        out_specs=[_step_spec(GLA_V), _step_spec(GLA_V)],
        out_shape=[jax.ShapeDtypeStruct((t_tot, GLA_V), BF16)] * 2,
        scratch_shapes=[pltpu.VMEM((GLA_HEADS, GLA_DK, GLA_DV), F32)] * 2,
        compiler_params=_cparams(1),
        name="gla_scan",
    )(*args)


def _merge_kernel(*refs, n_x):
    x_refs = refs[:n_x]
    mod_ref, g1_ref, wg_ref, attn_ref, of_ref = refs[n_x:n_x + 5]
    ob_refs = refs[n_x + 5:n_x + 5 + SUB_TILES]
    gain_ref, wa_ref, wl_ref, wo_ref, o_ref = refs[n_x + 5 + SUB_TILES:]
    gain = gain_ref[...]
    for t in range(SUB_TILES):
        rows = slice(t * TM, (t + 1) * TM)
        mod = _tile_mod(mod_ref, t)
        x = _tile_input(x_refs, t)
        h = _norm_modulate(x, g1_ref[...], mod[:, 0:D_MODEL], mod[:, D_MODEL:2 * D_MODEL]).astype(BF16)
        parts = []
        for hh in range(GLA_HEADS):
            vc = slice(hh * GLA_DV, (hh + 1) * GLA_DV)
            o = of_ref[rows, vc].astype(F32) + ob_refs[t][:, vc].astype(F32)
            ms = jnp.mean(o * o, axis=-1, keepdims=True)
            on = o * lax.rsqrt(ms + EPS) * gain
            gr = _dot(h, wg_ref[:, vc])
            parts.append((on * (gr * jax.nn.sigmoid(gr))).astype(BF16))
        gla = jnp.concatenate(parts, axis=1)
        gate_a = jax.nn.sigmoid(_dot(h, wg_ref[:, GLA_V:GLA_V + D_MODEL]))
        gate_b = jax.nn.sigmoid(_dot(h, wg_ref[:, GLA_V + D_MODEL:GLA_V + 2 * D_MODEL]))
        y = gate_a * _dot(attn_ref[rows, :], wa_ref[...]) + gate_b * _dot(gla, wl_ref[...])
        gt1 = mod[:, 2 * D_MODEL:3 * D_MODEL]
        o_ref[rows, :] = x + gt1 * _dot(y.astype(BF16), wo_ref[...])


def _merge_out(layer, xs, mod, g1, w_g, attn, o_f, o_b, gain, wa, wl, wo):
    t_tot = sum(a.shape[0] for a in xs)
    nt = t_tot // TM
    x_args = _tile_input_args(xs)
    ob_spec = lambda t: pl.BlockSpec((TM, GLA_V), lambda i: (_backward_scan_tile(SUB_TILES * i + t, nt), 0))
    return pl.pallas_call(
        functools.partial(_merge_kernel, n_x=len(x_args)),
        grid=(t_tot // (SUB_TILES * TM),),
        in_specs=_tile_input_specs(xs) + [
            _mod_pair_spec(layer), _layer_spec(g1, layer), _layer_spec(w_g, layer),
            _step_spec(ATT_Q), _step_spec(GLA_V)] + [ob_spec(t) for t in range(SUB_TILES)] + [
            _layer_spec(gain, layer), _layer_spec(wa, layer), _layer_spec(wl, layer), _layer_spec(wo, layer)],
        out_specs=_step_spec(D_MODEL),
        out_shape=jax.ShapeDtypeStruct((t_tot, D_MODEL), F32),
        compiler_params=_cparams(1),
        name="merge_out",
    )(*x_args, mod, g1, w_g, attn, o_f, *([o_b] * SUB_TILES), gain, wa, wl, wo)


FF_CHUNK = 1024


def _mlp_kernel(x_ref, mod_ref, g2_ref, w1_ref, w2_ref, o_ref, *, sub):
    for t in range(sub):
        rows = slice(t * TM, (t + 1) * TM)
        mod = _tile_mod(mod_ref, t)
        x = x_ref[rows, :]
        h = _norm_modulate(x, g2_ref[...], mod[:, 3 * D_MODEL:4 * D_MODEL], mod[:, 4 * D_MODEL:5 * D_MODEL])
        h = h.astype(BF16)
        acc = jnp.zeros((TM, D_MODEL), F32)
        for j in range(D_FF // FF_CHUNK):
            cs = slice(j * FF_CHUNK, (j + 1) * FF_CHUNK)
            u = jnp.maximum(_dot(h, w1_ref[:, cs]), 0.0)
            acc = acc + _dot((u * u).astype(BF16), w2_ref[cs, :])
        o_ref[rows, :] = x + mod[:, 5 * D_MODEL:6 * D_MODEL] * acc


def _mlp(layer, xc, mod, g2, w1, w2, latent_only):
    t_tot = xc.shape[0]
    if latent_only:
        sub = 1
        out_rows = t_tot - TM
        out_spec = pl.BlockSpec((TM, D_MODEL), lambda i: (jnp.maximum(i - 1, 0), 0))
    else:
        sub = SUB_TILES
        out_rows = t_tot
        out_spec = pl.BlockSpec((sub * TM, D_MODEL), lambda i: (i, 0))
    return pl.pallas_call(
        functools.partial(_mlp_kernel, sub=sub),
        grid=(t_tot // (sub * TM),),
        in_specs=[pl.BlockSpec((sub * TM, D_MODEL), lambda i: (i, 0)), _mod_pair_spec(layer), _layer_spec(g2, layer),
                  _layer_spec(w1, layer), _layer_spec(w2, layer)],
        out_specs=out_spec,
        out_shape=jax.ShapeDtypeStruct((out_rows, D_MODEL), F32),
        compiler_params=_cparams(1),
        name="mlp",
    )(xc, mod, g2, w1, w2)


def _rope_tables(n_lat, n_ctx):
    half = ROPE_AXIS_DIM // 2
    lane = np.arange(LANES)
    dd = lane % HEAD_DIM
    inv = (ROPE_BASE ** (-(dd % half).astype(np.float64) * 2.0 / ROPE_AXIS_DIM)).astype(np.float32)
    is_row = dd < ROPE_AXIS_DIM
    sign = np.where((dd % ROPE_AXIS_DIM) < half, -1.0, 1.0)
    rows = n_lat // GRID_W
    ang_r = np.arange(rows, dtype=np.float32).astype(np.float64)[:, None] * inv[None, :]
    ang_c = np.arange(GRID_W, dtype=np.float32).astype(np.float64)[:, None] * inv[None, :]
    m = jnp.asarray(is_row)[None, None, :]
    pick = lambda fr, fc: jnp.where(m, jnp.asarray(fr, F32)[:, None, :], jnp.asarray(fc, F32)[None, :, :])
    cos = pick(np.cos(ang_r), np.cos(ang_c)).reshape(n_lat, LANES)
    sin = pick(np.sin(ang_r) * sign, np.sin(ang_c) * sign).reshape(n_lat, LANES)
    cos = jnp.concatenate([jnp.ones((n_ctx, LANES), F32), cos], axis=0)
    sin = jnp.concatenate([jnp.zeros((n_ctx, LANES), F32), sin], axis=0)
    return cos, sin


def _tile_cumsum_matrices():
    i = np.arange(TM)[:, None]
    j = np.arange(TM)[None, :]
    return jnp.asarray(np.stack([j <= i, j >= i]).astype(np.float32) / GLA_NORMALIZER, BF16)


def kernel(x, c, ctx, c_ctx, w_mod, b_mod, g_norm1, w_in, q_gain, k_gain, sink, w_decay, b_decay,
           gla_gain, w_branch_attn, w_branch_gla, w_out, g_norm2, w_ff1, w_ff2):
    assert x.shape[0] == 1 and ctx.shape[0] == 1
    n_lat, n_ctx = x.shape[1], ctx.shape[1]
    assert n_ctx == TM and (n_ctx + n_lat) % (SUB_TILES * TM) == 0
    depth = w_mod.shape[0]

    cvec = jnp.zeros((MOD_ROWS, D_MODEL), F32).at[0].set(c[0]).at[1].set(c_ctx)
    mod = _mod_vectors(cvec, w_mod, b_mod).reshape(depth * MOD_ROWS, 1, N_MOD * D_MODEL)
    cos, sin = _rope_tables(n_lat, n_ctx)
    tri = _tile_cumsum_matrices()

    w_in_bf = _to_bf16(w_in)
    w_g = jnp.concatenate([w_in_bf[:, :, C_GR:C_GA], w_in_bf[:, :, C_MERGE:]], axis=2)
    wa = w_branch_attn.astype(BF16)
    wl = w_branch_gla.astype(BF16)
    wo = w_out.astype(BF16)
    w1 = w_ff1.astype(BF16)
    w2 = w_ff2.astype(BF16)
    wd = jnp.zeros((depth, 2, LANES, GLA_K), F32)
    wd = wd.at[:, 0, 0:GLA_RANK].set(w_decay[:, 0]).at[:, 1, GLA_RANK:2 * GLA_RANK].set(w_decay[:, 1]).astype(BF16)
    bd = b_decay.reshape(depth, 2, 1, GLA_K)
    g1 = g_norm1.reshape(depth, 1, D_MODEL)
    g2 = g_norm2.reshape(depth, 1, D_MODEL)
    qg = jnp.tile(q_gain, (1, LANES // HEAD_DIM)).reshape(depth, 1, LANES)
    kg = jnp.tile(k_gain, (1, LANES // HEAD_DIM)).reshape(depth, 1, LANES)
    gain = gla_gain.reshape(depth, 1, GLA_DV)
    sink_lanes = jnp.repeat((sink * LOG2E).reshape(depth, N_KV_HEADS, 1, GROUP), ATT_BLOCK, axis=3)
    logit_bound = (HEAD_DIM * Q_SCALE * BOUND_MARGIN) * jnp.max(jnp.abs(q_gain), axis=1) * jnp.max(jnp.abs(k_gain), axis=1)
    stab_lanes = jnp.maximum(sink_lanes, logit_bound.reshape(depth, 1, 1, 1))

    xs = (ctx[0], x[0])
    for l in range(depth):
        q, k, vt, gq, gk, gv, b_fwd, b_bwd = _in_proj(l, xs, mod, g1, w_in_bf, qg, kg, cos, sin, wd, bd, tri)
        attn = lax.cond(
            logit_bound[l] <= MAX_SAFE_BOUND,
            lambda *a: _window_attn(l, *a, stab_lanes, n_ctx),
            lambda *a: _window_attn(l, *a, None, n_ctx),
            q, k, vt, sink_lanes)
        o_f, o_b = _gla_scan(gq, gk, gv, b_fwd, b_bwd)
        xc = _merge_out(l, xs, mod, g1, w_g, attn, o_f, o_b, gain, wa, wl, wo)
        xc = _mlp(l, xc, mod, g2, w1, w2, latent_only=(l == depth - 1))
        xs = (xc,)
    return xc[None]
```

```python
import functools

import numpy as np
import jax
import jax.numpy as jnp
from jax import lax
from jax.experimental import pallas as pl
from jax.experimental.pallas import tpu as pltpu

F32 = jnp.float32
BF16 = jnp.bfloat16

D_MODEL = 1024
N_HEADS = 16
N_KV_HEADS = 4
HEAD_DIM = 64
GROUP = N_HEADS // N_KV_HEADS
ATT_BLOCK = 128
GRID_W = 64
ROPE_BASE = 10000.0
ROPE_AXIS_DIM = HEAD_DIM // 2
GLA_HEADS = 4
GLA_DK = 128
GLA_DV = 256
GLA_RANK = 16
GLA_NORMALIZER = 16.0
GLA_CHUNK = 64
D_FF = 4 * D_MODEL
N_MOD = 6
EPS = 1e-6
NEG = -1e30
LOG2E = 1.4426950408889634

ATT_Q = N_HEADS * HEAD_DIM
ATT_KV = N_KV_HEADS * HEAD_DIM
GLA_K = GLA_HEADS * GLA_DK
GLA_V = GLA_HEADS * GLA_DV

LANES = 128
TM = 256
SUB_TILES = 5
CHUNKS_PER_TILE = TM // GLA_CHUNK
MOD_ROWS = 8
VMEM_LIMIT = 56 * 1024 * 1024


def _cparams(n_axes):
    return pltpu.CompilerParams(dimension_semantics=("arbitrary",) * n_axes,
                                vmem_limit_bytes=VMEM_LIMIT)


def _dot(a, b):
    return jnp.dot(a, b, preferred_element_type=F32)


def _dot_nt(a, b):
    return lax.dot_general(a, b, (((1,), (1,)), ((), ())), preferred_element_type=F32)


def _dot_tn(a, b):
    return lax.dot_general(a, b, (((0,), (0,)), ((), ())), preferred_element_type=F32)


def _split2(x):
    x1 = x.astype(BF16)
    return x1, (x - x1.astype(F32)).astype(BF16)


def _layer_spec(arr, layer):
    rest = arr.shape[1:]
    return pl.BlockSpec((None,) + rest, lambda *_: (layer,) + (0,) * len(rest))


def _mod_pair_spec(layer):
    return pl.BlockSpec((2, 1, N_MOD * D_MODEL), lambda i: (layer * (MOD_ROWS // 2), 0, 0))


def _tile_mod(mod_ref, t):
    if t > 0:
        return mod_ref[0]
    return jnp.where(pl.program_id(0) == 0, mod_ref[1], mod_ref[0])


MOD_TN = 3072


def _mod_kernel(c_ref, w_ref, b_ref, o_ref):
    c = c_ref[...]
    s1, s2 = _split2(c * jax.nn.sigmoid(c))
    w1, w2 = _split2(w_ref[0])
    o_ref[0] = _dot(s1, w1) + _dot(s2, w1) + _dot(s1, w2) + b_ref[0]


def _mod_vectors(cvec, w_mod, b_mod):
    depth = w_mod.shape[0]
    n_out = w_mod.shape[2]
    return pl.pallas_call(
        _mod_kernel,
        grid=(depth, n_out // MOD_TN),
        in_specs=[pl.BlockSpec((MOD_ROWS, D_MODEL), lambda l, j: (0, 0)),
                  pl.BlockSpec((1, D_MODEL, MOD_TN), lambda l, j: (l, 0, j)),
                  pl.BlockSpec((1, 1, MOD_TN), lambda l, j: (l, 0, j))],
        out_specs=pl.BlockSpec((1, MOD_ROWS, MOD_TN), lambda l, j: (l, 0, j)),
        out_shape=jax.ShapeDtypeStruct((depth, MOD_ROWS, n_out), F32),
        compiler_params=_cparams(2),
        name="mod_vectors",
    )(cvec, w_mod, b_mod.reshape(depth, 1, n_out))


def _norm_modulate(x, g, shift, scale):
    ms = jnp.mean(x * x, axis=-1, keepdims=True)
    y = x * lax.rsqrt(ms + EPS) * g
    return y * (1.0 + scale) + shift


C_Q = 0
C_K = C_Q + ATT_Q
C_V = C_K + ATT_KV
C_GQ = C_V + ATT_KV
C_GK = C_GQ + GLA_K
C_GV = C_GK + GLA_K
C_GR = C_GV + GLA_V
C_GA = C_GR + GLA_V
C_END = C_GA + LANES
C_MERGE = C_GA + 2 * GLA_RANK
Q_SCALE = HEAD_DIM ** -0.5 * LOG2E


def _headnorm_rope(z, gain, cos, sin, lane):
    lo = lane < HEAD_DIM
    sq = z * z
    s_lo = jnp.sum(jnp.where(lo, sq, 0.0), axis=-1, keepdims=True)
    s_hi = jnp.sum(jnp.where(lo, 0.0, sq), axis=-1, keepdims=True)
    ms = jnp.where(lo, s_lo, s_hi) * (1.0 / HEAD_DIM)
    y = z * lax.rsqrt(ms + EPS) * gain
    half = ROPE_AXIS_DIM // 2
    nxt = pltpu.roll(y, LANES - half, 1)
    prv = pltpu.roll(y, half, 1)
    partner = jnp.where((lane % ROPE_AXIS_DIM) < half, nxt, prv)
    return y * cos + partner * sin


def _tile_input(x_refs, t):
    if len(x_refs) == 1:
        return x_refs[0][t * TM:(t + 1) * TM, :]
    if t > 0:
        return x_refs[1 + t][...]
    return jnp.where(pl.program_id(0) == 0, x_refs[0][...], x_refs[1][...])


def _tile_input_specs(xs):
    if len(xs) == 1:
        return [_step_spec(D_MODEL)]
    lat = lambda t: pl.BlockSpec((TM, D_MODEL), lambda i: (jnp.maximum(SUB_TILES * i + t - 1, 0), 0))
    return [pl.BlockSpec((TM, D_MODEL), lambda i: (0, 0))] + [lat(t) for t in range(SUB_TILES)]


def _tile_input_args(xs):
    return xs if len(xs) == 1 else (xs[0],) + (xs[1],) * SUB_TILES


def _step_spec(width):
    return pl.BlockSpec((SUB_TILES * TM, width), lambda i: (i, 0))


def _in_proj_kernel(*refs, n_x):
    x_refs = refs[:n_x]
    (mod_ref, g1_ref, w_ref, qg_ref, kg_ref, cos_ref, sin_ref, wd_ref, bd_ref, tri_ref,
     q_ref, k_ref, vt_ref, gq_ref, gk_ref, gv_ref, bf_ref, bb_ref) = refs[n_x:]
    for t in range(SUB_TILES):
        rows = slice(t * TM, (t + 1) * TM)
        _in_proj_tile(_tile_input(x_refs, t), _tile_mod(mod_ref, t), g1_ref, w_ref, qg_ref, kg_ref,
                      cos_ref[rows, :], sin_ref[rows, :], wd_ref, bd_ref, tri_ref,
                      q_ref.at[rows, :], k_ref.at[rows, :], vt_ref.at[:, rows], gq_ref.at[rows, :],
                      gk_ref.at[rows, :], gv_ref.at[rows, :], bf_ref.at[rows, :], bb_ref.at[rows, :])


def _in_proj_tile(x, mod, g1_ref, w_ref, qg_ref, kg_ref, cos, sin, wd_ref, bd_ref, tri_ref,
                  q_ref, k_ref, vt_ref, gq_ref, gk_ref, gv_ref, bf_ref, bb_ref):
    h = _norm_modulate(x, g1_ref[...], mod[:, 0:D_MODEL], mod[:, D_MODEL:2 * D_MODEL])
    h = h.astype(BF16)
    lane = lax.broadcasted_iota(jnp.int32, (TM, LANES), 1)
    lo = lane < HEAD_DIM
    qg = qg_ref[...]
    kg = kg_ref[...]
    ga = _dot(h, w_ref[:, C_GA:C_GA + LANES]).astype(BF16)
    decay_pre = [_dot(ga, wd_ref[d]) + bd_ref[d] for d in range(2)]
    for j in range(N_KV_HEADS // 2):
        w = 2 * GROUP * HEAD_DIM
        z = _dot(h, w_ref[:, C_Q + j * w:C_Q + (j + 1) * w])
        r = [_headnorm_rope(z[:, i * LANES:(i + 1) * LANES], qg, cos, sin, lane) * Q_SCALE
             for i in range(w // LANES)]
        for hh in range(GROUP):
            a = r[hh // 2]
            b = r[GROUP // 2 + hh // 2]
            if hh % 2 == 0:
                out = jnp.where(lo, a, pltpu.roll(b, HEAD_DIM, 1))
            else:
                out = jnp.where(lo, pltpu.roll(a, HEAD_DIM, 1), b)
            c0 = (j * GROUP + hh) * LANES
            q_ref[:, c0:c0 + LANES] = out.astype(BF16)
    for d, b_ref in enumerate((bf_ref, bb_ref)):
        x = decay_pre[d]
        u = jnp.exp2(jnp.abs(x) * (-LOG2E))
        g = (jnp.minimum(x, 0.0) * LOG2E - jnp.log2(1.0 + u)).astype(BF16)
        b_ref[...] = _dot(tri_ref[d], g)
    z = _dot(h, w_ref[:, C_K:C_K + ATT_KV])
    for e in range(2):
        r = _headnorm_rope(z[:, e * LANES:(e + 1) * LANES], kg, cos, sin, lane)
        k_ref[:, e * LANES:(e + 1) * LANES] = r.astype(BF16)
    z = _dot(h, w_ref[:, C_V:C_V + ATT_KV])
    vt_ref[...] = z.T.astype(BF16)
    gq_ref[...] = _dot(h, w_ref[:, C_GQ:C_GQ + GLA_K]).astype(BF16)
    gk_ref[...] = _dot(h, w_ref[:, C_GK:C_GK + GLA_K]).astype(BF16)
    for j in range(GLA_V // 512):
        gv_ref[:, j * 512:(j + 1) * 512] = _dot(h, w_ref[:, C_GV + j * 512:C_GV + (j + 1) * 512]).astype(BF16)


def _in_proj(layer, xs, mod, g1, w_in_bf, qg, kg, cos, sin, wd, bd, tri):
    t_tot = sum(a.shape[0] for a in xs)
    depth, d_model, _ = w_in_bf.shape
    x_args = _tile_input_args(xs)
    return pl.pallas_call(
        functools.partial(_in_proj_kernel, n_x=len(x_args)),
        grid=(t_tot // (SUB_TILES * TM),),
        in_specs=_tile_input_specs(xs) + [
            _mod_pair_spec(layer), _layer_spec(g1, layer),
            pl.BlockSpec((None, d_model, C_END), lambda i: (layer, 0, 0)),
            _layer_spec(qg, layer), _layer_spec(kg, layer), _step_spec(LANES), _step_spec(LANES),
            _layer_spec(wd, layer), _layer_spec(bd, layer), pl.BlockSpec(tri.shape, lambda i: (0, 0, 0))],
        out_specs=[_step_spec(ATT_Q), _step_spec(ATT_KV), pl.BlockSpec((ATT_KV, SUB_TILES * TM), lambda i: (0, i)),
                   _step_spec(GLA_K), _step_spec(GLA_K), _step_spec(GLA_V), _step_spec(GLA_K), _step_spec(GLA_K)],
        out_shape=[jax.ShapeDtypeStruct((t_tot, ATT_Q), BF16),
                   jax.ShapeDtypeStruct((t_tot, ATT_KV), BF16),
                   jax.ShapeDtypeStruct((ATT_KV, t_tot), BF16),
                   jax.ShapeDtypeStruct((t_tot, GLA_K), BF16),
                   jax.ShapeDtypeStruct((t_tot, GLA_K), BF16),
                   jax.ShapeDtypeStruct((t_tot, GLA_V), BF16),
                   jax.ShapeDtypeStruct((t_tot, GLA_K), F32),
                   jax.ShapeDtypeStruct((t_tot, GLA_K), F32)],
        compiler_params=_cparams(1),
        name="in_proj",
    )(*x_args, mod, g1, w_in_bf, qg, kg, cos, sin, wd, bd, tri)


ONES_ROWS = 16
BOUND_MARGIN = 1.01
MAX_SAFE_BOUND = 60.0


ATT_STEP = SUB_TILES * TM


def _attn_blocks(q_ref, sink_ref, stab_ref, o_ref, key_lists):
    lane_q = lax.broadcasted_iota(jnp.int32, (ATT_BLOCK, LANES), 1)

    def scores(blk, g):
        j, e = divmod(g, 2)
        cs = slice(j * LANES, (j + 1) * LANES)
        qrows = slice(blk * ATT_BLOCK, (blk + 1) * ATT_BLOCK)
        half = (lane_q < HEAD_DIM) if e == 0 else (lane_q >= HEAD_DIM)
        qs = []
        for hh in range(GROUP):
            qc = q_ref[qrows, (j * GROUP + hh) * LANES:(j * GROUP + hh + 1) * LANES]
            qs.append(jnp.where(half, qc, jnp.zeros_like(qc)))
        qg = jnp.concatenate(qs, axis=0)
        keys = key_lists[blk]
        k_all = jnp.concatenate([kr[r0:r0 + ATT_BLOCK, cs] for kr, _, r0, _ in keys], axis=0)
        s = _dot_nt(k_all, qg)
        parts = []
        for i, (_, _, _, mk) in enumerate(keys):
            sp = s[i * ATT_BLOCK:(i + 1) * ATT_BLOCK]
            parts.append(sp if mk is None else jnp.where(mk, sp, NEG))
        return parts

    def finish(blk, g, parts):
        keys = key_lists[blk]
        sk = sink_ref[g]
        if stab_ref is None:
            tile_max = parts[0]
            for sp in parts[1:]:
                tile_max = jnp.maximum(tile_max, sp)
            m = jnp.maximum(sk, jnp.max(tile_max, axis=0, keepdims=True))
        else:
            m = stab_ref[g]
        p = jnp.concatenate([jnp.exp2(sp - m) for sp in parts], axis=0).astype(BF16)
        rs = slice(g * HEAD_DIM, (g + 1) * HEAD_DIM)
        vt_all = jnp.concatenate([vr[rs, r0:r0 + ATT_BLOCK] for _, vr, r0, _ in keys], axis=1)
        ones = jnp.ones((ONES_ROWS, len(keys) * ATT_BLOCK), BF16)
        ot = _dot(jnp.concatenate([vt_all, ones], axis=0), p)
        denom = ot[HEAD_DIM:HEAD_DIM + 1] + jnp.exp2(sk - m)
        ot = ot[0:HEAD_DIM] * (1.0 / denom)
        qrows = slice(blk * ATT_BLOCK, (blk + 1) * ATT_BLOCK)
        for pair in range(GROUP // 2):
            x2 = jnp.concatenate([ot[:, (2 * pair) * ATT_BLOCK:(2 * pair + 1) * ATT_BLOCK],
                                  ot[:, (2 * pair + 1) * ATT_BLOCK:(2 * pair + 2) * ATT_BLOCK]], axis=0)
            c0 = g * GROUP * HEAD_DIM + pair * LANES
            o_ref[qrows, c0:c0 + LANES] = x2.T.astype(BF16)

    units = [(blk, g) for blk in range(len(key_lists)) for g in range(N_KV_HEADS)]
    pending = scores(*units[0])
    for i, unit in enumerate(units):
        upcoming = scores(*units[i + 1]) if i + 1 < len(units) else None
        finish(*unit, pending)
        pending = upcoming


def _attn_kernel(q_ref, kp_ref, kc_ref, kn_ref, kx_ref, vp_ref, vc_ref, vn_ref, vx_ref, sink_ref,
                 *rest, n_steps, ctx_blocks):
    stab_ref, o_ref = rest if len(rest) == 2 else (None, rest[0])
    s = pl.program_id(0)
    n_ctx = kx_ref.shape[0]
    ctx_keys = [(kx_ref, vx_ref, r0, None) for r0 in range(0, n_ctx, ATT_BLOCK)]
    gw = GROUP * ATT_BLOCK
    key_i = lax.broadcasted_iota(jnp.int32, (ATT_BLOCK, gw), 0)
    qry_i = lax.broadcasted_iota(jnp.int32, (ATT_BLOCK, gw), 1) % ATT_BLOCK
    before = key_i >= qry_i
    after = key_i <= qry_i
    everything = key_i >= 0
    past_first = s > 0
    before_last = s < n_steps - 1
    blocks_per_step = ATT_STEP // ATT_BLOCK
    key_lists = []
    for b in range(blocks_per_step):
        prev = (kc_ref, vc_ref, (b - 1) * ATT_BLOCK) if b > 0 else (kp_ref, vp_ref, 0)
        cur = (kc_ref, vc_ref, b * ATT_BLOCK)
        nxt = (kc_ref, vc_ref, (b + 1) * ATT_BLOCK) if b + 1 < blocks_per_step else (kn_ref, vn_ref, 0)
        m_prev = jnp.logical_and(before, past_first) if b <= ctx_blocks else before
        m_cur = jnp.logical_and(everything, past_first) if b < ctx_blocks else None
        m_next = jnp.logical_and(after, past_first) if b < ctx_blocks else after
        if b + 1 == blocks_per_step:
            m_next = jnp.logical_and(after, before_last)
        key_lists.append([prev + (m_prev,), cur + (m_cur,), nxt + (m_next,)] + ctx_keys)
    _attn_blocks(q_ref, sink_ref, stab_ref, o_ref, key_lists)


def _window_attn(layer, q, k, vt, sink_lanes, stab_lanes, n_ctx):
    t_tot = q.shape[0]
    stab = [] if stab_lanes is None else [stab_lanes]
    assert t_tot % ATT_STEP == 0
    ns = t_tot // ATT_STEP
    nb = t_tot // ATT_BLOCK
    cb = n_ctx // ATT_BLOCK
    assert cb + 1 < ATT_STEP // ATT_BLOCK
    per = ATT_STEP // ATT_BLOCK
    prev_i = lambda s: jnp.clip(per * s - 1, 0, nb - 1)
    next_i = lambda s: jnp.clip(per * s + per, 0, nb - 1)
    ident = lambda s: s
    kspec = lambda rows, f: pl.BlockSpec((rows, ATT_KV), lambda s: (f(s), 0))
    vspec = lambda rows, f: pl.BlockSpec((ATT_KV, rows), lambda s: (0, f(s)))
    return pl.pallas_call(
        functools.partial(_attn_kernel, n_steps=ns, ctx_blocks=cb),
        grid=(ns,),
        in_specs=[pl.BlockSpec((ATT_STEP, ATT_Q), lambda s: (s, 0)),
                  kspec(ATT_BLOCK, prev_i), kspec(ATT_STEP, ident), kspec(ATT_BLOCK, next_i),
                  pl.BlockSpec((n_ctx, ATT_KV), lambda s: (0, 0)),
                  vspec(ATT_BLOCK, prev_i), vspec(ATT_STEP, ident), vspec(ATT_BLOCK, next_i),
                  pl.BlockSpec((ATT_KV, n_ctx), lambda s: (0, 0)),
                  _layer_spec(sink_lanes, layer)] + [_layer_spec(a, layer) for a in stab],
        out_specs=pl.BlockSpec((ATT_STEP, ATT_Q), lambda s: (s, 0)),
        out_shape=jax.ShapeDtypeStruct((t_tot, ATT_Q), BF16),
        compiler_params=_cparams(1),
        name="window_attn" if stab_lanes is None else "window_attn_bounded",
    )(q, k, k, k, k, vt, vt, vt, vt, sink_lanes, *stab)


def _gla_direction(d, gq_ref, gk_ref, gv_ref, b_ref, o_ref, s_ref):
    nc = CHUNKS_PER_TILE
    rows = lambda c: slice(c * GLA_CHUNK, (c + 1) * GLA_CHUNK)
    bcum = b_ref[...]
    zero_row = jnp.zeros((1, GLA_K), F32)
    if d == 0:
        earlier = lambda c: range(0, c)
        b_start = lambda c: zero_row if c == 0 else bcum[c * GLA_CHUNK - 1:c * GLA_CHUNK]
        b_end = lambda c: bcum[(c + 1) * GLA_CHUNK - 1:(c + 1) * GLA_CHUNK]
        last_chunk = nc - 1
    else:
        earlier = lambda c: range(c + 1, nc)
        b_start = lambda c: zero_row if c == nc - 1 else bcum[(c + 1) * GLA_CHUNK:(c + 1) * GLA_CHUNK + 1]
        b_end = lambda c: bcum[c * GLA_CHUNK:c * GLA_CHUNK + 1]
        last_chunk = 0
    total = b_end(last_chunk)
    gq = gq_ref[...].astype(F32) * (GLA_DK ** -0.5)
    gk = gk_ref[...].astype(F32)
    q_loc, q_tile, k_in, k_out, k_raw = [], [], [], [], []
    for c in range(nc):
        b_loc = bcum[rows(c)] - b_start(c)
        ql = gq[rows(c)] * jnp.exp2(b_loc)
        q_loc.append(ql.astype(BF16))
        q_tile.append((ql * jnp.exp2(b_start(c))).astype(BF16))
        k_in.append((gk[rows(c)] * jnp.exp2(-b_loc)).astype(BF16))
        k_out.append(gk[rows(c)] * jnp.exp2(b_end(c) - bcum[rows(c)]))
        k_raw.append(gk_ref[rows(c), :])
    k_out_bf = [k.astype(BF16) for k in k_out]

    def key_seen_from(c, src):
        if src == c:
            return k_in[c]
        if src not in earlier(c):
            return k_raw[src]
        if abs(src - c) == 1:
            return k_out_bf[src]
        return (k_out[src] * jnp.exp2(b_start(c) - b_end(src))).astype(BF16)

    k_dst = [jnp.concatenate([key_seen_from(c, src) for src in range(nc)], axis=0) for c in range(nc)]
    k_state = jnp.concatenate(
        [k_out_bf[c] if c == last_chunk else (k_out[c] * jnp.exp2(total - b_end(c))).astype(BF16)
         for c in range(nc)], axis=0)
    q_tile = jnp.concatenate(q_tile, axis=0)
    ti = lax.broadcasted_iota(jnp.int32, (TM, TM), 0)
    tj = lax.broadcasted_iota(jnp.int32, (TM, TM), 1)
    causal = (tj <= ti) if d == 0 else (tj >= ti)
    decay = jnp.exp2(total)

    for hh in range(GLA_HEADS):
        kc = slice(hh * GLA_DK, (hh + 1) * GLA_DK)
        vc = slice(hh * GLA_DV, (hh + 1) * GLA_DV)
        a = jnp.concatenate([_dot_nt(q_loc[c][:, kc], k_dst[c][:, kc]) for c in range(nc)], axis=0)
        a = jnp.where(causal, a, 0.0).astype(BF16)
        v = gv_ref[:, vc]
        s_old = s_ref[hh]
        o_ref[:, vc] = (_dot(a, v) + _dot(q_tile[:, kc], s_old.astype(BF16))).astype(BF16)
        decay_col = jnp.broadcast_to(decay[:, kc], (GLA_DK, GLA_DK)).T
        decay_col = jnp.concatenate([decay_col] * (GLA_DV // GLA_DK), axis=1)
        s_ref[hh] = s_old * decay_col + _dot_tn(k_state[:, kc], v)


def _backward_scan_tile(pos, n_tiles):
    return jnp.where(pos == 0, 0, n_tiles - pos)


def _gla_kernel(*refs):
    gqf, gkf, gvf, bf = refs[0:4]
    bwd_in = refs[4:4 + 4 * SUB_TILES]
    of_ref, ob_ref, sf_ref, sb_ref = refs[4 + 4 * SUB_TILES:]

    @pl.when(pl.program_id(0) == 0)
    def _():
        sf_ref[...] = jnp.zeros_like(sf_ref)
        sb_ref[...] = jnp.zeros_like(sb_ref)

    for t in range(SUB_TILES):
        rows = slice(t * TM, (t + 1) * TM)
        _gla_direction(0, gqf.at[rows, :], gkf.at[rows, :], gvf.at[rows, :], bf.at[rows, :],
                       of_ref.at[rows, :], sf_ref)
        gqb, gkb, gvb, bb = bwd_in[4 * t:4 * t + 4]
        _gla_direction(1, gqb, gkb, gvb, bb, ob_ref.at[rows, :], sb_ref)


def _gla_scan(gq, gk, gv, b_fwd, b_bwd):
    t_tot = gq.shape[0]
    nt = t_tot // TM
    bwd_spec = lambda w, t: pl.BlockSpec(
        (TM, w), lambda s: (_backward_scan_tile(SUB_TILES * s + t, nt), 0))
    ins = [_step_spec(GLA_K), _step_spec(GLA_K), _step_spec(GLA_V), _step_spec(GLA_K)]
    args = [gq, gk, gv, b_fwd]
    for t in range(SUB_TILES):
        ins += [bwd_spec(GLA_K, t), bwd_spec(GLA_K, t), bwd_spec(GLA_V, t), bwd_spec(GLA_K, t)]
        args += [gq, gk, gv, b_bwd]
    return pl.pallas_call(
        _gla_kernel,
        grid=(t_tot // (SUB_TILES * TM),),
        in_specs=ins,
        out_specs=[_step_spec(GLA_V), _step_spec(GLA_V)],
        out_shape=[jax.ShapeDtypeStruct((t_tot, GLA_V), BF16)] * 2,
        scratch_shapes=[pltpu.VMEM((GLA_HEADS, GLA_DK, GLA_DV), F32)] * 2,
        compiler_params=_cparams(1),
        name="gla_scan",
    )(*args)


def _merge_kernel(*refs, n_x):
    x_refs = refs[:n_x]
    mod_ref, g1_ref, wg_ref, attn_ref, of_ref = refs[n_x:n_x + 5]
    ob_refs = refs[n_x + 5:n_x + 5 + SUB_TILES]
    gain_ref, wa_ref, wl_ref, wo_ref, o_ref = refs[n_x + 5 + SUB_TILES:]
    gain = gain_ref[...]
    for t in range(SUB_TILES):
        rows = slice(t * TM, (t + 1) * TM)
        mod = _tile_mod(mod_ref, t)
        x = _tile_input(x_refs, t)
        h = _norm_modulate(x, g1_ref[...], mod[:, 0:D_MODEL], mod[:, D_MODEL:2 * D_MODEL]).astype(BF16)
        parts = []
        for hh in range(GLA_HEADS):
            vc = slice(hh * GLA_DV, (hh + 1) * GLA_DV)
            o = of_ref[rows, vc].astype(F32) + ob_refs[t][:, vc].astype(F32)
            ms = jnp.mean(o * o, axis=-1, keepdims=True)
            on = o * lax.rsqrt(ms + EPS) * gain
            gr = _dot(h, wg_ref[:, vc])
            parts.append((on * (gr * jax.nn.sigmoid(gr))).astype(BF16))
        gla = jnp.concatenate(parts, axis=1)
        gate_a = jax.nn.sigmoid(_dot(h, wg_ref[:, GLA_V:GLA_V + D_MODEL]))
        gate_b = jax.nn.sigmoid(_dot(h, wg_ref[:, GLA_V + D_MODEL:GLA_V + 2 * D_MODEL]))
        y = gate_a * _dot(attn_ref[rows, :], wa_ref[...]) + gate_b * _dot(gla, wl_ref[...])
        gt1 = mod[:, 2 * D_MODEL:3 * D_MODEL]
        o_ref[rows, :] = x + gt1 * _dot(y.astype(BF16), wo_ref[...])


def _merge_out(layer, xs, mod, g1, w_g, attn, o_f, o_b, gain, wa, wl, wo):
    t_tot = sum(a.shape[0] for a in xs)
    nt = t_tot // TM
    x_args = _tile_input_args(xs)
    ob_spec = lambda t: pl.BlockSpec((TM, GLA_V), lambda i: (_backward_scan_tile(SUB_TILES * i + t, nt), 0))
    return pl.pallas_call(
        functools.partial(_merge_kernel, n_x=len(x_args)),
        grid=(t_tot // (SUB_TILES * TM),),
        in_specs=_tile_input_specs(xs) + [
            _mod_pair_spec(layer), _layer_spec(g1, layer), _layer_spec(w_g, layer),
            _step_spec(ATT_Q), _step_spec(GLA_V)] + [ob_spec(t) for t in range(SUB_TILES)] + [
            _layer_spec(gain, layer), _layer_spec(wa, layer), _layer_spec(wl, layer), _layer_spec(wo, layer)],
        out_specs=_step_spec(D_MODEL),
        out_shape=jax.ShapeDtypeStruct((t_tot, D_MODEL), F32),
        compiler_params=_cparams(1),
        name="merge_out",
    )(*x_args, mod, g1, w_g, attn, o_f, *([o_b] * SUB_TILES), gain, wa, wl, wo)


FF_CHUNK = 1024


def _mlp_kernel(x_ref, mod_ref, g2_ref, w1_ref, w2_ref, o_ref, *, sub):
    for t in range(sub):
        rows = slice(t * TM, (t + 1) * TM)
        mod = _tile_mod(mod_ref, t)
        x = x_ref[rows, :]
        h = _norm_modulate(x, g2_ref[...], mod[:, 3 * D_MODEL:4 * D_MODEL], mod[:, 4 * D_MODEL:5 * D_MODEL])
        h = h.astype(BF16)
        acc = jnp.zeros((TM, D_MODEL), F32)
        for j in range(D_FF // FF_CHUNK):
            cs = slice(j * FF_CHUNK, (j + 1) * FF_CHUNK)
            u = jnp.maximum(_dot(h, w1_ref[:, cs]), 0.0)
            acc = acc + _dot((u * u).astype(BF16), w2_ref[cs, :])
        o_ref[rows, :] = x + mod[:, 5 * D_MODEL:6 * D_MODEL] * acc


def _mlp(layer, xc, mod, g2, w1, w2, latent_only):
    t_tot = xc.shape[0]
    if latent_only:
        sub = 1
        out_rows = t_tot - TM
        out_spec = pl.BlockSpec((TM, D_MODEL), lambda i: (jnp.maximum(i - 1, 0), 0))
    else:
        sub = SUB_TILES
        out_rows = t_tot
        out_spec = pl.BlockSpec((sub * TM, D_MODEL), lambda i: (i, 0))
    return pl.pallas_call(
        functools.partial(_mlp_kernel, sub=sub),
        grid=(t_tot // (sub * TM),),
        in_specs=[pl.BlockSpec((sub * TM, D_MODEL), lambda i: (i, 0)), _mod_pair_spec(layer), _layer_spec(g2, layer),
                  _layer_spec(w1, layer), _layer_spec(w2, layer)],
        out_specs=out_spec,
        out_shape=jax.ShapeDtypeStruct((out_rows, D_MODEL), F32),
        compiler_params=_cparams(1),
        name="mlp",
    )(xc, mod, g2, w1, w2)


def _rope_tables(n_lat, n_ctx):
    half = ROPE_AXIS_DIM // 2
    lane = np.arange(LANES)
    dd = lane % HEAD_DIM
    inv = (ROPE_BASE ** (-(dd % half).astype(np.float64) * 2.0 / ROPE_AXIS_DIM)).astype(np.float32)
    is_row = dd < ROPE_AXIS_DIM
    sign = np.where((dd % ROPE_AXIS_DIM) < half, -1.0, 1.0)
    rows = n_lat // GRID_W
    ang_r = np.arange(rows, dtype=np.float32).astype(np.float64)[:, None] * inv[None, :]
    ang_c = np.arange(GRID_W, dtype=np.float32).astype(np.float64)[:, None] * inv[None, :]
    m = jnp.asarray(is_row)[None, None, :]
    pick = lambda fr, fc: jnp.where(m, jnp.asarray(fr, F32)[:, None, :], jnp.asarray(fc, F32)[None, :, :])
    cos = pick(np.cos(ang_r), np.cos(ang_c)).reshape(n_lat, LANES)
    sin = pick(np.sin(ang_r) * sign, np.sin(ang_c) * sign).reshape(n_lat, LANES)
    cos = jnp.concatenate([jnp.ones((n_ctx, LANES), F32), cos], axis=0)
    sin = jnp.concatenate([jnp.zeros((n_ctx, LANES), F32), sin], axis=0)
    return cos, sin


def _tile_cumsum_matrices():
    i = np.arange(TM)[:, None]
    j = np.arange(TM)[None, :]
    return jnp.asarray(np.stack([j <= i, j >= i]).astype(np.float32) / GLA_NORMALIZER, BF16)


def kernel(x, c, ctx, c_ctx, w_mod, b_mod, g_norm1, w_in, q_gain, k_gain, sink, w_decay, b_decay,
           gla_gain, w_branch_attn, w_branch_gla, w_out, g_norm2, w_ff1, w_ff2):
    assert x.shape[0] == 1 and ctx.shape[0] == 1
    n_lat, n_ctx = x.shape[1], ctx.shape[1]
    assert n_ctx == TM and (n_ctx + n_lat) % (SUB_TILES * TM) == 0
    depth = w_mod.shape[0]

    cvec = jnp.zeros((MOD_ROWS, D_MODEL), F32).at[0].set(c[0]).at[1].set(c_ctx)
    mod = _mod_vectors(cvec, w_mod, b_mod).reshape(depth * MOD_ROWS, 1, N_MOD * D_MODEL)
    cos, sin = _rope_tables(n_lat, n_ctx)
    tri = _tile_cumsum_matrices()

    w_in_bf = w_in.astype(BF16)
    w_g = jnp.concatenate([w_in_bf[:, :, C_GR:C_GA], w_in_bf[:, :, C_MERGE:]], axis=2)
    wa = w_branch_attn.astype(BF16)
    wl = w_branch_gla.astype(BF16)
    wo = w_out.astype(BF16)
    w1 = w_ff1.astype(BF16)
    w2 = w_ff2.astype(BF16)
    wd = jnp.zeros((depth, 2, LANES, GLA_K), F32)
    wd = wd.at[:, 0, 0:GLA_RANK].set(w_decay[:, 0]).at[:, 1, GLA_RANK:2 * GLA_RANK].set(w_decay[:, 1]).astype(BF16)
    bd = b_decay.reshape(depth, 2, 1, GLA_K)
    g1 = g_norm1.reshape(depth, 1, D_MODEL)
    g2 = g_norm2.reshape(depth, 1, D_MODEL)
    qg = jnp.tile(q_gain, (1, LANES // HEAD_DIM)).reshape(depth, 1, LANES)
    kg = jnp.tile(k_gain, (1, LANES // HEAD_DIM)).reshape(depth, 1, LANES)
    gain = gla_gain.reshape(depth, 1, GLA_DV)
    sink_lanes = jnp.repeat((sink * LOG2E).reshape(depth, N_KV_HEADS, 1, GROUP), ATT_BLOCK, axis=3)
    logit_bound = (HEAD_DIM * Q_SCALE * BOUND_MARGIN) * jnp.max(jnp.abs(q_gain), axis=1) * jnp.max(jnp.abs(k_gain), axis=1)
    stab_lanes = jnp.maximum(sink_lanes, logit_bound.reshape(depth, 1, 1, 1))

    xs = (ctx[0], x[0])
    for l in range(depth):
        q, k, vt, gq, gk, gv, b_fwd, b_bwd = _in_proj(l, xs, mod, g1, w_in_bf, qg, kg, cos, sin, wd, bd, tri)
        attn = lax.cond(
            logit_bound[l] <= MAX_SAFE_BOUND,
            lambda *a: _window_attn(l, *a, stab_lanes, n_ctx),
            lambda *a: _window_attn(l, *a, None, n_ctx),
            q, k, vt, sink_lanes)
        o_f, o_b = _gla_scan(gq, gk, gv, b_fwd, b_bwd)
        xc = _merge_out(l, xs, mod, g1, w_g, attn, o_f, o_b, gain, wa, wl, wo)
        xc = _mlp(l, xc, mod, g2, w1, w2, latent_only=(l == depth - 1))
        xs = (xc,)
    return xc[None]
```

```python
import functools

import numpy as np
import jax
import jax.numpy as jnp
from jax import lax
from jax.experimental import pallas as pl
from jax.experimental.pallas import tpu as pltpu

F32 = jnp.float32
BF16 = jnp.bfloat16

D_MODEL = 1024
N_HEADS = 16
N_KV_HEADS = 4
HEAD_DIM = 64
GROUP = N_HEADS // N_KV_HEADS
ATT_BLOCK = 128
GRID_W = 64
ROPE_BASE = 10000.0
ROPE_AXIS_DIM = HEAD_DIM // 2
GLA_HEADS = 4
GLA_DK = 128
GLA_DV = 256
GLA_RANK = 16
GLA_NORMALIZER = 16.0
GLA_CHUNK = 64
D_FF = 4 * D_MODEL
N_MOD = 6
EPS = 1e-6
NEG = -1e30
LOG2E = 1.4426950408889634

ATT_Q = N_HEADS * HEAD_DIM
ATT_KV = N_KV_HEADS * HEAD_DIM
GLA_K = GLA_HEADS * GLA_DK
GLA_V = GLA_HEADS * GLA_DV

LANES = 128
TM = 256
SUB_TILES = 5
CHUNKS_PER_TILE = TM // GLA_CHUNK
MOD_ROWS = 8
VMEM_LIMIT = 56 * 1024 * 1024


def _cparams(n_axes):
    return pltpu.CompilerParams(dimension_semantics=("arbitrary",) * n_axes,
                                vmem_limit_bytes=VMEM_LIMIT)


def _dot(a, b):
    return jnp.dot(a, b, preferred_element_type=F32)


def _dot_nt(a, b):
    return lax.dot_general(a, b, (((1,), (1,)), ((), ())), preferred_element_type=F32)


def _dot_tn(a, b):
    return lax.dot_general(a, b, (((0,), (0,)), ((), ())), preferred_element_type=F32)


def _split2(x):
    x1 = x.astype(BF16)
    return x1, (x - x1.astype(F32)).astype(BF16)


def _layer_spec(arr, layer):
    rest = arr.shape[1:]
    return pl.BlockSpec((None,) + rest, lambda *_: (layer,) + (0,) * len(rest))


def _mod_pair_spec(layer):
    return pl.BlockSpec((2, 1, N_MOD * D_MODEL), lambda i: (layer * (MOD_ROWS // 2), 0, 0))


def _tile_mod(mod_ref, t):
    if t > 0:
        return mod_ref[0]
    return jnp.where(pl.program_id(0) == 0, mod_ref[1], mod_ref[0])


MOD_TN = 3072


def _mod_kernel(c_ref, w_ref, b_ref, o_ref):
    c = c_ref[...]
    s1, s2 = _split2(c * jax.nn.sigmoid(c))
    w1, w2 = _split2(w_ref[0])
    o_ref[0] = _dot(s1, w1) + _dot(s2, w1) + _dot(s1, w2) + b_ref[0]


def _mod_vectors(cvec, w_mod, b_mod):
    depth = w_mod.shape[0]
    n_out = w_mod.shape[2]
    return pl.pallas_call(
        _mod_kernel,
        grid=(depth, n_out // MOD_TN),
        in_specs=[pl.BlockSpec((MOD_ROWS, D_MODEL), lambda l, j: (0, 0)),
                  pl.BlockSpec((1, D_MODEL, MOD_TN), lambda l, j: (l, 0, j)),
                  pl.BlockSpec((1, 1, MOD_TN), lambda l, j: (l, 0, j))],
        out_specs=pl.BlockSpec((1, MOD_ROWS, MOD_TN), lambda l, j: (l, 0, j)),
        out_shape=jax.ShapeDtypeStruct((depth, MOD_ROWS, n_out), F32),
        compiler_params=_cparams(2),
        name="mod_vectors",
    )(cvec, w_mod, b_mod.reshape(depth, 1, n_out))


def _norm_modulate(x, g, shift, scale):
    ms = jnp.mean(x * x, axis=-1, keepdims=True)
    y = x * lax.rsqrt(ms + EPS) * g
    return y * (1.0 + scale) + shift


C_Q = 0
C_K = C_Q + ATT_Q
C_V = C_K + ATT_KV
C_GQ = C_V + ATT_KV
C_GK = C_GQ + GLA_K
C_GV = C_GK + GLA_K
C_GR = C_GV + GLA_V
C_GA = C_GR + GLA_V
C_END = C_GA + LANES
C_MERGE = C_GA + 2 * GLA_RANK
Q_SCALE = HEAD_DIM ** -0.5 * LOG2E


def _headnorm_rope(z, gain, cos, sin, lane):
    lo = lane < HEAD_DIM
    sq = z * z
    s_lo = jnp.sum(jnp.where(lo, sq, 0.0), axis=-1, keepdims=True)
    s_hi = jnp.sum(jnp.where(lo, 0.0, sq), axis=-1, keepdims=True)
    ms = jnp.where(lo, s_lo, s_hi) * (1.0 / HEAD_DIM)
    y = z * lax.rsqrt(ms + EPS) * gain
    half = ROPE_AXIS_DIM // 2
    nxt = pltpu.roll(y, LANES - half, 1)
    prv = pltpu.roll(y, half, 1)
    partner = jnp.where((lane % ROPE_AXIS_DIM) < half, nxt, prv)
    return y * cos + partner * sin


def _tile_input(x_refs, t):
    if len(x_refs) == 1:
        return x_refs[0][t * TM:(t + 1) * TM, :]
    if t > 0:
        return x_refs[1 + t][...]
    return jnp.where(pl.program_id(0) == 0, x_refs[0][...], x_refs[1][...])


def _tile_input_specs(xs):
    if len(xs) == 1:
        return [_step_spec(D_MODEL)]
    lat = lambda t: pl.BlockSpec((TM, D_MODEL), lambda i: (jnp.maximum(SUB_TILES * i + t - 1, 0), 0))
    return [pl.BlockSpec((TM, D_MODEL), lambda i: (0, 0))] + [lat(t) for t in range(SUB_TILES)]


def _tile_input_args(xs):
    return xs if len(xs) == 1 else (xs[0],) + (xs[1],) * SUB_TILES


def _step_spec(width):
    return pl.BlockSpec((SUB_TILES * TM, width), lambda i: (i, 0))


def _in_proj_kernel(*refs, n_x):
    x_refs = refs[:n_x]
    (mod_ref, g1_ref, w_ref, qg_ref, kg_ref, cos_ref, sin_ref, wd_ref, bd_ref, tri_ref,
     q_ref, k_ref, vt_ref, gq_ref, gk_ref, gv_ref, bf_ref, bb_ref) = refs[n_x:]
    for t in range(SUB_TILES):
        rows = slice(t * TM, (t + 1) * TM)
        _in_proj_tile(_tile_input(x_refs, t), _tile_mod(mod_ref, t), g1_ref, w_ref, qg_ref, kg_ref,
                      cos_ref[rows, :], sin_ref[rows, :], wd_ref, bd_ref, tri_ref,
                      q_ref.at[rows, :], k_ref.at[rows, :], vt_ref.at[:, rows], gq_ref.at[rows, :],
                      gk_ref.at[rows, :], gv_ref.at[rows, :], bf_ref.at[rows, :], bb_ref.at[rows, :])


def _in_proj_tile(x, mod, g1_ref, w_ref, qg_ref, kg_ref, cos, sin, wd_ref, bd_ref, tri_ref,
                  q_ref, k_ref, vt_ref, gq_ref, gk_ref, gv_ref, bf_ref, bb_ref):
    h = _norm_modulate(x, g1_ref[...], mod[:, 0:D_MODEL], mod[:, D_MODEL:2 * D_MODEL])
    h = h.astype(BF16)
    lane = lax.broadcasted_iota(jnp.int32, (TM, LANES), 1)
    lo = lane < HEAD_DIM
    qg = qg_ref[...]
    kg = kg_ref[...]
    ga = _dot(h, w_ref[:, C_GA:C_GA + LANES]).astype(BF16)
    decay_pre = [_dot(ga, wd_ref[d]) + bd_ref[d] for d in range(2)]
    for j in range(N_KV_HEADS // 2):
        w = 2 * GROUP * HEAD_DIM
        z = _dot(h, w_ref[:, C_Q + j * w:C_Q + (j + 1) * w])
        r = [_headnorm_rope(z[:, i * LANES:(i + 1) * LANES], qg, cos, sin, lane) * Q_SCALE
             for i in range(w // LANES)]
        for hh in range(GROUP):
            a = r[hh // 2]
            b = r[GROUP // 2 + hh // 2]
            if hh % 2 == 0:
                out = jnp.where(lo, a, pltpu.roll(b, HEAD_DIM, 1))
            else:
                out = jnp.where(lo, pltpu.roll(a, HEAD_DIM, 1), b)
            c0 = (j * GROUP + hh) * LANES
            q_ref[:, c0:c0 + LANES] = out.astype(BF16)
    for d, b_ref in enumerate((bf_ref, bb_ref)):
        x = decay_pre[d]
        u = jnp.exp2(jnp.abs(x) * (-LOG2E))
        g = (jnp.minimum(x, 0.0) * LOG2E - jnp.log2(1.0 + u)).astype(BF16)
        b_ref[...] = _dot(tri_ref[d], g)
    z = _dot(h, w_ref[:, C_K:C_K + ATT_KV])
    for e in range(2):
        r = _headnorm_rope(z[:, e * LANES:(e + 1) * LANES], kg, cos, sin, lane)
        k_ref[:, e * LANES:(e + 1) * LANES] = r.astype(BF16)
    z = _dot(h, w_ref[:, C_V:C_V + ATT_KV])
    vt_ref[...] = z.T.astype(BF16)
    gq_ref[...] = _dot(h, w_ref[:, C_GQ:C_GQ + GLA_K]).astype(BF16)
    gk_ref[...] = _dot(h, w_ref[:, C_GK:C_GK + GLA_K]).astype(BF16)
    for j in range(GLA_V // 512):
        gv_ref[:, j * 512:(j + 1) * 512] = _dot(h, w_ref[:, C_GV + j * 512:C_GV + (j + 1) * 512]).astype(BF16)


def _in_proj(layer, xs, mod, g1, w_in_bf, qg, kg, cos, sin, wd, bd, tri):
    t_tot = sum(a.shape[0] for a in xs)
    depth, d_model, _ = w_in_bf.shape
    x_args = _tile_input_args(xs)
    return pl.pallas_call(
        functools.partial(_in_proj_kernel, n_x=len(x_args)),
        grid=(t_tot // (SUB_TILES * TM),),
        in_specs=_tile_input_specs(xs) + [
            _mod_pair_spec(layer), _layer_spec(g1, layer),
            pl.BlockSpec((None, d_model, C_END), lambda i: (layer, 0, 0)),
            _layer_spec(qg, layer), _layer_spec(kg, layer), _step_spec(LANES), _step_spec(LANES),
            _layer_spec(wd, layer), _layer_spec(bd, layer), pl.BlockSpec(tri.shape, lambda i: (0, 0, 0))],
        out_specs=[_step_spec(ATT_Q), _step_spec(ATT_KV), pl.BlockSpec((ATT_KV, SUB_TILES * TM), lambda i: (0, i)),
                   _step_spec(GLA_K), _step_spec(GLA_K), _step_spec(GLA_V), _step_spec(GLA_K), _step_spec(GLA_K)],
        out_shape=[jax.ShapeDtypeStruct((t_tot, ATT_Q), BF16),
                   jax.ShapeDtypeStruct((t_tot, ATT_KV), BF16),
                   jax.ShapeDtypeStruct((ATT_KV, t_tot), BF16),
                   jax.ShapeDtypeStruct((t_tot, GLA_K), BF16),
                   jax.ShapeDtypeStruct((t_tot, GLA_K), BF16),
                   jax.ShapeDtypeStruct((t_tot, GLA_V), BF16),
                   jax.ShapeDtypeStruct((t_tot, GLA_K), F32),
                   jax.ShapeDtypeStruct((t_tot, GLA_K), F32)],
        compiler_params=_cparams(1),
        name="in_proj",
    )(*x_args, mod, g1, w_in_bf, qg, kg, cos, sin, wd, bd, tri)


ONES_ROWS = 16
BOUND_MARGIN = 1.01
MAX_SAFE_BOUND = 60.0


ATT_STEP = SUB_TILES * TM


def _attn_blocks(q_ref, sink_ref, stab_ref, o_ref, key_lists):
    lane_q = lax.broadcasted_iota(jnp.int32, (ATT_BLOCK, LANES), 1)

    def scores(blk, g):
        j, e = divmod(g, 2)
        cs = slice(j * LANES, (j + 1) * LANES)
        qrows = slice(blk * ATT_BLOCK, (blk + 1) * ATT_BLOCK)
        half = (lane_q < HEAD_DIM) if e == 0 else (lane_q >= HEAD_DIM)
        qs = []
        for hh in range(GROUP):
            qc = q_ref[qrows, (j * GROUP + hh) * LANES:(j * GROUP + hh + 1) * LANES]
            qs.append(jnp.where(half, qc, jnp.zeros_like(qc)))
        qg = jnp.concatenate(qs, axis=0)
        keys = key_lists[blk]
        k_all = jnp.concatenate([kr[r0:r0 + ATT_BLOCK, cs] for kr, _, r0, _ in keys], axis=0)
        s = _dot_nt(k_all, qg)
        parts = []
        for i, (_, _, _, mk) in enumerate(keys):
            sp = s[i * ATT_BLOCK:(i + 1) * ATT_BLOCK]
            parts.append(sp if mk is None else jnp.where(mk, sp, NEG))
        return parts

    def finish(blk, g, parts):
        keys = key_lists[blk]
        sk = sink_ref[g]
        if stab_ref is None:
            tile_max = parts[0]
            for sp in parts[1:]:
                tile_max = jnp.maximum(tile_max, sp)
            m = jnp.maximum(sk, jnp.max(tile_max, axis=0, keepdims=True))
        else:
            m = stab_ref[g]
        p = jnp.concatenate([jnp.exp2(sp - m) for sp in parts], axis=0).astype(BF16)
        rs = slice(g * HEAD_DIM, (g + 1) * HEAD_DIM)
        vt_all = jnp.concatenate([vr[rs, r0:r0 + ATT_BLOCK] for _, vr, r0, _ in keys], axis=1)
        ones = jnp.ones((ONES_ROWS, len(keys) * ATT_BLOCK), BF16)
        ot = _dot(jnp.concatenate([vt_all, ones], axis=0), p)
        denom = ot[HEAD_DIM:HEAD_DIM + 1] + jnp.exp2(sk - m)
        ot = ot[0:HEAD_DIM] * (1.0 / denom)
        qrows = slice(blk * ATT_BLOCK, (blk + 1) * ATT_BLOCK)
        for pair in range(GROUP // 2):
            x2 = jnp.concatenate([ot[:, (2 * pair) * ATT_BLOCK:(2 * pair + 1) * ATT_BLOCK],
                                  ot[:, (2 * pair + 1) * ATT_BLOCK:(2 * pair + 2) * ATT_BLOCK]], axis=0)
            c0 = g * GROUP * HEAD_DIM + pair * LANES
            o_ref[qrows, c0:c0 + LANES] = x2.T.astype(BF16)

    units = [(blk, g) for blk in range(len(key_lists)) for g in range(N_KV_HEADS)]
    pending = scores(*units[0])
    for i, unit in enumerate(units):
        upcoming = scores(*units[i + 1]) if i + 1 < len(units) else None
        finish(*unit, pending)
        pending = upcoming


def _attn_kernel(q_ref, kp_ref, kc_ref, kn_ref, kx_ref, vp_ref, vc_ref, vn_ref, vx_ref, sink_ref,
                 *rest, n_steps, ctx_blocks):
    stab_ref, o_ref = rest if len(rest) == 2 else (None, rest[0])
    s = pl.program_id(0)
    n_ctx = kx_ref.shape[0]
    ctx_keys = [(kx_ref, vx_ref, r0, None) for r0 in range(0, n_ctx, ATT_BLOCK)]
    gw = GROUP * ATT_BLOCK
    key_i = lax.broadcasted_iota(jnp.int32, (ATT_BLOCK, gw), 0)
    qry_i = lax.broadcasted_iota(jnp.int32, (ATT_BLOCK, gw), 1) % ATT_BLOCK
    before = key_i >= qry_i
    after = key_i <= qry_i
    everything = key_i >= 0
    past_first = s > 0
    before_last = s < n_steps - 1
    blocks_per_step = ATT_STEP // ATT_BLOCK
    key_lists = []
    for b in range(blocks_per_step):
        prev = (kc_ref, vc_ref, (b - 1) * ATT_BLOCK) if b > 0 else (kp_ref, vp_ref, 0)
        cur = (kc_ref, vc_ref, b * ATT_BLOCK)
        nxt = (kc_ref, vc_ref, (b + 1) * ATT_BLOCK) if b + 1 < blocks_per_step else (kn_ref, vn_ref, 0)
        m_prev = jnp.logical_and(before, past_first) if b <= ctx_blocks else before
        m_cur = jnp.logical_and(everything, past_first) if b < ctx_blocks else None
        m_next = jnp.logical_and(after, past_first) if b < ctx_blocks else after
        if b + 1 == blocks_per_step:
            m_next = jnp.logical_and(after, before_last)
        key_lists.append([prev + (m_prev,), cur + (m_cur,), nxt + (m_next,)] + ctx_keys)
    _attn_blocks(q_ref, sink_ref, stab_ref, o_ref, key_lists)


def _window_attn(layer, q, k, vt, sink_lanes, stab_lanes, n_ctx):
    t_tot = q.shape[0]
    stab = [] if stab_lanes is None else [stab_lanes]
    assert t_tot % ATT_STEP == 0
    ns = t_tot // ATT_STEP
    nb = t_tot // ATT_BLOCK
    cb = n_ctx // ATT_BLOCK
    assert cb + 1 < ATT_STEP // ATT_BLOCK
    per = ATT_STEP // ATT_BLOCK
    prev_i = lambda s: jnp.clip(per * s - 1, 0, nb - 1)
    next_i = lambda s: jnp.clip(per * s + per, 0, nb - 1)
    ident = lambda s: s
    kspec = lambda rows, f: pl.BlockSpec((rows, ATT_KV), lambda s: (f(s), 0))
    vspec = lambda rows, f: pl.BlockSpec((ATT_KV, rows), lambda s: (0, f(s)))
    return pl.pallas_call(
        functools.partial(_attn_kernel, n_steps=ns, ctx_blocks=cb),
        grid=(ns,),
        in_specs=[pl.BlockSpec((ATT_STEP, ATT_Q), lambda s: (s, 0)),
                  kspec(ATT_BLOCK, prev_i), kspec(ATT_STEP, ident), kspec(ATT_BLOCK, next_i),
                  pl.BlockSpec((n_ctx, ATT_KV), lambda s: (0, 0)),
                  vspec(ATT_BLOCK, prev_i), vspec(ATT_STEP, ident), vspec(ATT_BLOCK, next_i),
                  pl.BlockSpec((ATT_KV, n_ctx), lambda s: (0, 0)),
                  _layer_spec(sink_lanes, layer)] + [_layer_spec(a, layer) for a in stab],
        out_specs=pl.BlockSpec((ATT_STEP, ATT_Q), lambda s: (s, 0)),
        out_shape=jax.ShapeDtypeStruct((t_tot, ATT_Q), BF16),
        compiler_params=_cparams(1),
        name="window_attn" if stab_lanes is None else "window_attn_bounded",
    )(q, k, k, k, k, vt, vt, vt, vt, sink_lanes, *stab)


def _gla_direction(d, gq_ref, gk_ref, gv_ref, b_ref, o_ref, s_ref):
    nc = CHUNKS_PER_TILE
    rows = lambda c: slice(c * GLA_CHUNK, (c + 1) * GLA_CHUNK)
    bcum = b_ref[...]
    zero_row = jnp.zeros((1, GLA_K), F32)
    if d == 0:
        earlier = lambda c: range(0, c)
        b_start = lambda c: zero_row if c == 0 else bcum[c * GLA_CHUNK - 1:c * GLA_CHUNK]
        b_end = lambda c: bcum[(c + 1) * GLA_CHUNK - 1:(c + 1) * GLA_CHUNK]
        last_chunk = nc - 1
    else:
        earlier = lambda c: range(c + 1, nc)
        b_start = lambda c: zero_row if c == nc - 1 else bcum[(c + 1) * GLA_CHUNK:(c + 1) * GLA_CHUNK + 1]
        b_end = lambda c: bcum[c * GLA_CHUNK:c * GLA_CHUNK + 1]
        last_chunk = 0
    total = b_end(last_chunk)
    gq = gq_ref[...].astype(F32) * (GLA_DK ** -0.5)
    gk = gk_ref[...].astype(F32)
    q_loc, q_tile, k_in, k_out, k_raw = [], [], [], [], []
    for c in range(nc):
        b_loc = bcum[rows(c)] - b_start(c)
        ql = gq[rows(c)] * jnp.exp2(b_loc)
        q_loc.append(ql.astype(BF16))
        q_tile.append((ql * jnp.exp2(b_start(c))).astype(BF16))
        k_in.append((gk[rows(c)] * jnp.exp2(-b_loc)).astype(BF16))
        k_out.append(gk[rows(c)] * jnp.exp2(b_end(c) - bcum[rows(c)]))
        k_raw.append(gk_ref[rows(c), :])
    k_out_bf = [k.astype(BF16) for k in k_out]

    def key_seen_from(c, src):
        if src == c:
            return k_in[c]
        if src not in earlier(c):
            return k_raw[src]
        if abs(src - c) == 1:
            return k_out_bf[src]
        return (k_out[src] * jnp.exp2(b_start(c) - b_end(src))).astype(BF16)

    k_dst = [jnp.concatenate([key_seen_from(c, src) for src in range(nc)], axis=0) for c in range(nc)]
    k_state = jnp.concatenate(
        [k_out_bf[c] if c == last_chunk else (k_out[c] * jnp.exp2(total - b_end(c))).astype(BF16)
         for c in range(nc)], axis=0)
    q_tile = jnp.concatenate(q_tile, axis=0)
    ti = lax.broadcasted_iota(jnp.int32, (TM, TM), 0)
    tj = lax.broadcasted_iota(jnp.int32, (TM, TM), 1)
    causal = (tj <= ti) if d == 0 else (tj >= ti)
    decay = jnp.exp2(total)

    for hh in range(GLA_HEADS):
        kc = slice(hh * GLA_DK, (hh + 1) * GLA_DK)
        vc = slice(hh * GLA_DV, (hh + 1) * GLA_DV)
        a = jnp.concatenate([_dot_nt(q_loc[c][:, kc], k_dst[c][:, kc]) for c in range(nc)], axis=0)
        a = jnp.where(causal, a, 0.0).astype(BF16)
        v = gv_ref[:, vc]
        s_old = s_ref[hh]
        o_ref[:, vc] = (_dot(a, v) + _dot(q_tile[:, kc], s_old.astype(BF16))).astype(BF16)
        decay_col = jnp.broadcast_to(decay[:, kc], (GLA_DK, GLA_DK)).T
        decay_col = jnp.concatenate([decay_col] * (GLA_DV // GLA_DK), axis=1)
        s_ref[hh] = s_old * decay_col + _dot_tn(k_state[:, kc], v)


def _backward_scan_tile(pos, n_tiles):
    return jnp.where(pos == 0, 0, n_tiles - pos)


def _gla_kernel(*refs):
    gqf, gkf, gvf, bf = refs[0:4]
    bwd_in = refs[4:4 + 4 * SUB_TILES]
    of_ref, ob_ref, sf_ref, sb_ref = refs[4 + 4 * SUB_TILES:]

    @pl.when(pl.program_id(0) == 0)
    def _():
        sf_ref[...] = jnp.zeros_like(sf_ref)
        sb_ref[...] = jnp.zeros_like(sb_ref)

    for t in range(SUB_TILES):
        rows = slice(t * TM, (t + 1) * TM)
        _gla_direction(0, gqf.at[rows, :], gkf.at[rows, :], gvf.at[rows, :], bf.at[rows, :],
                       of_ref.at[rows, :], sf_ref)
        gqb, gkb, gvb, bb = bwd_in[4 * t:4 * t + 4]
        _gla_direction(1, gqb, gkb, gvb, bb, ob_ref.at[rows, :], sb_ref)


def _gla_scan(gq, gk, gv, b_fwd, b_bwd):
    t_tot = gq.shape[0]
    nt = t_tot // TM
    bwd_spec = lambda w, t: pl.BlockSpec(
        (TM, w), lambda s: (_backward_scan_tile(SUB_TILES * s + t, nt), 0))
    ins = [_step_spec(GLA_K), _step_spec(GLA_K), _step_spec(GLA_V), _step_spec(GLA_K)]
    args = [gq, gk, gv, b_fwd]
    for t in range(SUB_TILES):
        ins += [bwd_spec(GLA_K, t), bwd_spec(GLA_K, t), bwd_spec(GLA_V, t), bwd_spec(GLA_K, t)]
        args += [gq, gk, gv, b_bwd]
    return pl.pallas_call(
        _gla_kernel,
        grid=(t_tot // (SUB_TILES * TM),),
        in_specs=ins,
        out_specs=[_step_spec(GLA_V), _step_spec(GLA_V)],
        out_shape=[jax.ShapeDtypeStruct((t_tot, GLA_V), BF16)] * 2,
        scratch_shapes=[pltpu.VMEM((GLA_HEADS, GLA_DK, GLA_DV), F32)] * 2,
        compiler_params=_cparams(1),
        name="gla_scan",
    )(*args)


def _merge_kernel(*refs, n_x):
    x_refs = refs[:n_x]
    mod_ref, g1_ref, wg_ref, attn_ref, of_ref = refs[n_x:n_x + 5]
    ob_refs = refs[n_x + 5:n_x + 5 + SUB_TILES]
    gain_ref, wa_ref, wl_ref, wo_ref, o_ref = refs[n_x + 5 + SUB_TILES:]
    gain = gain_ref[...]
    for t in range(SUB_TILES):
        rows = slice(t * TM, (t + 1) * TM)
        mod = _tile_mod(mod_ref, t)
        x = _tile_input(x_refs, t)
        h = _norm_modulate(x, g1_ref[...], mod[:, 0:D_MODEL], mod[:, D_MODEL:2 * D_MODEL]).astype(BF16)
        parts = []
        for hh in range(GLA_HEADS):
            vc = slice(hh * GLA_DV, (hh + 1) * GLA_DV)
            o = of_ref[rows, vc].astype(F32) + ob_refs[t][:, vc].astype(F32)
            ms = jnp.mean(o * o, axis=-1, keepdims=True)
            on = o * lax.rsqrt(ms + EPS) * gain
            gr = _dot(h, wg_ref[:, vc])
            parts.append((on * (gr * jax.nn.sigmoid(gr))).astype(BF16))
        gla = jnp.concatenate(parts, axis=1)
        gate_a = jax.nn.sigmoid(_dot(h, wg_ref[:, GLA_V:GLA_V + D_MODEL]))
        gate_b = jax.nn.sigmoid(_dot(h, wg_ref[:, GLA_V + D_MODEL:GLA_V + 2 * D_MODEL]))
        y = gate_a * _dot(attn_ref[rows, :], wa_ref[...]) + gate_b * _dot(gla, wl_ref[...])
        gt1 = mod[:, 2 * D_MODEL:3 * D_MODEL]
        o_ref[rows, :] = x + gt1 * _dot(y.astype(BF16), wo_ref[...])


def _merge_out(layer, xs, mod, g1, w_g, attn, o_f, o_b, gain, wa, wl, wo):
    t_tot = sum(a.shape[0] for a in xs)
    nt = t_tot // TM
    x_args = _tile_input_args(xs)
    ob_spec = lambda t: pl.BlockSpec((TM, GLA_V), lambda i: (_backward_scan_tile(SUB_TILES * i + t, nt), 0))
    return pl.pallas_call(
        functools.partial(_merge_kernel, n_x=len(x_args)),
        grid=(t_tot // (SUB_TILES * TM),),
        in_specs=_tile_input_specs(xs) + [
            _mod_pair_spec(layer), _layer_spec(g1, layer), _layer_spec(w_g, layer),
            _step_spec(ATT_Q), _step_spec(GLA_V)] + [ob_spec(t) for t in range(SUB_TILES)] + [
            _layer_spec(gain, layer), _layer_spec(wa, layer), _layer_spec(wl, layer), _layer_spec(wo, layer)],
        out_specs=_step_spec(D_MODEL),
        out_shape=jax.ShapeDtypeStruct((t_tot, D_MODEL), F32),
        compiler_params=_cparams(1),
        name="merge_out",
    )(*x_args, mod, g1, w_g, attn, o_f, *([o_b] * SUB_TILES), gain, wa, wl, wo)


FF_CHUNK = 1024


def _mlp_kernel(*refs, latent_only):
    x_refs = refs[:-5]
    mod_ref, g2_ref, w1_ref, w2_ref, o_ref = refs[-5:]
    for t in range(SUB_TILES):
        rows = slice(t * TM, (t + 1) * TM)
        mod = mod_ref[0] if latent_only else _tile_mod(mod_ref, t)
        x = x_refs[t][...] if latent_only else x_refs[0][rows, :]
        h = _norm_modulate(x, g2_ref[...], mod[:, 3 * D_MODEL:4 * D_MODEL], mod[:, 4 * D_MODEL:5 * D_MODEL])
        h = h.astype(BF16)
        acc = jnp.zeros((TM, D_MODEL), F32)
        for j in range(D_FF // FF_CHUNK):
            cs = slice(j * FF_CHUNK, (j + 1) * FF_CHUNK)
            u = jnp.maximum(_dot(h, w1_ref[:, cs]), 0.0)
            acc = acc + _dot((u * u).astype(BF16), w2_ref[cs, :])
        o_ref[rows, :] = x + mod[:, 5 * D_MODEL:6 * D_MODEL] * acc


def _mlp(layer, xc, mod, g2, w1, w2, latent_only):
    t_tot = xc.shape[0]
    nt = t_tot // TM
    if latent_only:
        out_rows = t_tot - TM
        lat = lambda t: pl.BlockSpec((TM, D_MODEL), lambda i: (jnp.minimum(SUB_TILES * i + t + 1, nt - 1), 0))
        x_specs, x_args = [lat(t) for t in range(SUB_TILES)], [xc] * SUB_TILES
    else:
        out_rows = t_tot
        x_specs, x_args = [_step_spec(D_MODEL)], [xc]
    return pl.pallas_call(
        functools.partial(_mlp_kernel, latent_only=latent_only),
        grid=(pl.cdiv(out_rows, SUB_TILES * TM),),
        in_specs=x_specs + [_mod_pair_spec(layer), _layer_spec(g2, layer),
                            _layer_spec(w1, layer), _layer_spec(w2, layer)],
        out_specs=_step_spec(D_MODEL),
        out_shape=jax.ShapeDtypeStruct((out_rows, D_MODEL), F32),
        compiler_params=_cparams(1),
        name="mlp",
    )(*x_args, mod, g2, w1, w2)


def _rope_tables(n_lat, n_ctx):
    half = ROPE_AXIS_DIM // 2
    lane = np.arange(LANES)
    dd = lane % HEAD_DIM
    inv = (ROPE_BASE ** (-(dd % half).astype(np.float64) * 2.0 / ROPE_AXIS_DIM)).astype(np.float32)
    is_row = dd < ROPE_AXIS_DIM
    sign = np.where((dd % ROPE_AXIS_DIM) < half, -1.0, 1.0)
    rows = n_lat // GRID_W
    ang_r = np.arange(rows, dtype=np.float32).astype(np.float64)[:, None] * inv[None, :]
    ang_c = np.arange(GRID_W, dtype=np.float32).astype(np.float64)[:, None] * inv[None, :]
    m = jnp.asarray(is_row)[None, None, :]
    pick = lambda fr, fc: jnp.where(m, jnp.asarray(fr, F32)[:, None, :], jnp.asarray(fc, F32)[None, :, :])
    cos = pick(np.cos(ang_r), np.cos(ang_c)).reshape(n_lat, LANES)
    sin = pick(np.sin(ang_r) * sign, np.sin(ang_c) * sign).reshape(n_lat, LANES)
    cos = jnp.concatenate([jnp.ones((n_ctx, LANES), F32), cos], axis=0)
    sin = jnp.concatenate([jnp.zeros((n_ctx, LANES), F32), sin], axis=0)
    return cos, sin


def _tile_cumsum_matrices():
    i = np.arange(TM)[:, None]
    j = np.arange(TM)[None, :]
    return jnp.asarray(np.stack([j <= i, j >= i]).astype(np.float32) / GLA_NORMALIZER, BF16)


def kernel(x, c, ctx, c_ctx, w_mod, b_mod, g_norm1, w_in, q_gain, k_gain, sink, w_decay, b_decay,
           gla_gain, w_branch_attn, w_branch_gla, w_out, g_norm2, w_ff1, w_ff2):
    assert x.shape[0] == 1 and ctx.shape[0] == 1
    n_lat, n_ctx = x.shape[1], ctx.shape[1]
    assert n_ctx == TM and (n_ctx + n_lat) % (SUB_TILES * TM) == 0
    depth = w_mod.shape[0]

    cvec = jnp.zeros((MOD_ROWS, D_MODEL), F32).at[0].set(c[0]).at[1].set(c_ctx)
    mod = _mod_vectors(cvec, w_mod, b_mod).reshape(depth * MOD_ROWS, 1, N_MOD * D_MODEL)
    cos, sin = _rope_tables(n_lat, n_ctx)
    tri = _tile_cumsum_matrices()

    w_in_bf = w_in.astype(BF16)
    w_g = jnp.concatenate([w_in_bf[:, :, C_GR:C_GA], w_in_bf[:, :, C_MERGE:]], axis=2)
    wa = w_branch_attn.astype(BF16)
    wl = w_branch_gla.astype(BF16)
    wo = w_out.astype(BF16)
    w1 = w_ff1.astype(BF16)
    w2 = w_ff2.astype(BF16)
    wd = jnp.zeros((depth, 2, LANES, GLA_K), F32)
    wd = wd.at[:, 0, 0:GLA_RANK].set(w_decay[:, 0]).at[:, 1, GLA_RANK:2 * GLA_RANK].set(w_decay[:, 1]).astype(BF16)
    bd = b_decay.reshape(depth, 2, 1, GLA_K)
    g1 = g_norm1.reshape(depth, 1, D_MODEL)
    g2 = g_norm2.reshape(depth, 1, D_MODEL)
    qg = jnp.tile(q_gain, (1, LANES // HEAD_DIM)).reshape(depth, 1, LANES)
    kg = jnp.tile(k_gain, (1, LANES // HEAD_DIM)).reshape(depth, 1, LANES)
    gain = gla_gain.reshape(depth, 1, GLA_DV)
    sink_lanes = jnp.repeat((sink * LOG2E).reshape(depth, N_KV_HEADS, 1, GROUP), ATT_BLOCK, axis=3)
    logit_bound = (HEAD_DIM * Q_SCALE * BOUND_MARGIN) * jnp.max(jnp.abs(q_gain), axis=1) * jnp.max(jnp.abs(k_gain), axis=1)
    stab_lanes = jnp.maximum(sink_lanes, logit_bound.reshape(depth, 1, 1, 1))

    xs = (ctx[0], x[0])
    for l in range(depth):
        q, k, vt, gq, gk, gv, b_fwd, b_bwd = _in_proj(l, xs, mod, g1, w_in_bf, qg, kg, cos, sin, wd, bd, tri)
        attn = lax.cond(
            logit_bound[l] <= MAX_SAFE_BOUND,
            lambda *a: _window_attn(l, *a, stab_lanes, n_ctx),
            lambda *a: _window_attn(l, *a, None, n_ctx),
            q, k, vt, sink_lanes)
        o_f, o_b = _gla_scan(gq, gk, gv, b_fwd, b_bwd)
        xc = _merge_out(l, xs, mod, g1, w_g, attn, o_f, o_b, gain, wa, wl, wo)
        xc = _mlp(l, xc, mod, g2, w1, w2, latent_only=(l == depth - 1))
        xs = (xc,)
    return xc[None]
```
